```python
import jax, jax.numpy as jnp
from jax import lax
import numpy as np

D_MODEL = 1024
BATCH = 2
SEQ = 16384
DEPTH = 4

CHUNK = 64
Q_BLOCK = 128
RMS_EPS = 1e-6
D_FF = 2816
POOL_WINDOWS = (2, 4, 8, 16)
POOL_GROUP = D_MODEL // 8
POOL_WIDTH = POOL_GROUP * len(POOL_WINDOWS)
FOX_HEADS = 8
FOX_HEAD_DIM = D_MODEL // (2 * FOX_HEADS)
FOX_WIDTH = FOX_HEADS * FOX_HEAD_DIM
AB_IN = POOL_WIDTH + 3 * FOX_WIDTH + FOX_HEADS
AB_MIX = POOL_WIDTH + FOX_WIDTH
MLA_HEADS = 16
MLA_NOPE = 64
MLA_ROPE = 32
MLA_V = 64
MLA_Q_LORA = 256
MLA_KV_LORA = 128
MLA_IN = MLA_Q_LORA + MLA_KV_LORA + MLA_ROPE
ROPE_THETA = 10000.0
N_EVEN = (DEPTH + 1) // 2
N_ODD = DEPTH // 2

kernel_name = 'hybrid_pool_fox_mla_macaron_encoder'


def rms_norm(x, g):
    xf = x.astype(jnp.float32)
    y = xf * lax.rsqrt(jnp.mean(xf * xf, axis=-1, keepdims=True) + RMS_EPS)
    return (y * g.astype(jnp.float32)).astype(x.dtype)


def swiglu(x, w_gate, w_up, w_down):
    return (jax.nn.silu(x @ w_gate) * (x @ w_up)) @ w_down


def rope_tables(positions):
    inv_freq = ROPE_THETA ** (-jnp.arange(0, MLA_ROPE, 2, dtype=jnp.float32) / MLA_ROPE)
    ang = positions.astype(jnp.float32)[..., None] * inv_freq
    return jnp.cos(ang), jnp.sin(ang)


def apply_rope(x, cos, sin):
    xf = x.astype(jnp.float32)
    half = xf.shape[-1] // 2
    x1, x2 = xf[..., :half], xf[..., half:]
    return jnp.concatenate([x1 * cos - x2 * sin, x2 * cos + x1 * sin], axis=-1).astype(x.dtype)


def blocked_attention(q, k, v, scale, chunk, log_decay=None):
    b, s, h, dk = q.shape
    dv = v.shape[-1]
    nb = s // Q_BLOCK
    q = q * scale
    qb = q.reshape(b, nb, Q_BLOCK, h, dk).transpose(1, 0, 2, 3, 4)
    k_chunk = jnp.arange(s) // chunk
    xs = (qb, jnp.arange(nb))
    if log_decay is not None:
        fk = log_decay.transpose(0, 2, 1)
        fqb = log_decay.reshape(b, nb, Q_BLOCK, h).transpose(1, 0, 3, 2)
        xs = xs + (fqb,)

    def attend(args):
        q_blk, blk = args[0], args[1]
        logits = jnp.einsum('bqhd,bkhd->bhqk', q_blk, k).astype(jnp.float32)
        if log_decay is not None:
            logits = logits + (args[2][..., None] - fk[:, :, None, :])
        t_pos = blk * Q_BLOCK + jnp.arange(Q_BLOCK)
        allowed = k_chunk[None, :] <= (t_pos // chunk)[:, None]
        logits = jnp.where(allowed, logits, -jnp.inf)
        p = jax.nn.softmax(logits, axis=-1)
        return jnp.einsum('bhqk,bkhd->bqhd', p.astype(v.dtype), v)

    out = lax.map(attend, xs)
    return out.transpose(1, 0, 2, 3, 4).reshape(b, s, h, dv)


def pool_mixer(u, w_pool, pool_scale):
    s = u.shape[1]
    cs = jnp.cumsum(u.astype(jnp.float32), axis=1)
    t_count = jnp.arange(s) + 1
    outs = []
    for g, w in enumerate(POOL_WINDOWS):
        lo, hi = g * POOL_GROUP, (g + 1) * POOL_GROUP
        cs_g = cs[..., lo:hi]
        lagged = jnp.pad(cs_g, ((0, 0), (w, 0), (0, 0)))[:, :s]
        count = jnp.minimum(t_count, w).astype(jnp.float32)[None, :, None]
        diff = (cs_g - lagged) / count - u[..., lo:hi].astype(jnp.float32)
        outs.append(diff.astype(u.dtype) @ w_pool[g])
    return jnp.concatenate(outs, axis=-1) * pool_scale


def pool_fox_mixer(h, w_in, b_forget, w_pool, pool_scale, w_out):
    b, s, _ = h.shape
    proj = h @ w_in
    o1 = POOL_WIDTH
    o2 = o1 + FOX_WIDTH
    o3 = o2 + FOX_WIDTH
    o4 = o3 + FOX_WIDTH
    u = proj[..., :o1]
    q = proj[..., o1:o2].reshape(b, s, FOX_HEADS, FOX_HEAD_DIM)
    k = proj[..., o2:o3].reshape(b, s, FOX_HEADS, FOX_HEAD_DIM)
    v = proj[..., o3:o4].reshape(b, s, FOX_HEADS, FOX_HEAD_DIM)
    f_logit = proj[..., o4:] + b_forget
    y_pool = pool_mixer(u, w_pool, pool_scale)
    log_f = jax.nn.log_sigmoid(f_logit.astype(jnp.float32))
    cum_log_f = jnp.cumsum(log_f, axis=1)
    y_fox = blocked_attention(q, k, v, FOX_HEAD_DIM ** -0.5, 1, cum_log_f)
    y = jnp.concatenate([y_pool, y_fox.reshape(b, s, FOX_WIDTH)], axis=-1)
    return y @ w_out


def mla_mixer(h, cos, sin, w_in, q_norm, kv_norm, w_q_b, w_kv_b, w_out):
    b, s, _ = h.shape
    proj = h @ w_in
    c_q = proj[..., :MLA_Q_LORA]
    c_kv = proj[..., MLA_Q_LORA:MLA_Q_LORA + MLA_KV_LORA]
    k_rope = apply_rope(proj[..., MLA_Q_LORA + MLA_KV_LORA:], cos, sin)
    q = (rms_norm(c_q, q_norm) @ w_q_b).reshape(b, s, MLA_HEADS, MLA_NOPE + MLA_ROPE)
    q_rope = apply_rope(q[..., MLA_NOPE:], cos[:, :, None, :], sin[:, :, None, :])
    qk = jnp.concatenate([q[..., :MLA_NOPE], q_rope], axis=-1)
    kv = (rms_norm(c_kv, kv_norm) @ w_kv_b).reshape(b, s, MLA_HEADS, MLA_NOPE + MLA_V)
    k = jnp.concatenate(
        [kv[..., :MLA_NOPE], jnp.broadcast_to(k_rope[:, :, None, :], (b, s, MLA_HEADS, MLA_ROPE))],
        axis=-1)
    v = kv[..., MLA_NOPE:]
    y = blocked_attention(qk, k, v, (MLA_NOPE + MLA_ROPE) ** -0.5, CHUNK)
    return y.reshape(b, s, MLA_HEADS * MLA_V) @ w_out


def setup_inputs(seed: int = 0) -> dict:
    key = jax.random.key(seed)
    ks = jax.random.split(key, 20)

    def normal(k, shape, scale):
        return scale * jax.random.normal(k, shape, jnp.float32)

    x = normal(ks[0], (BATCH, SEQ, D_MODEL), 1.0)
    start = jax.random.randint(ks[1], (BATCH, 1), 0, 64, dtype=jnp.int32) * CHUNK
    positions = (start + jnp.arange(SEQ, dtype=jnp.int32)[None, :]).astype(jnp.int32)
    norm_ffn = 1.0 + normal(ks[2], (DEPTH, 2, D_MODEL), 0.05)
    norm_mix = 1.0 + normal(ks[3], (DEPTH, D_MODEL), 0.05)
    norm_final = 1.0 + normal(ks[4], (D_MODEL,), 0.05)
    ffn_w_gate = normal(ks[5], (DEPTH, 2, D_MODEL, D_FF), D_MODEL ** -0.5)
    ffn_w_up = normal(ks[6], (DEPTH, 2, D_MODEL, D_FF), D_MODEL ** -0.5)
    ffn_w_down = normal(ks[7], (DEPTH, 2, D_FF, D_MODEL), D_FF ** -0.5)
    ab_w_in = normal(ks[8], (N_EVEN, D_MODEL, AB_IN), D_MODEL ** -0.5)
    ab_b_forget = jax.random.uniform(ks[9], (N_EVEN, FOX_HEADS), jnp.float32, 1.0, 6.0)
    pool_w = normal(ks[10], (N_EVEN, len(POOL_WINDOWS), POOL_GROUP, POOL_GROUP), POOL_GROUP ** -0.5)
    pool_scale = 1.0 + normal(ks[11], (N_EVEN, POOL_WIDTH), 0.1)
    ab_w_out = normal(ks[12], (N_EVEN, AB_MIX, D_MODEL), AB_MIX ** -0.5)
    mla_w_in = normal(ks[13], (N_ODD, D_MODEL, MLA_IN), D_MODEL ** -0.5)
    mla_q_norm = 1.0 + normal(ks[14], (N_ODD, MLA_Q_LORA), 0.05)
    mla_kv_norm = 1.0 + normal(ks[15], (N_ODD, MLA_KV_LORA), 0.05)
    mla_w_q_b = normal(ks[16], (N_ODD, MLA_Q_LORA, MLA_HEADS * (MLA_NOPE + MLA_ROPE)), MLA_Q_LORA ** -0.5)
    mla_w_kv_b = normal(ks[17], (N_ODD, MLA_KV_LORA, MLA_HEADS * (MLA_NOPE + MLA_V)), MLA_KV_LORA ** -0.5)
    mla_w_out = normal(ks[18], (N_ODD, MLA_HEADS * MLA_V, D_MODEL), (MLA_HEADS * MLA_V) ** -0.5)
    return {'x': x, 'positions': positions, 'norm_ffn': norm_ffn, 'norm_mix': norm_mix,
            'norm_final': norm_final, 'ffn_w_gate': ffn_w_gate, 'ffn_w_up': ffn_w_up,
            'ffn_w_down': ffn_w_down, 'ab_w_in': ab_w_in, 'ab_b_forget': ab_b_forget,
            'pool_w': pool_w, 'pool_scale': pool_scale, 'ab_w_out': ab_w_out,
            'mla_w_in': mla_w_in, 'mla_q_norm': mla_q_norm, 'mla_kv_norm': mla_kv_norm,
            'mla_w_q_b': mla_w_q_b, 'mla_w_kv_b': mla_w_kv_b, 'mla_w_out': mla_w_out}


def reference(x, positions, norm_ffn, norm_mix, norm_final, ffn_w_gate, ffn_w_up, ffn_w_down,
              ab_w_in, ab_b_forget, pool_w, pool_scale, ab_w_out,
              mla_w_in, mla_q_norm, mla_kv_norm, mla_w_q_b, mla_w_kv_b, mla_w_out):
    cos, sin = rope_tables(positions)
    h = x
    for layer in range(DEPTH):
        h = h + 0.5 * swiglu(rms_norm(h, norm_ffn[layer, 0]), ffn_w_gate[layer, 0],
                             ffn_w_up[layer, 0], ffn_w_down[layer, 0])
        hn = rms_norm(h, norm_mix[layer])
        i = layer // 2
        if layer % 2 == 0:
            h = h + pool_fox_mixer(hn, ab_w_in[i], ab_b_forget[i], pool_w[i], pool_scale[i], ab_w_out[i])
        else:
            h = h + mla_mixer(hn, cos, sin, mla_w_in[i], mla_q_norm[i], mla_kv_norm[i],
                              mla_w_q_b[i], mla_w_kv_b[i], mla_w_out[i])
        h = h + 0.5 * swiglu(rms_norm(h, norm_ffn[layer, 1]), ffn_w_gate[layer, 1],
                             ffn_w_up[layer, 1], ffn_w_down[layer, 1])
    return rms_norm(h, norm_final)
```

```python
import functools
import math

import jax
import jax.numpy as jnp
from jax import lax
from jax.experimental import pallas as pl
from jax.experimental.pallas import tpu as pltpu

F32 = jnp.float32
BF16 = jnp.bfloat16

D_MODEL = 1024
DEPTH = 4
CHUNK = 64
RMS_EPS = 1e-6
D_FF = 2816
POOL_WINDOWS = (2, 4, 8, 16)
POOL_GROUP = 128
POOL_WIDTH = 512
FOX_HEADS = 8
FOX_HEAD_DIM = 64
FOX_WIDTH = 512
MLA_HEADS = 16
MLA_NOPE = 64
MLA_ROPE = 32
MLA_V = 64
MLA_Q_LORA = 256
MLA_KV_LORA = 128
ROPE_THETA = 10000.0

LANES = 128
SUBLANES = 8
VMEM_LIMIT_BYTES = 56 * 1024 * 1024

FFN_TM = 512
FFN_FC = 256
FFN_NC = D_FF // FFN_FC
PROJ_TM = 512
ATT_TQ = 512
ATT_TK = 512
HEADS_PER_STEP = 2
POOL_HALO = 16

LOG2E = math.log2(math.e)
MASKED = -1e30


def _rms(x, g):
    return x * lax.rsqrt(jnp.mean(x * x, axis=-1, keepdims=True) + RMS_EPS) * g


def _params(*sem):
    return pltpu.CompilerParams(dimension_semantics=sem, vmem_limit_bytes=VMEM_LIMIT_BYTES)


def _ffn_kernel(h_ref, g_ref, wgu_ref, wd_ref, o_ref, xn_ref, acc_ref):
    h = h_ref[...]
    xn_ref[...] = _rms(h, g_ref[...]).astype(BF16)
    acc_ref[...] = jnp.zeros_like(acc_ref)

    def chunk(c, carry):
        gu = jnp.dot(xn_ref[...], wgu_ref[c], preferred_element_type=F32)
        gate = gu[:, :FFN_FC]
        up = gu[:, FFN_FC:]
        act = (gate * jax.nn.sigmoid(gate) * up).astype(BF16)
        acc_ref[...] += jnp.dot(act, wd_ref[c], preferred_element_type=F32)
        return carry

    lax.fori_loop(0, FFN_NC, chunk, 0)
    o_ref[...] = h + 0.5 * acc_ref[...]


def _ffn(h2d, g, wgu, wd):
    t = h2d.shape[0]
    return pl.pallas_call(
        _ffn_kernel,
        name="ffn",
        grid=(t // FFN_TM,),
        in_specs=[
            pl.BlockSpec((FFN_TM, D_MODEL), lambda i: (i, 0)),
            pl.BlockSpec((1, D_MODEL), lambda i: (0, 0)),
            pl.BlockSpec((FFN_NC, D_MODEL, 2 * FFN_FC), lambda i: (0, 0, 0)),
            pl.BlockSpec((FFN_NC, FFN_FC, D_MODEL), lambda i: (0, 0, 0)),
        ],
        out_specs=pl.BlockSpec((FFN_TM, D_MODEL), lambda i: (i, 0)),
        out_shape=jax.ShapeDtypeStruct(h2d.shape, F32),
        scratch_shapes=[
            pltpu.VMEM((FFN_TM, D_MODEL), BF16),
            pltpu.VMEM((FFN_TM, D_MODEL), F32),
        ],
        compiler_params=_params("parallel"),
    )(h2d, g, wgu, wd)


def _prep_ffn(w_gate, w_up, w_down):
    wg = w_gate.astype(BF16).reshape(D_MODEL, FFN_NC, FFN_FC)
    wu = w_up.astype(BF16).reshape(D_MODEL, FFN_NC, FFN_FC)
    wgu = jnp.concatenate([wg, wu], axis=-1).transpose(1, 0, 2)
    wd = w_down.astype(BF16).reshape(FFN_NC, FFN_FC, D_MODEL)
    return wgu, wd


def _ab_in_kernel(h_ref, g_ref, wu_ref, wq_ref, wk_ref, wv_ref, wf_ref, bf_ref, tri_ref,
                  pq_ref, pk_ref, vone_ref, wpool_ref, pscale_ref,
                  ypool_ref, q_ref, k_ref, v_ref, halo_ref, fcarry_ref):
    i = pl.program_id(1)
    tm = h_ref.shape[1]

    @pl.when(i == 0)
    def _():
        halo_ref[...] = jnp.zeros_like(halo_ref)
        fcarry_ref[...] = jnp.zeros_like(fcarry_ref)

    hn = _rms(h_ref[0], g_ref[...]).astype(BF16)

    logit = jnp.dot(hn, wf_ref[...], preferred_element_type=F32) + bf_ref[...]
    log_f = jnp.minimum(logit, 0.0) - jnp.log1p(jnp.exp(-jnp.abs(logit)))
    tri = tri_ref[...]

    def split3(x):
        hi = x.astype(BF16)
        r1 = x - hi.astype(F32)
        mid = r1.astype(BF16)
        lo = (r1 - mid.astype(F32)).astype(BF16)
        return hi, mid, lo

    hi, mid, lo = split3(log_f)
    csum = (jnp.dot(tri, hi, preferred_element_type=F32)
            + jnp.dot(tri, mid, preferred_element_type=F32)
            + jnp.dot(tri, lo, preferred_element_type=F32))
    cum_f = csum + fcarry_ref[0:1, :]
    fcarry_ref[...] = jnp.broadcast_to(cum_f[tm - 1:tm, :], fcarry_ref.shape)

    fh, fm, fl = (x.astype(F32) for x in split3(cum_f * LOG2E))
    lane = lax.broadcasted_iota(jnp.int32, fh.shape, 1)
    xterms = jnp.where(lane < 8, fh, jnp.where(lane < 16, fm, jnp.where(
        lane < 24, fl, jnp.where(lane == 24, 1.0, 0.0)))).astype(BF16)

    q = jnp.dot(hn, wq_ref[...], preferred_element_type=F32) * (FOX_HEAD_DIM ** -0.5 * LOG2E)
    q = q + jnp.dot(xterms, pq_ref[...], preferred_element_type=F32)
    q_ref[0] = q.astype(BF16)
    k = jnp.dot(hn, wk_ref[...], preferred_element_type=F32)
    k = k + jnp.dot(xterms, pk_ref[...], preferred_element_type=F32)
    k_ref[0] = k.astype(BF16)
    v = jnp.dot(hn, wv_ref[...], preferred_element_type=F32) + vone_ref[...]
    v_ref[0] = v.astype(BF16)

    u = jnp.dot(hn, wu_ref[...], preferred_element_type=F32)
    ext = jnp.concatenate([halo_ref[...], u], axis=0)
    halo_ref[...] = u[tm - POOL_HALO:, :]
    t_pos = i * tm + lax.broadcasted_iota(jnp.int32, (tm, POOL_GROUP), 0)
    sums = ext
    outs = []
    for g, w in enumerate(POOL_WINDOWS):
        sums = sums + pltpu.roll(sums, w // 2, axis=0)
        win = sums[POOL_HALO:, :POOL_GROUP]
        count = jnp.minimum(t_pos + 1, w).astype(F32)
        diff = win / count - u[:, g * POOL_GROUP:(g + 1) * POOL_GROUP]
        outs.append(jnp.dot(diff.astype(BF16), wpool_ref[g], preferred_element_type=F32))
        if g + 1 < len(POOL_WINDOWS):
            sums = sums[:, POOL_GROUP:]
    y = jnp.concatenate(outs, axis=-1) * pscale_ref[...]
    ypool_ref[0] = y.astype(BF16)


def _ab_in(h, g, w):
    b, s, _ = h.shape
    tm = PROJ_TM
    hw = FOX_HEADS * LANES
    const2 = lambda shape: pl.BlockSpec(shape, lambda bi, i: (0,) * len(shape))
    tok = lambda width: pl.BlockSpec((1, tm, width), lambda bi, i: (bi, i, 0))
    return pl.pallas_call(
        _ab_in_kernel,
        name="ab_in",
        grid=(b, s // tm),
        in_specs=[
            tok(D_MODEL), const2((1, D_MODEL)),
            const2((D_MODEL, POOL_WIDTH)), const2((D_MODEL, hw)), const2((D_MODEL, hw)),
            const2((D_MODEL, hw)), const2((D_MODEL, LANES)), const2((1, LANES)),
            const2((tm, tm)), const2((LANES, hw)), const2((LANES, hw)), const2((1, hw)),
            const2((len(POOL_WINDOWS), POOL_GROUP, POOL_GROUP)), const2((1, POOL_WIDTH)),
        ],
        out_specs=[tok(POOL_WIDTH), tok(hw), tok(hw), tok(hw)],
        out_shape=[
            jax.ShapeDtypeStruct((b, s, POOL_WIDTH), BF16),
            jax.ShapeDtypeStruct((b, s, hw), BF16),
            jax.ShapeDtypeStruct((b, s, hw), BF16),
            jax.ShapeDtypeStruct((b, s, hw), BF16),
        ],
        scratch_shapes=[
            pltpu.VMEM((POOL_HALO, POOL_WIDTH), F32),
            pltpu.VMEM((SUBLANES, LANES), F32),
        ],
        compiler_params=_params("arbitrary", "arbitrary"),
    )(h, g, w["wu"], w["wq"], w["wk"], w["wv"], w["wf"], w["bf"], w["tri"],
      w["pq"], w["pk"], w["vone"], w["wpool"], w["pscale"])


def _head_groups(w, heads, width):
    rows = w.shape[0]
    w = w.reshape(rows, heads, width)
    w = jnp.pad(w, ((0, 0), (0, 0), (0, LANES - width)))
    return w.reshape(rows, heads * LANES)


def _prep_ab(w_in, b_forget, w_pool, pool_scale, w_out):
    o1, o2, o3, o4 = POOL_WIDTH, POOL_WIDTH + FOX_WIDTH, POOL_WIDTH + 2 * FOX_WIDTH, POOL_WIDTH + 3 * FOX_WIDTH
    wb = w_in.astype(BF16)
    hw = FOX_HEADS * LANES
    wf = jnp.pad(jnp.tile(wb[:, o4:], (1, 3)), ((0, 0), (0, LANES - 3 * FOX_HEADS)))
    bf = jnp.pad(jnp.tile(b_forget.astype(F32), 3), (0, LANES - 3 * FOX_HEADS))[None, :]
    r = jnp.arange(LANES)[:, None]
    c = jnp.arange(hw)[None, :]
    head, lane = c // LANES, c % LANES
    is_term = r < 3 * FOX_HEADS
    pq = jnp.where(is_term & (head == r % FOX_HEADS) & (lane == FOX_HEAD_DIM + r // FOX_HEADS), 1.0, 0.0)
    pq = pq + jnp.where((r == 3 * FOX_HEADS) & (lane >= FOX_HEAD_DIM + 3) & (lane < FOX_HEAD_DIM + 6), 1.0, 0.0)
    pk = jnp.where(is_term & (head == r % FOX_HEADS) & (lane == FOX_HEAD_DIM + 3 + r // FOX_HEADS), -1.0, 0.0)
    pk = pk + jnp.where((r == 3 * FOX_HEADS) & (lane >= FOX_HEAD_DIM) & (lane < FOX_HEAD_DIM + 3), 1.0, 0.0)
    vone = jnp.where(lane == FOX_HEAD_DIM, 1.0, 0.0).astype(F32)
    tri = jnp.tril(jnp.ones((PROJ_TM, PROJ_TM), BF16))
    wo = w_out.astype(BF16)
    return {
        "wu": wb[:, :o1],
        "wq": _head_groups(wb[:, o1:o2], FOX_HEADS, FOX_HEAD_DIM),
        "wk": _head_groups(wb[:, o2:o3], FOX_HEADS, FOX_HEAD_DIM),
        "wv": _head_groups(wb[:, o3:o4], FOX_HEADS, FOX_HEAD_DIM),
        "wf": wf, "bf": bf, "tri": tri,
        "pq": pq.astype(BF16), "pk": pk.astype(BF16), "vone": vone,
        "wpool": w_pool.astype(BF16), "pscale": pool_scale.astype(F32)[None, :],
        "wo_pool": wo[:POOL_WIDTH], "wo_fox": wo[POOL_WIDTH:],
    }


def _mla_in_kernel(h_ref, g_ref, win_ref, qn_ref, kvn_ref, wqa_ref, wqb_ref, wk_ref, wv_ref,
                   vone_ref, cos_ref, sin_ref, q_ref, k_ref, v_ref):
    hn = _rms(h_ref[0], g_ref[...]).astype(BF16)
    proj = jnp.dot(hn, win_ref[...], preferred_element_type=F32)
    c_q = proj[:, :MLA_Q_LORA]
    c_kv = proj[:, MLA_Q_LORA:MLA_Q_LORA + MLA_KV_LORA]
    kr_a = proj[:, MLA_Q_LORA + MLA_KV_LORA:MLA_Q_LORA + MLA_KV_LORA + LANES]
    kr_b = proj[:, MLA_Q_LORA + MLA_KV_LORA + LANES:]
    cos = cos_ref[0]
    sin = sin_ref[0]
    k_rope = kr_a * cos + kr_b * sin

    qn = _rms(c_q, qn_ref[...]).astype(BF16)
    kvn = _rms(c_kv, kvn_ref[...]).astype(BF16)

    scale = (MLA_NOPE + MLA_ROPE) ** -0.5 * LOG2E
    lane = lax.broadcasted_iota(jnp.int32, cos.shape, 1)
    cos_q = (cos + jnp.where(lane < MLA_NOPE, 1.0, 0.0)) * scale
    sin_q = sin * scale
    q_a = jnp.dot(qn, wqa_ref[...], preferred_element_type=F32)
    q_b = jnp.dot(qn, wqb_ref[...], preferred_element_type=F32)
    k_all = jnp.dot(kvn, wk_ref[...], preferred_element_type=F32)
    v_all = jnp.dot(kvn, wv_ref[...], preferred_element_type=F32) + vone_ref[...]
    for hd in range(MLA_HEADS):
        grp = slice(hd * LANES, (hd + 1) * LANES)
        q_ref[0, :, grp] = (q_a[:, grp] * cos_q + q_b[:, grp] * sin_q).astype(BF16)
        k_ref[0, :, grp] = (k_all[:, grp] + k_rope).astype(BF16)
    v_ref[0] = v_all.astype(BF16)


def _mla_in(h, g, w, cos_l, sin_l):
    b, s, _ = h.shape
    tm = PROJ_TM
    hw = MLA_HEADS * LANES
    nin = MLA_Q_LORA + MLA_KV_LORA + 2 * LANES
    const2 = lambda shape: pl.BlockSpec(shape, lambda bi, i: (0,) * len(shape))
    tok = lambda width: pl.BlockSpec((1, tm, width), lambda bi, i: (bi, i, 0))
    return pl.pallas_call(
        _mla_in_kernel,
        name="mla_in",
        grid=(b, s // tm),
        in_specs=[
            tok(D_MODEL), const2((1, D_MODEL)), const2((D_MODEL, nin)),
            const2((1, MLA_Q_LORA)), const2((1, MLA_KV_LORA)),
            const2((MLA_Q_LORA, hw)), const2((MLA_Q_LORA, hw)),
            const2((MLA_KV_LORA, hw)), const2((MLA_KV_LORA, hw)), const2((1, hw)),
            tok(LANES), tok(LANES),
        ],
        out_specs=[tok(hw), tok(hw), tok(hw)],
        out_shape=[jax.ShapeDtypeStruct((b, s, hw), BF16)] * 3,
        compiler_params=_params("parallel", "parallel"),
    )(h, g, w["win"], w["qn"], w["kvn"], w["wqa"], w["wqb"], w["wk"], w["wv"], w["vone"],
      cos_l, sin_l)


def _rope_group(x1, x2, lead):
    z0 = jnp.zeros(lead + (MLA_NOPE,), x1.dtype)
    z1 = jnp.zeros(lead + (LANES - MLA_NOPE - MLA_ROPE,), x1.dtype)
    return jnp.concatenate([z0, x1, x2, z1], axis=-1)


def _prep_mla(w_in, q_norm, kv_norm, w_q_b, w_kv_b, w_out):
    half = MLA_ROPE // 2
    wb = w_in.astype(BF16)
    kr = wb[:, MLA_Q_LORA + MLA_KV_LORA:]
    a1, a2 = kr[:, :half], kr[:, half:]
    win = jnp.concatenate([
        wb[:, :MLA_Q_LORA + MLA_KV_LORA],
        _rope_group(a1, a2, (D_MODEL,)),
        _rope_group(-a2, a1, (D_MODEL,)),
    ], axis=-1)
    wq = w_q_b.astype(BF16).reshape(MLA_Q_LORA, MLA_HEADS, MLA_NOPE + MLA_ROPE)
    nope, x1, x2 = wq[..., :MLA_NOPE], wq[..., MLA_NOPE:MLA_NOPE + half], wq[..., MLA_NOPE + half:]
    tail = jnp.zeros((MLA_Q_LORA, MLA_HEADS, LANES - MLA_NOPE - MLA_ROPE), BF16)
    wqa = jnp.concatenate([nope, x1, x2, tail], axis=-1).reshape(MLA_Q_LORA, MLA_HEADS * LANES)
    wqb = jnp.concatenate([jnp.zeros_like(nope), -x2, x1, tail], axis=-1).reshape(MLA_Q_LORA, MLA_HEADS * LANES)
    wkv = w_kv_b.astype(BF16).reshape(MLA_KV_LORA, MLA_HEADS, MLA_NOPE + MLA_V)
    pad = jnp.zeros((MLA_KV_LORA, MLA_HEADS, LANES - MLA_NOPE), BF16)
    wk = jnp.concatenate([wkv[..., :MLA_NOPE], pad], axis=-1).reshape(MLA_KV_LORA, MLA_HEADS * LANES)
    wv = jnp.concatenate([wkv[..., MLA_NOPE:], pad], axis=-1).reshape(MLA_KV_LORA, MLA_HEADS * LANES)
    lane = jnp.arange(MLA_HEADS * LANES) % LANES
    vone = jnp.where(lane == MLA_V, 1.0, 0.0).astype(F32)[None, :]
    return {
        "win": win, "qn": q_norm.astype(F32)[None, :], "kvn": kv_norm.astype(F32)[None, :],
        "wqa": wqa, "wqb": wqb, "wk": wk, "wv": wv, "vone": vone, "wo": w_out.astype(BF16),
    }


def _attn_kernel(q_ref, k_ref, v_ref, o_ref, m_ref, acc_ref, *, tq, tk, chunk, head_dim):
    i = pl.program_id(2)
    shift = chunk.bit_length() - 1
    row = lax.broadcasted_iota(jnp.int32, (tq, tk), 0)
    col = lax.broadcasted_iota(jnp.int32, (tq, tk), 1)
    diag_mask = (col >> shift) <= (row >> shift)

    for hh in range(HEADS_PER_STEP):
        grp = slice(hh * LANES, (hh + 1) * LANES)
        q = q_ref[0, :, grp]
        m_ref[...] = jnp.full(m_ref.shape, MASKED, F32)
        acc_ref[...] = jnp.zeros_like(acc_ref)

        def tile(j, masked):
            off = pl.multiple_of(j * tk, tk)
            kt = k_ref[0, pl.ds(off, tk), grp]
            vt = v_ref[0, pl.ds(off, tk), grp]
            s = lax.dot_general(q, kt, (((1,), (1,)), ((), ())), preferred_element_type=F32)
            if masked:
                s = jnp.where(diag_mask, s, MASKED)
            m_prev = m_ref[...]
            m_new = jnp.maximum(m_prev, jnp.max(s, axis=1, keepdims=True))
            alpha = jnp.exp2(m_prev - m_new)
            p = jnp.exp2(s - pltpu.repeat(m_new, tk // LANES, axis=1))
            acc_ref[...] = acc_ref[...] * alpha + jnp.dot(p.astype(BF16), vt, preferred_element_type=F32)
            m_ref[...] = m_new

        def full_tile(j, carry):
            tile(j, False)
            return carry

        lax.fori_loop(0, i, full_tile, 0)
        tile(i, True)

        acc = acc_ref[...]
        out = acc[:, :head_dim] / acc[:, head_dim:head_dim + 1]
        o_ref[0, :, hh * head_dim:(hh + 1) * head_dim] = out.astype(BF16)


def _attention(q, k, v, heads, chunk, head_dim):
    b, s, _ = q.shape
    tq, tk = ATT_TQ, ATT_TK
    assert tq == tk and tq % chunk == 0 and s % tq == 0 and heads % HEADS_PER_STEP == 0
    gw = HEADS_PER_STEP * LANES
    return pl.pallas_call(
        functools.partial(_attn_kernel, tq=tq, tk=tk, chunk=chunk, head_dim=head_dim),
        name="attention",
        grid=(b, heads // HEADS_PER_STEP, s // tq),
        in_specs=[
            pl.BlockSpec((1, tq, gw), lambda bi, hp, i: (bi, i, hp)),
            pl.BlockSpec((1, s, gw), lambda bi, hp, i: (bi, 0, hp)),
            pl.BlockSpec((1, s, gw), lambda bi, hp, i: (bi, 0, hp)),
        ],
        out_specs=pl.BlockSpec((1, tq, HEADS_PER_STEP * head_dim), lambda bi, hp, i: (bi, i, hp)),
        out_shape=jax.ShapeDtypeStruct((b, s, heads * head_dim), BF16),
        scratch_shapes=[
            pltpu.VMEM((tq, LANES), F32),
            pltpu.VMEM((tq, LANES), F32),
        ],
        compiler_params=_params("parallel", "parallel", "arbitrary"),
    )(q, k, v)


def _out_kernel(*refs):
    n = (len(refs) - 2) // 2
    h_ref, o_ref = refs[0], refs[-1]
    acc = h_ref[...]
    for y_ref, w_ref in zip(refs[1:1 + n], refs[1 + n:1 + 2 * n]):
        acc = acc + jnp.dot(y_ref[...], w_ref[...], preferred_element_type=F32)
    o_ref[...] = acc


def _mix_out(h2d, ys, ws):
    t = h2d.shape[0]
    tm = PROJ_TM
    in_specs = [pl.BlockSpec((tm, D_MODEL), lambda i: (i, 0))]
    in_specs += [pl.BlockSpec((tm, y.shape[1]), lambda i: (i, 0)) for y in ys]
    in_specs += [pl.BlockSpec(w.shape, lambda i: (0, 0)) for w in ws]
    return pl.pallas_call(
        _out_kernel,
        name="mix_out",
        grid=(t // tm,),
        in_specs=in_specs,
        out_specs=pl.BlockSpec((tm, D_MODEL), lambda i: (i, 0)),
        out_shape=jax.ShapeDtypeStruct(h2d.shape, F32),
        compiler_params=_params("parallel"),
    )(h2d, *ys, *ws)


def _final_norm_kernel(h_ref, g_ref, o_ref):
    o_ref[...] = _rms(h_ref[...], g_ref[...])


def _final_norm(h2d, g):
    t = h2d.shape[0]
    tm = PROJ_TM
    return pl.pallas_call(
        _final_norm_kernel,
        name="final_norm",
        grid=(t // tm,),
        in_specs=[pl.BlockSpec((tm, D_MODEL), lambda i: (i, 0)), pl.BlockSpec((1, D_MODEL), lambda i: (0, 0))],
        out_specs=pl.BlockSpec((tm, D_MODEL), lambda i: (i, 0)),
        out_shape=jax.ShapeDtypeStruct(h2d.shape, F32),
        compiler_params=_params("parallel"),
    )(h2d, g)


def _rope_lane_tables(positions):
    inv_freq = ROPE_THETA ** (-jnp.arange(0, MLA_ROPE, 2, dtype=F32) / MLA_ROPE)
    ang = positions.astype(F32)[..., None] * inv_freq
    cos, sin = jnp.cos(ang), jnp.sin(ang)
    lead = positions.shape
    return _rope_group(cos, cos, lead), _rope_group(sin, sin, lead)


def kernel(x, positions, norm_ffn, norm_mix, norm_final, ffn_w_gate, ffn_w_up, ffn_w_down,
           ab_w_in, ab_b_forget, pool_w, pool_scale, ab_w_out,
           mla_w_in, mla_q_norm, mla_kv_norm, mla_w_q_b, mla_w_kv_b, mla_w_out):
    b, s, d = x.shape
    t = b * s
    cos_l, sin_l = _rope_lane_tables(positions)
    h = x.astype(F32)
    for layer in range(DEPTH):
        idx = layer // 2
        wgu, wd = _prep_ffn(ffn_w_gate[layer, 0], ffn_w_up[layer, 0], ffn_w_down[layer, 0])
        h = _ffn(h.reshape(t, d), norm_ffn[layer, 0][None, :], wgu, wd).reshape(b, s, d)
        g_mix = norm_mix[layer][None, :]
        if layer % 2 == 0:
            w = _prep_ab(ab_w_in[idx], ab_b_forget[idx], pool_w[idx], pool_scale[idx], ab_w_out[idx])
            y_pool, q, k, v = _ab_in(h, g_mix, w)
            y_fox = _attention(q, k, v, FOX_HEADS, 1, FOX_HEAD_DIM)
            h = _mix_out(h.reshape(t, d), [y_pool.reshape(t, -1), y_fox.reshape(t, -1)],
                         [w["wo_pool"], w["wo_fox"]]).reshape(b, s, d)
        else:
            w = _prep_mla(mla_w_in[idx], mla_q_norm[idx], mla_kv_norm[idx], mla_w_q_b[idx],
                          mla_w_kv_b[idx], mla_w_out[idx])
            q, k, v = _mla_in(h, g_mix, w, cos_l, sin_l)
            y = _attention(q, k, v, MLA_HEADS, CHUNK, MLA_V)
            h = _mix_out(h.reshape(t, d), [y.reshape(t, -1)], [w["wo"]]).reshape(b, s, d)
        wgu, wd = _prep_ffn(ffn_w_gate[layer, 1], ffn_w_up[layer, 1], ffn_w_down[layer, 1])
        h = _ffn(h.reshape(t, d), norm_ffn[layer, 1][None, :], wgu, wd).reshape(b, s, d)
    return _final_norm(h.reshape(t, d), norm_final[None, :]).reshape(b, s, d)
```

```python
import functools
import math

import jax
import jax.numpy as jnp
from jax import lax
from jax.experimental import pallas as pl
from jax.experimental.pallas import tpu as pltpu

F32 = jnp.float32
BF16 = jnp.bfloat16

D_MODEL = 1024
DEPTH = 4
CHUNK = 64
RMS_EPS = 1e-6
D_FF = 2816
POOL_WINDOWS = (2, 4, 8, 16)
POOL_GROUP = 128
POOL_WIDTH = 512
FOX_HEADS = 8
FOX_HEAD_DIM = 64
FOX_WIDTH = 512
MLA_HEADS = 16
MLA_NOPE = 64
MLA_ROPE = 32
MLA_V = 64
MLA_Q_LORA = 256
MLA_KV_LORA = 128
ROPE_THETA = 10000.0

LANES = 128
SUBLANES = 8
VMEM_LIMIT_BYTES = 56 * 1024 * 1024

FFN_TM = 512
FFN_FC = 256
FFN_NC = D_FF // FFN_FC
PROJ_TM = 512
ATT_TQ = 512
ATT_TK = 512
HEADS_PER_STEP = 2
POOL_HALO = 16

LOG2E = math.log2(math.e)
MASKED = -1e30


def _rms(x, g):
    return x * lax.rsqrt(jnp.mean(x * x, axis=-1, keepdims=True) + RMS_EPS) * g


def _params(*sem):
    return pltpu.CompilerParams(dimension_semantics=sem, vmem_limit_bytes=VMEM_LIMIT_BYTES)


def _ffn_kernel(h_ref, g_ref, wgu_ref, wd_ref, o_ref, xn_ref, acc_ref):
    h = h_ref[...]
    xn_ref[...] = _rms(h, g_ref[...]).astype(BF16)
    acc_ref[...] = jnp.zeros_like(acc_ref)

    def chunk(c, carry):
        gu = jnp.dot(xn_ref[...], wgu_ref[c], preferred_element_type=F32)
        gate = gu[:, :FFN_FC]
        up = gu[:, FFN_FC:]
        act = (gate * jax.nn.sigmoid(gate) * up).astype(BF16)
        acc_ref[...] += jnp.dot(act, wd_ref[c], preferred_element_type=F32)
        return carry

    lax.fori_loop(0, FFN_NC, chunk, 0)
    o_ref[...] = h + 0.5 * acc_ref[...]


def _ffn(h2d, g, wgu, wd):
    t = h2d.shape[0]
    return pl.pallas_call(
        _ffn_kernel,
        name="ffn",
        grid=(t // FFN_TM,),
        in_specs=[
            pl.BlockSpec((FFN_TM, D_MODEL), lambda i: (i, 0)),
            pl.BlockSpec((1, D_MODEL), lambda i: (0, 0)),
            pl.BlockSpec((FFN_NC, D_MODEL, 2 * FFN_FC), lambda i: (0, 0, 0)),
            pl.BlockSpec((FFN_NC, FFN_FC, D_MODEL), lambda i: (0, 0, 0)),
        ],
        out_specs=pl.BlockSpec((FFN_TM, D_MODEL), lambda i: (i, 0)),
        out_shape=jax.ShapeDtypeStruct(h2d.shape, F32),
        scratch_shapes=[
            pltpu.VMEM((FFN_TM, D_MODEL), BF16),
            pltpu.VMEM((FFN_TM, D_MODEL), F32),
        ],
        compiler_params=_params("parallel"),
    )(h2d, g, wgu, wd)


def _prep_ffn(w_gate, w_up, w_down):
    wg = w_gate.astype(BF16).reshape(D_MODEL, FFN_NC, FFN_FC)
    wu = w_up.astype(BF16).reshape(D_MODEL, FFN_NC, FFN_FC)
    wgu = jnp.concatenate([wg, wu], axis=-1).transpose(1, 0, 2)
    wd = w_down.astype(BF16).reshape(FFN_NC, FFN_FC, D_MODEL)
    return wgu, wd


def _ab_in_kernel(h_ref, g_ref, wu_ref, wq_ref, wk_ref, wv_ref, wf_ref, bf_ref, tri_ref,
                  pq_ref, pk_ref, vone_ref, wpool_ref, pscale_ref,
                  ypool_ref, q_ref, k_ref, v_ref, halo_ref, fcarry_ref):
    i = pl.program_id(1)
    tm = h_ref.shape[1]

    @pl.when(i == 0)
    def _():
        halo_ref[...] = jnp.zeros_like(halo_ref)
        fcarry_ref[...] = jnp.zeros_like(fcarry_ref)

    hn = _rms(h_ref[0], g_ref[...]).astype(BF16)

    logit = jnp.dot(hn, wf_ref[...], preferred_element_type=F32) + bf_ref[...]
    log_f = jnp.minimum(logit, 0.0) - jnp.log1p(jnp.exp(-jnp.abs(logit)))
    tri = tri_ref[...]

    def split3(x):
        hi = x.astype(BF16)
        r1 = x - hi.astype(F32)
        mid = r1.astype(BF16)
        lo = (r1 - mid.astype(F32)).astype(BF16)
        return hi, mid, lo

    hi, mid, lo = split3(log_f)
    csum = (jnp.dot(tri, hi, preferred_element_type=F32)
            + jnp.dot(tri, mid, preferred_element_type=F32)
            + jnp.dot(tri, lo, preferred_element_type=F32))
    cum_f = csum + fcarry_ref[0:1, :]
    fcarry_ref[...] = jnp.broadcast_to(cum_f[tm - 1:tm, :], fcarry_ref.shape)

    fh, fm, fl = (x.astype(F32) for x in split3(cum_f * LOG2E))
    lane = lax.broadcasted_iota(jnp.int32, fh.shape, 1)
    xterms = jnp.where(lane < 8, fh, jnp.where(lane < 16, fm, jnp.where(
        lane < 24, fl, jnp.where(lane == 24, 1.0, 0.0)))).astype(BF16)

    q = jnp.dot(hn, wq_ref[...], preferred_element_type=F32) * (FOX_HEAD_DIM ** -0.5 * LOG2E)
    q = q + jnp.dot(xterms, pq_ref[...], preferred_element_type=F32)
    q_ref[0] = q.astype(BF16)
    k = jnp.dot(hn, wk_ref[...], preferred_element_type=F32)
    k = k + jnp.dot(xterms, pk_ref[...], preferred_element_type=F32)
    k_ref[0] = k.astype(BF16)
    v = jnp.dot(hn, wv_ref[...], preferred_element_type=F32) + vone_ref[...]
    v_ref[0] = v.astype(BF16)

    u = jnp.dot(hn, wu_ref[...], preferred_element_type=F32)
    ext = jnp.concatenate([halo_ref[...], u], axis=0)
    halo_ref[...] = u[tm - POOL_HALO:, :]
    t_pos = i * tm + lax.broadcasted_iota(jnp.int32, (tm, POOL_GROUP), 0)
    sums = ext
    outs = []
    for g, w in enumerate(POOL_WINDOWS):
        sums = sums + pltpu.roll(sums, w // 2, axis=0)
        win = sums[POOL_HALO:, :POOL_GROUP]
        count = jnp.minimum(t_pos + 1, w).astype(F32)
        diff = win / count - u[:, g * POOL_GROUP:(g + 1) * POOL_GROUP]
        outs.append(jnp.dot(diff.astype(BF16), wpool_ref[g], preferred_element_type=F32))
        if g + 1 < len(POOL_WINDOWS):
            sums = sums[:, POOL_GROUP:]
    y = jnp.concatenate(outs, axis=-1) * pscale_ref[...]
    ypool_ref[0] = y.astype(BF16)


def _ab_in(h, g, w):
    b, s, _ = h.shape
    tm = PROJ_TM
    hw = FOX_HEADS * LANES
    const2 = lambda shape: pl.BlockSpec(shape, lambda bi, i: (0,) * len(shape))
    tok = lambda width: pl.BlockSpec((1, tm, width), lambda bi, i: (bi, i, 0))
    return pl.pallas_call(
        _ab_in_kernel,
        name="ab_in",
        grid=(b, s // tm),
        in_specs=[
            tok(D_MODEL), const2((1, D_MODEL)),
            const2((D_MODEL, POOL_WIDTH)), const2((D_MODEL, hw)), const2((D_MODEL, hw)),
            const2((D_MODEL, hw)), const2((D_MODEL, LANES)), const2((1, LANES)),
            const2((tm, tm)), const2((LANES, hw)), const2((LANES, hw)), const2((1, hw)),
            const2((len(POOL_WINDOWS), POOL_GROUP, POOL_GROUP)), const2((1, POOL_WIDTH)),
        ],
        out_specs=[tok(POOL_WIDTH), tok(hw), tok(hw), tok(hw)],
        out_shape=[
            jax.ShapeDtypeStruct((b, s, POOL_WIDTH), BF16),
            jax.ShapeDtypeStruct((b, s, hw), BF16),
            jax.ShapeDtypeStruct((b, s, hw), BF16),
            jax.ShapeDtypeStruct((b, s, hw), BF16),
        ],
        scratch_shapes=[
            pltpu.VMEM((POOL_HALO, POOL_WIDTH), F32),
            pltpu.VMEM((SUBLANES, LANES), F32),
        ],
        compiler_params=_params("arbitrary", "arbitrary"),
    )(h, g, w["wu"], w["wq"], w["wk"], w["wv"], w["wf"], w["bf"], w["tri"],
      w["pq"], w["pk"], w["vone"], w["wpool"], w["pscale"])


def _head_groups(w, heads, width):
    rows = w.shape[0]
    w = w.reshape(rows, heads, width)
    w = jnp.pad(w, ((0, 0), (0, 0), (0, LANES - width)))
    return w.reshape(rows, heads * LANES)


def _prep_ab(w_in, b_forget, w_pool, pool_scale, w_out):
    o1, o2, o3, o4 = POOL_WIDTH, POOL_WIDTH + FOX_WIDTH, POOL_WIDTH + 2 * FOX_WIDTH, POOL_WIDTH + 3 * FOX_WIDTH
    wb = w_in.astype(BF16)
    hw = FOX_HEADS * LANES
    wf = jnp.pad(jnp.tile(wb[:, o4:], (1, 3)), ((0, 0), (0, LANES - 3 * FOX_HEADS)))
    bf = jnp.pad(jnp.tile(b_forget.astype(F32), 3), (0, LANES - 3 * FOX_HEADS))[None, :]
    r = jnp.arange(LANES)[:, None]
    c = jnp.arange(hw)[None, :]
    head, lane = c // LANES, c % LANES
    is_term = r < 3 * FOX_HEADS
    pq = jnp.where(is_term & (head == r % FOX_HEADS) & (lane == FOX_HEAD_DIM + r // FOX_HEADS), 1.0, 0.0)
    pq = pq + jnp.where((r == 3 * FOX_HEADS) & (lane >= FOX_HEAD_DIM + 3) & (lane < FOX_HEAD_DIM + 6), 1.0, 0.0)
    pk = jnp.where(is_term & (head == r % FOX_HEADS) & (lane == FOX_HEAD_DIM + 3 + r // FOX_HEADS), -1.0, 0.0)
    pk = pk + jnp.where((r == 3 * FOX_HEADS) & (lane >= FOX_HEAD_DIM) & (lane < FOX_HEAD_DIM + 3), 1.0, 0.0)
    vone = jnp.where(lane == FOX_HEAD_DIM, 1.0, 0.0).astype(F32)
    tri = jnp.tril(jnp.ones((PROJ_TM, PROJ_TM), BF16))
    wo = w_out.astype(BF16)
    return {
        "wu": wb[:, :o1],
        "wq": _head_groups(wb[:, o1:o2], FOX_HEADS, FOX_HEAD_DIM),
        "wk": _head_groups(wb[:, o2:o3], FOX_HEADS, FOX_HEAD_DIM),
        "wv": _head_groups(wb[:, o3:o4], FOX_HEADS, FOX_HEAD_DIM),
        "wf": wf, "bf": bf, "tri": tri,
        "pq": pq.astype(BF16), "pk": pk.astype(BF16), "vone": vone,
        "wpool": w_pool.astype(BF16), "pscale": pool_scale.astype(F32)[None, :],
        "wo_pool": wo[:POOL_WIDTH], "wo_fox": wo[POOL_WIDTH:],
    }


def _mla_in_kernel(h_ref, g_ref, win_ref, qn_ref, kvn_ref, wqa_ref, wqb_ref, wk_ref, wv_ref,
                   vone_ref, cos_ref, sin_ref, q_ref, k_ref, v_ref):
    hn = _rms(h_ref[0], g_ref[...]).astype(BF16)
    proj = jnp.dot(hn, win_ref[...], preferred_element_type=F32)
    c_q = proj[:, :MLA_Q_LORA]
    c_kv = proj[:, MLA_Q_LORA:MLA_Q_LORA + MLA_KV_LORA]
    kr_a = proj[:, MLA_Q_LORA + MLA_KV_LORA:MLA_Q_LORA + MLA_KV_LORA + LANES]
    kr_b = proj[:, MLA_Q_LORA + MLA_KV_LORA + LANES:]
    cos = cos_ref[0]
    sin = sin_ref[0]
    k_rope = kr_a * cos + kr_b * sin

    qn = _rms(c_q, qn_ref[...]).astype(BF16)
    kvn = _rms(c_kv, kvn_ref[...]).astype(BF16)

    scale = (MLA_NOPE + MLA_ROPE) ** -0.5 * LOG2E
    lane = lax.broadcasted_iota(jnp.int32, cos.shape, 1)
    cos_q = (cos + jnp.where(lane < MLA_NOPE, 1.0, 0.0)) * scale
    sin_q = sin * scale
    q_a = jnp.dot(qn, wqa_ref[...], preferred_element_type=F32)
    q_b = jnp.dot(qn, wqb_ref[...], preferred_element_type=F32)
    k_all = jnp.dot(kvn, wk_ref[...], preferred_element_type=F32)
    v_all = jnp.dot(kvn, wv_ref[...], preferred_element_type=F32) + vone_ref[...]
    for hd in range(MLA_HEADS):
        grp = slice(hd * LANES, (hd + 1) * LANES)
        q_ref[0, :, grp] = (q_a[:, grp] * cos_q + q_b[:, grp] * sin_q).astype(BF16)
        k_ref[0, :, grp] = (k_all[:, grp] + k_rope).astype(BF16)
    v_ref[0] = v_all.astype(BF16)


def _mla_in(h, g, w, cos_l, sin_l):
    b, s, _ = h.shape
    tm = PROJ_TM
    hw = MLA_HEADS * LANES
    nin = MLA_Q_LORA + MLA_KV_LORA + 2 * LANES
    const2 = lambda shape: pl.BlockSpec(shape, lambda bi, i: (0,) * len(shape))
    tok = lambda width: pl.BlockSpec((1, tm, width), lambda bi, i: (bi, i, 0))
    return pl.pallas_call(
        _mla_in_kernel,
        name="mla_in",
        grid=(b, s // tm),
        in_specs=[
            tok(D_MODEL), const2((1, D_MODEL)), const2((D_MODEL, nin)),
            const2((1, MLA_Q_LORA)), const2((1, MLA_KV_LORA)),
            const2((MLA_Q_LORA, hw)), const2((MLA_Q_LORA, hw)),
            const2((MLA_KV_LORA, hw)), const2((MLA_KV_LORA, hw)), const2((1, hw)),
            tok(LANES), tok(LANES),
        ],
        out_specs=[tok(hw), tok(hw), tok(hw)],
        out_shape=[jax.ShapeDtypeStruct((b, s, hw), BF16)] * 3,
        compiler_params=_params("parallel", "parallel"),
    )(h, g, w["win"], w["qn"], w["kvn"], w["wqa"], w["wqb"], w["wk"], w["wv"], w["vone"],
      cos_l, sin_l)


def _rope_group(x1, x2, lead):
    z0 = jnp.zeros(lead + (MLA_NOPE,), x1.dtype)
    z1 = jnp.zeros(lead + (LANES - MLA_NOPE - MLA_ROPE,), x1.dtype)
    return jnp.concatenate([z0, x1, x2, z1], axis=-1)


def _prep_mla(w_in, q_norm, kv_norm, w_q_b, w_kv_b, w_out):
    half = MLA_ROPE // 2
    wb = w_in.astype(BF16)
    kr = wb[:, MLA_Q_LORA + MLA_KV_LORA:]
    a1, a2 = kr[:, :half], kr[:, half:]
    win = jnp.concatenate([
        wb[:, :MLA_Q_LORA + MLA_KV_LORA],
        _rope_group(a1, a2, (D_MODEL,)),
        _rope_group(-a2, a1, (D_MODEL,)),
    ], axis=-1)
    wq = w_q_b.astype(BF16).reshape(MLA_Q_LORA, MLA_HEADS, MLA_NOPE + MLA_ROPE)
    nope, x1, x2 = wq[..., :MLA_NOPE], wq[..., MLA_NOPE:MLA_NOPE + half], wq[..., MLA_NOPE + half:]
    tail = jnp.zeros((MLA_Q_LORA, MLA_HEADS, LANES - MLA_NOPE - MLA_ROPE), BF16)
    wqa = jnp.concatenate([nope, x1, x2, tail], axis=-1).reshape(MLA_Q_LORA, MLA_HEADS * LANES)
    wqb = jnp.concatenate([jnp.zeros_like(nope), -x2, x1, tail], axis=-1).reshape(MLA_Q_LORA, MLA_HEADS * LANES)
    wkv = w_kv_b.astype(BF16).reshape(MLA_KV_LORA, MLA_HEADS, MLA_NOPE + MLA_V)
    pad = jnp.zeros((MLA_KV_LORA, MLA_HEADS, LANES - MLA_NOPE), BF16)
    wk = jnp.concatenate([wkv[..., :MLA_NOPE], pad], axis=-1).reshape(MLA_KV_LORA, MLA_HEADS * LANES)
    wv = jnp.concatenate([wkv[..., MLA_NOPE:], pad], axis=-1).reshape(MLA_KV_LORA, MLA_HEADS * LANES)
    lane = jnp.arange(MLA_HEADS * LANES) % LANES
    vone = jnp.where(lane == MLA_V, 1.0, 0.0).astype(F32)[None, :]
    return {
        "win": win, "qn": q_norm.astype(F32)[None, :], "kvn": kv_norm.astype(F32)[None, :],
        "wqa": wqa, "wqb": wqb, "wk": wk, "wv": wv, "vone": vone, "wo": w_out.astype(BF16),
    }


def _attn_kernel(qt_ref, k_ref, vt_ref, o_ref, m_ref, acc_ref, s_ref, *, tq, tk, chunk, head_dim):
    i = pl.program_id(2)
    shift = chunk.bit_length() - 1
    key = lax.broadcasted_iota(jnp.int32, (tk, tq), 0)
    qry = lax.broadcasted_iota(jnp.int32, (tk, tq), 1)
    diag_mask = (key >> shift) <= (qry >> shift)

    m_ref[...] = jnp.full(m_ref.shape, MASKED, F32)
    acc_ref[...] = jnp.zeros_like(acc_ref)
    groups = [slice(hh * LANES, (hh + 1) * LANES) for hh in range(HEADS_PER_STEP)]

    def scores_into(j, slot):
        off = pl.multiple_of(j * tk, tk)
        for hh, grp in enumerate(groups):
            s_ref[slot, hh] = jnp.dot(k_ref[0, pl.ds(off, tk), grp], qt_ref[0, grp, :],
                                      preferred_element_type=F32)

    def softmax_pv(j, slot, masked):
        off = pl.multiple_of(j * tk, tk)
        for hh, grp in enumerate(groups):
            s = s_ref[slot, hh]
            vt = vt_ref[0, grp, pl.ds(off, tk)]
            if masked:
                s = jnp.where(diag_mask, s, MASKED)
            m_prev = m_ref[hh]
            m_new = jnp.maximum(m_prev, jnp.max(s, axis=0, keepdims=True))
            alpha = jnp.exp2(m_prev - m_new)
            p = jnp.exp2(s - m_new)
            acc_ref[hh] = acc_ref[hh] * alpha + jnp.dot(vt, p.astype(BF16), preferred_element_type=F32)
            m_ref[hh] = m_new

    scores_into(0, 0)

    def tile_pair(pair, carry):
        j = 2 * pair
        scores_into(j + 1, 1)
        softmax_pv(j, 0, False)
        scores_into(j + 2, 0)
        softmax_pv(j + 1, 1, False)
        return carry

    lax.fori_loop(0, i // 2, tile_pair, 0)

    @pl.when(i % 2 == 0)
    def _():
        softmax_pv(i, 0, True)

    @pl.when(i % 2 == 1)
    def _():
        scores_into(i, 1)
        softmax_pv(i - 1, 0, False)
        softmax_pv(i, 1, True)

    outs = []
    for hh in range(HEADS_PER_STEP):
        acc = acc_ref[hh]
        outs.append(acc[:head_dim, :] / acc[head_dim:head_dim + 1, :])
    o_ref[0] = jnp.concatenate(outs, axis=0).T.astype(BF16)


def _attention(qt, k, vt, heads, chunk, head_dim):
    b, s, _ = k.shape
    tq, tk = ATT_TQ, ATT_TK
    assert tq == tk and tq % chunk == 0 and s % tq == 0 and heads % HEADS_PER_STEP == 0
    assert HEADS_PER_STEP * head_dim == LANES
    gw = HEADS_PER_STEP * LANES
    return pl.pallas_call(
        functools.partial(_attn_kernel, tq=tq, tk=tk, chunk=chunk, head_dim=head_dim),
        name="attention",
        grid=(b, heads // HEADS_PER_STEP, s // tq),
        in_specs=[
            pl.BlockSpec((1, gw, tq), lambda bi, hp, i: (bi, hp, i)),
            pl.BlockSpec((1, s, gw), lambda bi, hp, i: (bi, 0, hp)),
            pl.BlockSpec((1, gw, s), lambda bi, hp, i: (bi, hp, 0)),
        ],
        out_specs=pl.BlockSpec((1, tq, HEADS_PER_STEP * head_dim), lambda bi, hp, i: (bi, i, hp)),
        out_shape=jax.ShapeDtypeStruct((b, s, heads * head_dim), BF16),
        scratch_shapes=[
            pltpu.VMEM((HEADS_PER_STEP, 1, tq), F32),
            pltpu.VMEM((HEADS_PER_STEP, LANES, tq), F32),
            pltpu.VMEM((2, HEADS_PER_STEP, tk, tq), F32),
        ],
        compiler_params=_params("parallel", "parallel", "arbitrary"),
    )(qt, k, vt)


def _out_kernel(*refs):
    n = (len(refs) - 2) // 2
    h_ref, o_ref = refs[0], refs[-1]
    acc = h_ref[...]
    for y_ref, w_ref in zip(refs[1:1 + n], refs[1 + n:1 + 2 * n]):
        acc = acc + jnp.dot(y_ref[...], w_ref[...], preferred_element_type=F32)
    o_ref[...] = acc


def _mix_out(h2d, ys, ws):
    t = h2d.shape[0]
    tm = PROJ_TM
    in_specs = [pl.BlockSpec((tm, D_MODEL), lambda i: (i, 0))]
    in_specs += [pl.BlockSpec((tm, y.shape[1]), lambda i: (i, 0)) for y in ys]
    in_specs += [pl.BlockSpec(w.shape, lambda i: (0, 0)) for w in ws]
    return pl.pallas_call(
        _out_kernel,
        name="mix_out",
        grid=(t // tm,),
        in_specs=in_specs,
        out_specs=pl.BlockSpec((tm, D_MODEL), lambda i: (i, 0)),
        out_shape=jax.ShapeDtypeStruct(h2d.shape, F32),
        compiler_params=_params("parallel"),
    )(h2d, *ys, *ws)


def _final_norm_kernel(h_ref, g_ref, o_ref):
    o_ref[...] = _rms(h_ref[...], g_ref[...])


def _final_norm(h2d, g):
    t = h2d.shape[0]
    tm = PROJ_TM
    return pl.pallas_call(
        _final_norm_kernel,
        name="final_norm",
        grid=(t // tm,),
        in_specs=[pl.BlockSpec((tm, D_MODEL), lambda i: (i, 0)), pl.BlockSpec((1, D_MODEL), lambda i: (0, 0))],
        out_specs=pl.BlockSpec((tm, D_MODEL), lambda i: (i, 0)),
        out_shape=jax.ShapeDtypeStruct(h2d.shape, F32),
        compiler_params=_params("parallel"),
    )(h2d, g)


def _rope_lane_tables(positions):
    inv_freq = ROPE_THETA ** (-jnp.arange(0, MLA_ROPE, 2, dtype=F32) / MLA_ROPE)
    ang = positions.astype(F32)[..., None] * inv_freq
    cos, sin = jnp.cos(ang), jnp.sin(ang)
    lead = positions.shape
    return _rope_group(cos, cos, lead), _rope_group(sin, sin, lead)


def kernel(x, positions, norm_ffn, norm_mix, norm_final, ffn_w_gate, ffn_w_up, ffn_w_down,
           ab_w_in, ab_b_forget, pool_w, pool_scale, ab_w_out,
           mla_w_in, mla_q_norm, mla_kv_norm, mla_w_q_b, mla_w_kv_b, mla_w_out):
    b, s, d = x.shape
    t = b * s
    cos_l, sin_l = _rope_lane_tables(positions)
    h = x.astype(F32)
    for layer in range(DEPTH):
        idx = layer // 2
        wgu, wd = _prep_ffn(ffn_w_gate[layer, 0], ffn_w_up[layer, 0], ffn_w_down[layer, 0])
        h = _ffn(h.reshape(t, d), norm_ffn[layer, 0][None, :], wgu, wd).reshape(b, s, d)
        g_mix = norm_mix[layer][None, :]
        if layer % 2 == 0:
            w = _prep_ab(ab_w_in[idx], ab_b_forget[idx], pool_w[idx], pool_scale[idx], ab_w_out[idx])
            y_pool, q, k, v = _ab_in(h, g_mix, w)
            y_fox = _attention(jnp.swapaxes(q, 1, 2), k, jnp.swapaxes(v, 1, 2), FOX_HEADS, 1, FOX_HEAD_DIM)
            h = _mix_out(h.reshape(t, d), [y_pool.reshape(t, -1), y_fox.reshape(t, -1)],
                         [w["wo_pool"], w["wo_fox"]]).reshape(b, s, d)
        else:
            w = _prep_mla(mla_w_in[idx], mla_q_norm[idx], mla_kv_norm[idx], mla_w_q_b[idx],
                          mla_w_kv_b[idx], mla_w_out[idx])
            q, k, v = _mla_in(h, g_mix, w, cos_l, sin_l)
            y = _attention(jnp.swapaxes(q, 1, 2), k, jnp.swapaxes(v, 1, 2), MLA_HEADS, CHUNK, MLA_V)
            h = _mix_out(h.reshape(t, d), [y.reshape(t, -1)], [w["wo"]]).reshape(b, s, d)
        wgu, wd = _prep_ffn(ffn_w_gate[layer, 1], ffn_w_up[layer, 1], ffn_w_down[layer, 1])
        h = _ffn(h.reshape(t, d), norm_ffn[layer, 1][None, :], wgu, wd).reshape(b, s, d)
    return _final_norm(h.reshape(t, d), norm_final[None, :]).reshape(b, s, d)
```

```python
import functools
import math

import jax
import jax.numpy as jnp
from jax import lax
from jax.experimental import pallas as pl
from jax.experimental.pallas import tpu as pltpu

F32 = jnp.float32
BF16 = jnp.bfloat16

D_MODEL = 1024
DEPTH = 4
CHUNK = 64
RMS_EPS = 1e-6
D_FF = 2816
POOL_WINDOWS = (2, 4, 8, 16)
POOL_GROUP = 128
POOL_WIDTH = 512
FOX_HEADS = 8
FOX_HEAD_DIM = 64
FOX_WIDTH = 512
MLA_HEADS = 16
MLA_NOPE = 64
MLA_ROPE = 32
MLA_V = 64
MLA_Q_LORA = 256
MLA_KV_LORA = 128
ROPE_THETA = 10000.0

LANES = 128
SUBLANES = 8
VMEM_LIMIT_BYTES = 56 * 1024 * 1024

FFN_TM = 1024
FFN_FC = 256
FFN_NC = D_FF // FFN_FC
PROJ_TM = 512
ATT_TQ = 512
ATT_TK = 512
HEADS_PER_STEP = 2
V_ROWS = 80
POOL_HALO = 16

LOG2E = math.log2(math.e)
MASKED = -1e30


def _rms(x, g):
    return x * lax.rsqrt(jnp.mean(x * x, axis=-1, keepdims=True) + RMS_EPS) * g


def _params(*sem):
    return pltpu.CompilerParams(dimension_semantics=sem, vmem_limit_bytes=VMEM_LIMIT_BYTES)


def _ffn_kernel(h_ref, g_ref, wgu_ref, wd_ref, o_ref, xn_ref, acc_ref, act_ref):
    h = h_ref[...]
    xn_ref[...] = _rms(h, g_ref[...]).astype(BF16)

    def hidden(c):
        gu = jnp.dot(xn_ref[...], wgu_ref[c], preferred_element_type=F32)
        gate = gu[:, :FFN_FC]
        up = gu[:, FFN_FC:]
        return (gate * jax.nn.sigmoid(gate) * up).astype(BF16)

    act_ref[0] = hidden(0)
    acc_ref[...] = jnp.zeros_like(acc_ref)

    def chunk_pair(pair, carry):
        c = 2 * pair
        act_ref[1] = hidden(c + 1)
        acc_ref[...] += jnp.dot(act_ref[0], wd_ref[c], preferred_element_type=F32)
        act_ref[0] = hidden(c + 2)
        acc_ref[...] += jnp.dot(act_ref[1], wd_ref[c + 1], preferred_element_type=F32)
        return carry

    assert FFN_NC % 2 == 1
    lax.fori_loop(0, FFN_NC // 2, chunk_pair, 0)
    o_ref[...] = h + 0.5 * (acc_ref[...] + jnp.dot(act_ref[0], wd_ref[FFN_NC - 1],
                                                   preferred_element_type=F32))


def _ffn(h2d, g, wgu, wd):
    t = h2d.shape[0]
    return pl.pallas_call(
        _ffn_kernel,
        name="ffn",
        grid=(t // FFN_TM,),
        in_specs=[
            pl.BlockSpec((FFN_TM, D_MODEL), lambda i: (i, 0)),
            pl.BlockSpec((1, D_MODEL), lambda i: (0, 0)),
            pl.BlockSpec((FFN_NC, D_MODEL, 2 * FFN_FC), lambda i: (0, 0, 0), pipeline_mode=pl.Buffered(1)),
            pl.BlockSpec((FFN_NC, FFN_FC, D_MODEL), lambda i: (0, 0, 0), pipeline_mode=pl.Buffered(1)),
        ],
        out_specs=pl.BlockSpec((FFN_TM, D_MODEL), lambda i: (i, 0)),
        out_shape=jax.ShapeDtypeStruct(h2d.shape, F32),
        scratch_shapes=[
            pltpu.VMEM((FFN_TM, D_MODEL), BF16),
            pltpu.VMEM((FFN_TM, D_MODEL), F32),
            pltpu.VMEM((2, FFN_TM, FFN_FC), BF16),
        ],
        compiler_params=_params("parallel"),
    )(h2d, g, wgu, wd)


def _prep_ffn(w_gate, w_up, w_down):
    wg = w_gate.astype(BF16).reshape(D_MODEL, FFN_NC, FFN_FC)
    wu = w_up.astype(BF16).reshape(D_MODEL, FFN_NC, FFN_FC)
    wgu = jnp.concatenate([wg, wu], axis=-1).transpose(1, 0, 2)
    wd = w_down.astype(BF16).reshape(FFN_NC, FFN_FC, D_MODEL)
    return wgu, wd


def _ab_in_kernel(h_ref, g_ref, wu_ref, wq_ref, wk_ref, wv_ref, wf_ref, bf_ref, tri_ref,
                  pq_ref, pk_ref, vone_ref, wpool_ref, pscale_ref,
                  ypool_ref, q_ref, k_ref, v_ref, halo_ref, fcarry_ref):
    i = pl.program_id(1)
    tm = h_ref.shape[1]

    @pl.when(i == 0)
    def _():
        halo_ref[...] = jnp.zeros_like(halo_ref)
        fcarry_ref[...] = jnp.zeros_like(fcarry_ref)

    hn = _rms(h_ref[0], g_ref[...]).astype(BF16)

    logit = jnp.dot(hn, wf_ref[...], preferred_element_type=F32) + bf_ref[...]
    log_f = jnp.minimum(logit, 0.0) - jnp.log1p(jnp.exp(-jnp.abs(logit)))
    tri = tri_ref[...]

    def split3(x):
        hi = x.astype(BF16)
        r1 = x - hi.astype(F32)
        mid = r1.astype(BF16)
        lo = (r1 - mid.astype(F32)).astype(BF16)
        return hi, mid, lo

    hi, mid, lo = split3(log_f)
    csum = (jnp.dot(tri, hi, preferred_element_type=F32)
            + jnp.dot(tri, mid, preferred_element_type=F32)
            + jnp.dot(tri, lo, preferred_element_type=F32))
    cum_f = csum + fcarry_ref[0:1, :]
    fcarry_ref[...] = jnp.broadcast_to(cum_f[tm - 1:tm, :], fcarry_ref.shape)

    fh, fm, fl = (x.astype(F32) for x in split3(cum_f * LOG2E))
    lane = lax.broadcasted_iota(jnp.int32, fh.shape, 1)
    xterms = jnp.where(lane < 8, fh, jnp.where(lane < 16, fm, jnp.where(
        lane < 24, fl, jnp.where(lane == 24, 1.0, 0.0)))).astype(BF16)

    q = jnp.dot(hn, wq_ref[...], preferred_element_type=F32) * (FOX_HEAD_DIM ** -0.5 * LOG2E)
    q = q + jnp.dot(xterms, pq_ref[...], preferred_element_type=F32)
    q_ref[0] = q.T.astype(BF16)
    k = jnp.dot(hn, wk_ref[...], preferred_element_type=F32)
    k = k + jnp.dot(xterms, pk_ref[...], preferred_element_type=F32)
    k_ref[0] = k.astype(BF16)
    v = jnp.dot(hn, wv_ref[...], preferred_element_type=F32) + vone_ref[...]
    v_ref[0] = v.T.astype(BF16)

    u = jnp.dot(hn, wu_ref[...], preferred_element_type=F32)
    ext = jnp.concatenate([halo_ref[...], u], axis=0)
    halo_ref[...] = u[tm - POOL_HALO:, :]
    t_pos = i * tm + lax.broadcasted_iota(jnp.int32, (tm, POOL_GROUP), 0)
    sums = ext
    outs = []
    for g, w in enumerate(POOL_WINDOWS):
        sums = sums + pltpu.roll(sums, w // 2, axis=0)
        win = sums[POOL_HALO:, :POOL_GROUP]
        count = jnp.minimum(t_pos + 1, w).astype(F32)
        diff = win / count - u[:, g * POOL_GROUP:(g + 1) * POOL_GROUP]
        outs.append(jnp.dot(diff.astype(BF16), wpool_ref[g], preferred_element_type=F32))
        if g + 1 < len(POOL_WINDOWS):
            sums = sums[:, POOL_GROUP:]
    y = jnp.concatenate(outs, axis=-1) * pscale_ref[...]
    ypool_ref[0] = y.astype(BF16)


def _ab_in(h, g, w):
    b, s, _ = h.shape
    tm = PROJ_TM
    hw = FOX_HEADS * LANES
    const2 = lambda shape: pl.BlockSpec(shape, lambda bi, i: (0,) * len(shape))
    tok = lambda width: pl.BlockSpec((1, tm, width), lambda bi, i: (bi, i, 0))
    tok_t = lambda width: pl.BlockSpec((1, width, tm), lambda bi, i: (bi, 0, i))
    return pl.pallas_call(
        _ab_in_kernel,
        name="ab_in",
        grid=(b, s // tm),
        in_specs=[
            tok(D_MODEL), const2((1, D_MODEL)),
            const2((D_MODEL, POOL_WIDTH)), const2((D_MODEL, hw)), const2((D_MODEL, hw)),
            const2((D_MODEL, hw)), const2((D_MODEL, LANES)), const2((1, LANES)),
            const2((tm, tm)), const2((LANES, hw)), const2((LANES, hw)), const2((1, hw)),
            const2((len(POOL_WINDOWS), POOL_GROUP, POOL_GROUP)), const2((1, POOL_WIDTH)),
        ],
        out_specs=[tok(POOL_WIDTH), tok_t(hw), tok(hw), tok_t(hw)],
        out_shape=[
            jax.ShapeDtypeStruct((b, s, POOL_WIDTH), BF16),
            jax.ShapeDtypeStruct((b, hw, s), BF16),
            jax.ShapeDtypeStruct((b, s, hw), BF16),
            jax.ShapeDtypeStruct((b, hw, s), BF16),
        ],
        scratch_shapes=[
            pltpu.VMEM((POOL_HALO, POOL_WIDTH), F32),
            pltpu.VMEM((SUBLANES, LANES), F32),
        ],
        compiler_params=_params("arbitrary", "arbitrary"),
    )(h, g, w["wu"], w["wq"], w["wk"], w["wv"], w["wf"], w["bf"], w["tri"],
      w["pq"], w["pk"], w["vone"], w["wpool"], w["pscale"])


def _head_groups(w, heads, width):
    rows = w.shape[0]
    w = w.reshape(rows, heads, width)
    w = jnp.pad(w, ((0, 0), (0, 0), (0, LANES - width)))
    return w.reshape(rows, heads * LANES)


def _prep_ab(w_in, b_forget, w_pool, pool_scale, w_out):
    o1, o2, o3, o4 = POOL_WIDTH, POOL_WIDTH + FOX_WIDTH, POOL_WIDTH + 2 * FOX_WIDTH, POOL_WIDTH + 3 * FOX_WIDTH
    wb = w_in.astype(BF16)
    hw = FOX_HEADS * LANES
    wf = jnp.pad(jnp.tile(wb[:, o4:], (1, 3)), ((0, 0), (0, LANES - 3 * FOX_HEADS)))
    bf = jnp.pad(jnp.tile(b_forget.astype(F32), 3), (0, LANES - 3 * FOX_HEADS))[None, :]
    r = jnp.arange(LANES)[:, None]
    c = jnp.arange(hw)[None, :]
    head, lane = c // LANES, c % LANES
    is_term = r < 3 * FOX_HEADS
    pq = jnp.where(is_term & (head == r % FOX_HEADS) & (lane == FOX_HEAD_DIM + r // FOX_HEADS), 1.0, 0.0)
    pq = pq + jnp.where((r == 3 * FOX_HEADS) & (lane >= FOX_HEAD_DIM + 3) & (lane < FOX_HEAD_DIM + 6), 1.0, 0.0)
    pk = jnp.where(is_term & (head == r % FOX_HEADS) & (lane == FOX_HEAD_DIM + 3 + r // FOX_HEADS), -1.0, 0.0)
    pk = pk + jnp.where((r == 3 * FOX_HEADS) & (lane >= FOX_HEAD_DIM) & (lane < FOX_HEAD_DIM + 3), 1.0, 0.0)
    vone = jnp.where(lane == FOX_HEAD_DIM, 1.0, 0.0).astype(F32)
    tri = jnp.tril(jnp.ones((PROJ_TM, PROJ_TM), BF16))
    wo = w_out.astype(BF16)
    return {
        "wu": wb[:, :o1],
        "wq": _head_groups(wb[:, o1:o2], FOX_HEADS, FOX_HEAD_DIM),
        "wk": _head_groups(wb[:, o2:o3], FOX_HEADS, FOX_HEAD_DIM),
        "wv": _head_groups(wb[:, o3:o4], FOX_HEADS, FOX_HEAD_DIM),
        "wf": wf, "bf": bf, "tri": tri,
        "pq": pq.astype(BF16), "pk": pk.astype(BF16), "vone": vone,
        "wpool": w_pool.astype(BF16), "pscale": pool_scale.astype(F32)[None, :],
        "wo_pool": wo[:POOL_WIDTH], "wo_fox": wo[POOL_WIDTH:],
    }


def _mla_in_kernel(h_ref, g_ref, win_ref, qn_ref, kvn_ref, wqa_ref, wqb_ref, wk_ref, wv_ref,
                   vone_ref, cos_ref, sin_ref, q_ref, k_ref, v_ref):
    hn = _rms(h_ref[0], g_ref[...]).astype(BF16)
    proj = jnp.dot(hn, win_ref[...], preferred_element_type=F32)
    c_q = proj[:, :MLA_Q_LORA]
    c_kv = proj[:, MLA_Q_LORA:MLA_Q_LORA + MLA_KV_LORA]
    kr_a = proj[:, MLA_Q_LORA + MLA_KV_LORA:MLA_Q_LORA + MLA_KV_LORA + LANES]
    kr_b = proj[:, MLA_Q_LORA + MLA_KV_LORA + LANES:]
    cos = cos_ref[0]
    sin = sin_ref[0]
    k_rope = kr_a * cos + kr_b * sin

    qn = _rms(c_q, qn_ref[...]).astype(BF16)
    kvn = _rms(c_kv, kvn_ref[...]).astype(BF16)

    scale = (MLA_NOPE + MLA_ROPE) ** -0.5 * LOG2E
    lane = lax.broadcasted_iota(jnp.int32, cos.shape, 1)
    cos_q = (cos + jnp.where(lane < MLA_NOPE, 1.0, 0.0)) * scale
    sin_q = sin * scale
    q_a = jnp.dot(qn, wqa_ref[...], preferred_element_type=F32)
    q_b = jnp.dot(qn, wqb_ref[...], preferred_element_type=F32)
    k_all = jnp.dot(kvn, wk_ref[...], preferred_element_type=F32)
    v_all = jnp.dot(kvn, wv_ref[...], preferred_element_type=F32) + vone_ref[...]
    for hd in range(MLA_HEADS):
        grp = slice(hd * LANES, (hd + 1) * LANES)
        q_ref[0, grp, :] = (q_a[:, grp] * cos_q + q_b[:, grp] * sin_q).T.astype(BF16)
        k_ref[0, :, grp] = (k_all[:, grp] + k_rope).astype(BF16)
        v_ref[0, grp, :] = v_all[:, grp].T.astype(BF16)


def _mla_in(h, g, w, cos_l, sin_l):
    b, s, _ = h.shape
    tm = PROJ_TM
    hw = MLA_HEADS * LANES
    nin = MLA_Q_LORA + MLA_KV_LORA + 2 * LANES
    const2 = lambda shape: pl.BlockSpec(shape, lambda bi, i: (0,) * len(shape))
    tok = lambda width: pl.BlockSpec((1, tm, width), lambda bi, i: (bi, i, 0))
    tok_t = lambda width: pl.BlockSpec((1, width, tm), lambda bi, i: (bi, 0, i))
    return pl.pallas_call(
        _mla_in_kernel,
        name="mla_in",
        grid=(b, s // tm),
        in_specs=[
            tok(D_MODEL), const2((1, D_MODEL)), const2((D_MODEL, nin)),
            const2((1, MLA_Q_LORA)), const2((1, MLA_KV_LORA)),
            const2((MLA_Q_LORA, hw)), const2((MLA_Q_LORA, hw)),
            const2((MLA_KV_LORA, hw)), const2((MLA_KV_LORA, hw)), const2((1, hw)),
            tok(LANES), tok(LANES),
        ],
        out_specs=[tok_t(hw), tok(hw), tok_t(hw)],
        out_shape=[jax.ShapeDtypeStruct((b, hw, s), BF16), jax.ShapeDtypeStruct((b, s, hw), BF16),
                   jax.ShapeDtypeStruct((b, hw, s), BF16)],
        compiler_params=_params("parallel", "parallel"),
    )(h, g, w["win"], w["qn"], w["kvn"], w["wqa"], w["wqb"], w["wk"], w["wv"], w["vone"],
      cos_l, sin_l)


def _rope_group(x1, x2, lead):
    z0 = jnp.zeros(lead + (MLA_NOPE,), x1.dtype)
    z1 = jnp.zeros(lead + (LANES - MLA_NOPE - MLA_ROPE,), x1.dtype)
    return jnp.concatenate([z0, x1, x2, z1], axis=-1)


def _prep_mla(w_in, q_norm, kv_norm, w_q_b, w_kv_b, w_out):
    half = MLA_ROPE // 2
    wb = w_in.astype(BF16)
    kr = wb[:, MLA_Q_LORA + MLA_KV_LORA:]
    a1, a2 = kr[:, :half], kr[:, half:]
    win = jnp.concatenate([
        wb[:, :MLA_Q_LORA + MLA_KV_LORA],
        _rope_group(a1, a2, (D_MODEL,)),
        _rope_group(-a2, a1, (D_MODEL,)),
    ], axis=-1)
    wq = w_q_b.astype(BF16).reshape(MLA_Q_LORA, MLA_HEADS, MLA_NOPE + MLA_ROPE)
    nope, x1, x2 = wq[..., :MLA_NOPE], wq[..., MLA_NOPE:MLA_NOPE + half], wq[..., MLA_NOPE + half:]
    tail = jnp.zeros((MLA_Q_LORA, MLA_HEADS, LANES - MLA_NOPE - MLA_ROPE), BF16)
    wqa = jnp.concatenate([nope, x1, x2, tail], axis=-1).reshape(MLA_Q_LORA, MLA_HEADS * LANES)
    wqb = jnp.concatenate([jnp.zeros_like(nope), -x2, x1, tail], axis=-1).reshape(MLA_Q_LORA, MLA_HEADS * LANES)
    wkv = w_kv_b.astype(BF16).reshape(MLA_KV_LORA, MLA_HEADS, MLA_NOPE + MLA_V)
    pad = jnp.zeros((MLA_KV_LORA, MLA_HEADS, LANES - MLA_NOPE), BF16)
    wk = jnp.concatenate([wkv[..., :MLA_NOPE], pad], axis=-1).reshape(MLA_KV_LORA, MLA_HEADS * LANES)
    wv = jnp.concatenate([wkv[..., MLA_NOPE:], pad], axis=-1).reshape(MLA_KV_LORA, MLA_HEADS * LANES)
    lane = jnp.arange(MLA_HEADS * LANES) % LANES
    vone = jnp.where(lane == MLA_V, 1.0, 0.0).astype(F32)[None, :]
    return {
        "win": win, "qn": q_norm.astype(F32)[None, :], "kvn": kv_norm.astype(F32)[None, :],
        "wqa": wqa, "wqb": wqb, "wk": wk, "wv": wv, "vone": vone, "wo": w_out.astype(BF16),
    }


def _attn_kernel(qt_ref, k_ref, vt_ref, o_ref, m_ref, acc_ref, s_ref, *, tq, tk, chunk, head_dim):
    i = pl.program_id(2)
    shift = chunk.bit_length() - 1
    key = lax.broadcasted_iota(jnp.int32, (tk, tq), 0)
    qry = lax.broadcasted_iota(jnp.int32, (tk, tq), 1)
    diag_mask = (key >> shift) <= (qry >> shift)

    m_ref[...] = jnp.full(m_ref.shape, MASKED, F32)
    acc_ref[...] = jnp.zeros_like(acc_ref)
    groups = [slice(hh * LANES, (hh + 1) * LANES) for hh in range(HEADS_PER_STEP)]

    def scores_into(j, slot):
        off = pl.multiple_of(j * tk, tk)
        for hh, grp in enumerate(groups):
            s_ref[slot, hh] = jnp.dot(k_ref[0, pl.ds(off, tk), grp], qt_ref[0, grp, :],
                                      preferred_element_type=F32)

    def softmax_pv(j, slot, masked):
        off = pl.multiple_of(j * tk, tk)
        for hh in range(HEADS_PER_STEP):
            s = s_ref[slot, hh]
            vt = vt_ref[0, hh * LANES:hh * LANES + V_ROWS, pl.ds(off, tk)]
            if masked:
                s = jnp.where(diag_mask, s, MASKED)
            m_prev = m_ref[hh]
            m_new = jnp.maximum(m_prev, jnp.max(s, axis=0, keepdims=True))
            alpha = jnp.exp2(m_prev - m_new)
            p = jnp.exp2(s - m_new)
            acc_ref[hh] = acc_ref[hh] * alpha + jnp.dot(vt, p.astype(BF16), preferred_element_type=F32)
            m_ref[hh] = m_new

    scores_into(0, 0)

    def tile_pair(pair, carry):
        j = 2 * pair
        scores_into(j + 1, 1)
        softmax_pv(j, 0, False)
        scores_into(j + 2, 0)
        softmax_pv(j + 1, 1, False)
        return carry

    lax.fori_loop(0, i // 2, tile_pair, 0)

    @pl.when(i % 2 == 0)
    def _():
        softmax_pv(i, 0, True)

    @pl.when(i % 2 == 1)
    def _():
        scores_into(i, 1)
        softmax_pv(i - 1, 0, False)
        softmax_pv(i, 1, True)

    outs = []
    for hh in range(HEADS_PER_STEP):
        acc = acc_ref[hh]
        outs.append(acc[:head_dim, :] / acc[head_dim:head_dim + 1, :])
    o_ref[0] = jnp.concatenate(outs, axis=0).T.astype(BF16)


def _attention(qt, k, vt, heads, chunk, head_dim):
    b, s, _ = k.shape
    tq, tk = ATT_TQ, ATT_TK
    assert tq == tk and tq % chunk == 0 and s % tq == 0 and heads % HEADS_PER_STEP == 0
    assert HEADS_PER_STEP * head_dim == LANES
    gw = HEADS_PER_STEP * LANES
    return pl.pallas_call(
        functools.partial(_attn_kernel, tq=tq, tk=tk, chunk=chunk, head_dim=head_dim),
        name="attention",
        grid=(b, heads // HEADS_PER_STEP, s // tq),
        in_specs=[
            pl.BlockSpec((1, gw, tq), lambda bi, hp, i: (bi, hp, i)),
            pl.BlockSpec((1, s, gw), lambda bi, hp, i: (bi, 0, hp)),
            pl.BlockSpec((1, gw, s), lambda bi, hp, i: (bi, hp, 0)),
        ],
        out_specs=pl.BlockSpec((1, tq, HEADS_PER_STEP * head_dim), lambda bi, hp, i: (bi, i, hp)),
        out_shape=jax.ShapeDtypeStruct((b, s, heads * head_dim), BF16),
        scratch_shapes=[
            pltpu.VMEM((HEADS_PER_STEP, 1, tq), F32),
            pltpu.VMEM((HEADS_PER_STEP, V_ROWS, tq), F32),
            pltpu.VMEM((2, HEADS_PER_STEP, tk, tq), F32),
        ],
        compiler_params=_params("parallel", "parallel", "arbitrary"),
    )(qt, k, vt)


def _out_kernel(*refs):
    n = (len(refs) - 2) // 2
    h_ref, o_ref = refs[0], refs[-1]
    acc = h_ref[...]
    for y_ref, w_ref in zip(refs[1:1 + n], refs[1 + n:1 + 2 * n]):
        acc = acc + jnp.dot(y_ref[...], w_ref[...], preferred_element_type=F32)
    o_ref[...] = acc


def _mix_out(h2d, ys, ws):
    t = h2d.shape[0]
    tm = PROJ_TM
    in_specs = [pl.BlockSpec((tm, D_MODEL), lambda i: (i, 0))]
    in_specs += [pl.BlockSpec((tm, y.shape[1]), lambda i: (i, 0)) for y in ys]
    in_specs += [pl.BlockSpec(w.shape, lambda i: (0, 0)) for w in ws]
    return pl.pallas_call(
        _out_kernel,
        name="mix_out",
        grid=(t // tm,),
        in_specs=in_specs,
        out_specs=pl.BlockSpec((tm, D_MODEL), lambda i: (i, 0)),
        out_shape=jax.ShapeDtypeStruct(h2d.shape, F32),
        compiler_params=_params("parallel"),
    )(h2d, *ys, *ws)


def _final_norm_kernel(h_ref, g_ref, o_ref):
    o_ref[...] = _rms(h_ref[...], g_ref[...])


def _final_norm(h2d, g):
    t = h2d.shape[0]
    tm = PROJ_TM
    return pl.pallas_call(
        _final_norm_kernel,
        name="final_norm",
        grid=(t // tm,),
        in_specs=[pl.BlockSpec((tm, D_MODEL), lambda i: (i, 0)), pl.BlockSpec((1, D_MODEL), lambda i: (0, 0))],
        out_specs=pl.BlockSpec((tm, D_MODEL), lambda i: (i, 0)),
        out_shape=jax.ShapeDtypeStruct(h2d.shape, F32),
        compiler_params=_params("parallel"),
    )(h2d, g)


def _rope_lane_tables(positions):
    inv_freq = ROPE_THETA ** (-jnp.arange(0, MLA_ROPE, 2, dtype=F32) / MLA_ROPE)
    ang = positions.astype(F32)[..., None] * inv_freq
    cos, sin = jnp.cos(ang), jnp.sin(ang)
    lead = positions.shape
    return _rope_group(cos, cos, lead), _rope_group(sin, sin, lead)


def kernel(x, positions, norm_ffn, norm_mix, norm_final, ffn_w_gate, ffn_w_up, ffn_w_down,
           ab_w_in, ab_b_forget, pool_w, pool_scale, ab_w_out,
           mla_w_in, mla_q_norm, mla_kv_norm, mla_w_q_b, mla_w_kv_b, mla_w_out):
    b, s, d = x.shape
    t = b * s
    cos_l, sin_l = _rope_lane_tables(positions)
    h = x.astype(F32)
    for layer in range(DEPTH):
        idx = layer // 2
        wgu, wd = _prep_ffn(ffn_w_gate[layer, 0], ffn_w_up[layer, 0], ffn_w_down[layer, 0])
        h = _ffn(h.reshape(t, d), norm_ffn[layer, 0][None, :], wgu, wd).reshape(b, s, d)
        g_mix = norm_mix[layer][None, :]
        if layer % 2 == 0:
            w = _prep_ab(ab_w_in[idx], ab_b_forget[idx], pool_w[idx], pool_scale[idx], ab_w_out[idx])
            y_pool, q, k, v = _ab_in(h, g_mix, w)
            y_fox = _attention(q, k, v, FOX_HEADS, 1, FOX_HEAD_DIM)
            h = _mix_out(h.reshape(t, d), [y_pool.reshape(t, -1), y_fox.reshape(t, -1)],
                         [w["wo_pool"], w["wo_fox"]]).reshape(b, s, d)
        else:
            w = _prep_mla(mla_w_in[idx], mla_q_norm[idx], mla_kv_norm[idx], mla_w_q_b[idx],
                          mla_w_kv_b[idx], mla_w_out[idx])
            q, k, v = _mla_in(h, g_mix, w, cos_l, sin_l)
            y = _attention(q, k, v, MLA_HEADS, CHUNK, MLA_V)
            h = _mix_out(h.reshape(t, d), [y.reshape(t, -1)], [w["wo"]]).reshape(b, s, d)
        wgu, wd = _prep_ffn(ffn_w_gate[layer, 1], ffn_w_up[layer, 1], ffn_w_down[layer, 1])
        h = _ffn(h.reshape(t, d), norm_ffn[layer, 1][None, :], wgu, wd).reshape(b, s, d)
    return _final_norm(h.reshape(t, d), norm_final[None, :]).reshape(b, s, d)
```

```python
import functools
import math

import jax
import jax.numpy as jnp
from jax import lax
from jax.experimental import pallas as pl
from jax.experimental.pallas import tpu as pltpu

F32 = jnp.float32
BF16 = jnp.bfloat16

D_MODEL = 1024
DEPTH = 4
CHUNK = 64
RMS_EPS = 1e-6
D_FF = 2816
POOL_WINDOWS = (2, 4, 8, 16)
POOL_GROUP = 128
POOL_WIDTH = 512
FOX_HEADS = 8
FOX_HEAD_DIM = 64
FOX_WIDTH = 512
MLA_HEADS = 16
MLA_NOPE = 64
MLA_ROPE = 32
MLA_V = 64
MLA_Q_LORA = 256
MLA_KV_LORA = 128
ROPE_THETA = 10000.0

LANES = 128
SUBLANES = 8
VMEM_LIMIT_BYTES = 56 * 1024 * 1024

FFN_TM = 1024
FFN_FC = 256
FFN_NC = D_FF // FFN_FC
PROJ_TM = 512
ATT_TQ = 1024
ATT_TK = 512
HEADS_PER_STEP = 2
V_ROWS = 80
POOL_HALO = 16

LOG2E = math.log2(math.e)
MASKED = -1e30


def _rms(x, g):
    return x * lax.rsqrt(jnp.mean(x * x, axis=-1, keepdims=True) + RMS_EPS) * g


def _params(*sem):
    return pltpu.CompilerParams(dimension_semantics=sem, vmem_limit_bytes=VMEM_LIMIT_BYTES)


def _ffn_kernel(*refs, n_mix, final_norm):
    h_ref = refs[0]
    y_refs = refs[1:1 + n_mix]
    wo_refs = refs[1 + n_mix:1 + 2 * n_mix]
    rest = refs[1 + 2 * n_mix:]
    g_ref, wgu_ref, wd_ref = rest[:3]
    gf_ref = rest[3] if final_norm else None
    o_ref, xn_ref, acc_ref, act_ref = rest[-4:]

    h = h_ref[...]
    for y_ref, wo_ref in zip(y_refs, wo_refs):
        h = h + jnp.dot(y_ref[...], wo_ref[...], preferred_element_type=F32)
    xn_ref[...] = _rms(h, g_ref[...]).astype(BF16)
    if n_mix:
        o_ref[...] = h

    def hidden(c):
        gu = jnp.dot(xn_ref[...], wgu_ref[c], preferred_element_type=F32)
        gate = gu[:, :FFN_FC]
        up = gu[:, FFN_FC:]
        return (gate * jax.nn.sigmoid(gate) * up).astype(BF16)

    act_ref[0] = hidden(0)
    acc_ref[...] = jnp.zeros_like(acc_ref)

    def chunk_pair(pair, carry):
        c = 2 * pair
        act_ref[1] = hidden(c + 1)
        acc_ref[...] += jnp.dot(act_ref[0], wd_ref[c], preferred_element_type=F32)
        act_ref[0] = hidden(c + 2)
        acc_ref[...] += jnp.dot(act_ref[1], wd_ref[c + 1], preferred_element_type=F32)
        return carry

    assert FFN_NC % 2 == 1
    lax.fori_loop(0, FFN_NC // 2, chunk_pair, 0)
    resid = o_ref[...] if n_mix else h_ref[...]
    out = resid + 0.5 * (acc_ref[...] + jnp.dot(act_ref[0], wd_ref[FFN_NC - 1],
                                                preferred_element_type=F32))
    o_ref[...] = _rms(out, gf_ref[...]) if final_norm else out


def _ffn(h2d, g, wgu, wd, mix=(), final_g=None):
    t = h2d.shape[0]
    tm = FFN_TM
    row = lambda width: pl.BlockSpec((tm, width), lambda i: (i, 0))
    const = lambda shape: pl.BlockSpec(shape, lambda i: (0,) * len(shape), pipeline_mode=pl.Buffered(1))
    ys = [y for y, _ in mix]
    wos = [w for _, w in mix]
    in_specs = [row(D_MODEL)] + [row(y.shape[1]) for y in ys] + [const(w.shape) for w in wos]
    in_specs += [const((1, D_MODEL)), const((FFN_NC, D_MODEL, 2 * FFN_FC)), const((FFN_NC, FFN_FC, D_MODEL))]
    args = [h2d, *ys, *wos, g, wgu, wd]
    if final_g is not None:
        in_specs.append(const((1, D_MODEL)))
        args.append(final_g)
    return pl.pallas_call(
        functools.partial(_ffn_kernel, n_mix=len(mix), final_norm=final_g is not None),
        name="ffn",
        grid=(t // tm,),
        in_specs=in_specs,
        out_specs=row(D_MODEL),
        out_shape=jax.ShapeDtypeStruct(h2d.shape, F32),
        scratch_shapes=[
            pltpu.VMEM((tm, D_MODEL), BF16),
            pltpu.VMEM((tm, D_MODEL), F32),
            pltpu.VMEM((2, tm, FFN_FC), BF16),
        ],
        compiler_params=_params("parallel"),
    )(*args)


def _prep_ffn(w_gate, w_up, w_down):
    wg = w_gate.astype(BF16).reshape(D_MODEL, FFN_NC, FFN_FC)
    wu = w_up.astype(BF16).reshape(D_MODEL, FFN_NC, FFN_FC)
    wgu = jnp.concatenate([wg, wu], axis=-1).transpose(1, 0, 2)
    wd = w_down.astype(BF16).reshape(FFN_NC, FFN_FC, D_MODEL)
    return wgu, wd


def _ab_in_kernel(h_ref, g_ref, wu_ref, wq_ref, wk_ref, wv_ref, wf_ref, bf_ref, tri_ref,
                  pq_ref, pk_ref, vone_ref, wpool_ref, pscale_ref,
                  ypool_ref, q_ref, k_ref, v_ref, halo_ref, fcarry_ref):
    i = pl.program_id(1)
    tm = h_ref.shape[1]

    @pl.when(i == 0)
    def _():
        halo_ref[...] = jnp.zeros_like(halo_ref)
        fcarry_ref[...] = jnp.zeros_like(fcarry_ref)

    hn = _rms(h_ref[0], g_ref[...]).astype(BF16)

    logit = jnp.dot(hn, wf_ref[...], preferred_element_type=F32) + bf_ref[...]
    log_f = jnp.minimum(logit, 0.0) - jnp.log1p(jnp.exp(-jnp.abs(logit)))
    tri = tri_ref[...]

    def split3(x):
        hi = x.astype(BF16)
        r1 = x - hi.astype(F32)
        mid = r1.astype(BF16)
        lo = (r1 - mid.astype(F32)).astype(BF16)
        return hi, mid, lo

    hi, mid, lo = split3(log_f)
    csum = (jnp.dot(tri, hi, preferred_element_type=F32)
            + jnp.dot(tri, mid, preferred_element_type=F32)
            + jnp.dot(tri, lo, preferred_element_type=F32))
    cum_f = csum + fcarry_ref[0:1, :]
    fcarry_ref[...] = jnp.broadcast_to(cum_f[tm - 1:tm, :], fcarry_ref.shape)

    fh, fm, fl = (x.astype(F32) for x in split3(cum_f * LOG2E))
    lane = lax.broadcasted_iota(jnp.int32, fh.shape, 1)
    xterms = jnp.where(lane < 8, fh, jnp.where(lane < 16, fm, jnp.where(
        lane < 24, fl, jnp.where(lane == 24, 1.0, 0.0)))).astype(BF16)

    q = jnp.dot(hn, wq_ref[...], preferred_element_type=F32) * (FOX_HEAD_DIM ** -0.5 * LOG2E)
    q = q + jnp.dot(xterms, pq_ref[...], preferred_element_type=F32)
    q_ref[0] = q.T.astype(BF16)
    k = jnp.dot(hn, wk_ref[...], preferred_element_type=F32)
    k = k + jnp.dot(xterms, pk_ref[...], preferred_element_type=F32)
    k_ref[0] = k.astype(BF16)
    v = jnp.dot(hn, wv_ref[...], preferred_element_type=F32) + vone_ref[...]
    v_ref[0] = v.T.astype(BF16)

    u = jnp.dot(hn, wu_ref[...], preferred_element_type=F32)
    ext = jnp.concatenate([halo_ref[...], u], axis=0)
    halo_ref[...] = u[tm - POOL_HALO:, :]
    t_pos = i * tm + lax.broadcasted_iota(jnp.int32, (tm, POOL_GROUP), 0)
    sums = ext
    outs = []
    for g, w in enumerate(POOL_WINDOWS):
        sums = sums + pltpu.roll(sums, w // 2, axis=0)
        win = sums[POOL_HALO:, :POOL_GROUP]
        count = jnp.minimum(t_pos + 1, w).astype(F32)
        diff = win / count - u[:, g * POOL_GROUP:(g + 1) * POOL_GROUP]
        outs.append(jnp.dot(diff.astype(BF16), wpool_ref[g], preferred_element_type=F32))
        if g + 1 < len(POOL_WINDOWS):
            sums = sums[:, POOL_GROUP:]
    y = jnp.concatenate(outs, axis=-1) * pscale_ref[...]
    ypool_ref[0] = y.astype(BF16)


def _ab_in(h, g, w):
    b, s, _ = h.shape
    tm = PROJ_TM
    hw = FOX_HEADS * LANES
    const2 = lambda shape: pl.BlockSpec(shape, lambda bi, i: (0,) * len(shape))
    tok = lambda width: pl.BlockSpec((1, tm, width), lambda bi, i: (bi, i, 0))
    tok_t = lambda width: pl.BlockSpec((1, width, tm), lambda bi, i: (bi, 0, i))
    return pl.pallas_call(
        _ab_in_kernel,
        name="ab_in",
        grid=(b, s // tm),
        in_specs=[
            tok(D_MODEL), const2((1, D_MODEL)),
            const2((D_MODEL, POOL_WIDTH)), const2((D_MODEL, hw)), const2((D_MODEL, hw)),
            const2((D_MODEL, hw)), const2((D_MODEL, LANES)), const2((1, LANES)),
            const2((tm, tm)), const2((LANES, hw)), const2((LANES, hw)), const2((1, hw)),
            const2((len(POOL_WINDOWS), POOL_GROUP, POOL_GROUP)), const2((1, POOL_WIDTH)),
        ],
        out_specs=[tok(POOL_WIDTH), tok_t(hw), tok(hw), tok_t(hw)],
        out_shape=[
            jax.ShapeDtypeStruct((b, s, POOL_WIDTH), BF16),
            jax.ShapeDtypeStruct((b, hw, s), BF16),
            jax.ShapeDtypeStruct((b, s, hw), BF16),
            jax.ShapeDtypeStruct((b, hw, s), BF16),
        ],
        scratch_shapes=[
            pltpu.VMEM((POOL_HALO, POOL_WIDTH), F32),
            pltpu.VMEM((SUBLANES, LANES), F32),
        ],
        compiler_params=_params("arbitrary", "arbitrary"),
    )(h, g, w["wu"], w["wq"], w["wk"], w["wv"], w["wf"], w["bf"], w["tri"],
      w["pq"], w["pk"], w["vone"], w["wpool"], w["pscale"])


def _head_groups(w, heads, width):
    rows = w.shape[0]
    w = w.reshape(rows, heads, width)
    w = jnp.pad(w, ((0, 0), (0, 0), (0, LANES - width)))
    return w.reshape(rows, heads * LANES)


def _prep_ab(w_in, b_forget, w_pool, pool_scale, w_out):
    o1, o2, o3, o4 = POOL_WIDTH, POOL_WIDTH + FOX_WIDTH, POOL_WIDTH + 2 * FOX_WIDTH, POOL_WIDTH + 3 * FOX_WIDTH
    wb = w_in.astype(BF16)
    hw = FOX_HEADS * LANES
    wf = jnp.pad(jnp.tile(wb[:, o4:], (1, 3)), ((0, 0), (0, LANES - 3 * FOX_HEADS)))
    bf = jnp.pad(jnp.tile(b_forget.astype(F32), 3), (0, LANES - 3 * FOX_HEADS))[None, :]
    r = jnp.arange(LANES)[:, None]
    c = jnp.arange(hw)[None, :]
    head, lane = c // LANES, c % LANES
    is_term = r < 3 * FOX_HEADS
    pq = jnp.where(is_term & (head == r % FOX_HEADS) & (lane == FOX_HEAD_DIM + r // FOX_HEADS), 1.0, 0.0)
    pq = pq + jnp.where((r == 3 * FOX_HEADS) & (lane >= FOX_HEAD_DIM + 3) & (lane < FOX_HEAD_DIM + 6), 1.0, 0.0)
    pk = jnp.where(is_term & (head == r % FOX_HEADS) & (lane == FOX_HEAD_DIM + 3 + r // FOX_HEADS), -1.0, 0.0)
    pk = pk + jnp.where((r == 3 * FOX_HEADS) & (lane >= FOX_HEAD_DIM) & (lane < FOX_HEAD_DIM + 3), 1.0, 0.0)
    vone = jnp.where(lane == FOX_HEAD_DIM, 1.0, 0.0).astype(F32)
    tri = jnp.tril(jnp.ones((PROJ_TM, PROJ_TM), BF16))
    wo = w_out.astype(BF16)
    return {
        "wu": wb[:, :o1],
        "wq": _head_groups(wb[:, o1:o2], FOX_HEADS, FOX_HEAD_DIM),
        "wk": _head_groups(wb[:, o2:o3], FOX_HEADS, FOX_HEAD_DIM),
        "wv": _head_groups(wb[:, o3:o4], FOX_HEADS, FOX_HEAD_DIM),
        "wf": wf, "bf": bf, "tri": tri,
        "pq": pq.astype(BF16), "pk": pk.astype(BF16), "vone": vone,
        "wpool": w_pool.astype(BF16), "pscale": pool_scale.astype(F32)[None, :],
        "wo_pool": wo[:POOL_WIDTH], "wo_fox": wo[POOL_WIDTH:],
    }


def _mla_in_kernel(h_ref, g_ref, win_ref, qn_ref, kvn_ref, wqa_ref, wqb_ref, wk_ref, wv_ref,
                   vone_ref, cos_ref, sin_ref, q_ref, k_ref, v_ref):
    hn = _rms(h_ref[0], g_ref[...]).astype(BF16)
    proj = jnp.dot(hn, win_ref[...], preferred_element_type=F32)
    c_q = proj[:, :MLA_Q_LORA]
    c_kv = proj[:, MLA_Q_LORA:MLA_Q_LORA + MLA_KV_LORA]
    kr_a = proj[:, MLA_Q_LORA + MLA_KV_LORA:MLA_Q_LORA + MLA_KV_LORA + LANES]
    kr_b = proj[:, MLA_Q_LORA + MLA_KV_LORA + LANES:]
    cos = cos_ref[0]
    sin = sin_ref[0]
    k_rope = kr_a * cos + kr_b * sin

    qn = _rms(c_q, qn_ref[...]).astype(BF16)
    kvn = _rms(c_kv, kvn_ref[...]).astype(BF16)

    scale = (MLA_NOPE + MLA_ROPE) ** -0.5 * LOG2E
    lane = lax.broadcasted_iota(jnp.int32, cos.shape, 1)
    cos_q = (cos + jnp.where(lane < MLA_NOPE, 1.0, 0.0)) * scale
    sin_q = sin * scale
    q_a = jnp.dot(qn, wqa_ref[...], preferred_element_type=F32)
    q_b = jnp.dot(qn, wqb_ref[...], preferred_element_type=F32)
    k_all = jnp.dot(kvn, wk_ref[...], preferred_element_type=F32)
    v_all = jnp.dot(kvn, wv_ref[...], preferred_element_type=F32) + vone_ref[...]
    for hd in range(MLA_HEADS):
        grp = slice(hd * LANES, (hd + 1) * LANES)
        q_ref[0, grp, :] = (q_a[:, grp] * cos_q + q_b[:, grp] * sin_q).T.astype(BF16)
        k_ref[0, :, grp] = (k_all[:, grp] + k_rope).astype(BF16)
        v_ref[0, grp, :] = v_all[:, grp].T.astype(BF16)


def _mla_in(h, g, w, cos_l, sin_l):
    b, s, _ = h.shape
    tm = PROJ_TM
    hw = MLA_HEADS * LANES
    nin = MLA_Q_LORA + MLA_KV_LORA + 2 * LANES
    const2 = lambda shape: pl.BlockSpec(shape, lambda bi, i: (0,) * len(shape))
    tok = lambda width: pl.BlockSpec((1, tm, width), lambda bi, i: (bi, i, 0))
    tok_t = lambda width: pl.BlockSpec((1, width, tm), lambda bi, i: (bi, 0, i))
    return pl.pallas_call(
        _mla_in_kernel,
        name="mla_in",
        grid=(b, s // tm),
        in_specs=[
            tok(D_MODEL), const2((1, D_MODEL)), const2((D_MODEL, nin)),
            const2((1, MLA_Q_LORA)), const2((1, MLA_KV_LORA)),
            const2((MLA_Q_LORA, hw)), const2((MLA_Q_LORA, hw)),
            const2((MLA_KV_LORA, hw)), const2((MLA_KV_LORA, hw)), const2((1, hw)),
            tok(LANES), tok(LANES),
        ],
        out_specs=[tok_t(hw), tok(hw), tok_t(hw)],
        out_shape=[jax.ShapeDtypeStruct((b, hw, s), BF16), jax.ShapeDtypeStruct((b, s, hw), BF16),
                   jax.ShapeDtypeStruct((b, hw, s), BF16)],
        compiler_params=_params("parallel", "parallel"),
    )(h, g, w["win"], w["qn"], w["kvn"], w["wqa"], w["wqb"], w["wk"], w["wv"], w["vone"],
      cos_l, sin_l)


def _rope_group(x1, x2, lead):
    z0 = jnp.zeros(lead + (MLA_NOPE,), x1.dtype)
    z1 = jnp.zeros(lead + (LANES - MLA_NOPE - MLA_ROPE,), x1.dtype)
    return jnp.concatenate([z0, x1, x2, z1], axis=-1)


def _prep_mla(w_in, q_norm, kv_norm, w_q_b, w_kv_b, w_out):
    half = MLA_ROPE // 2
    wb = w_in.astype(BF16)
    kr = wb[:, MLA_Q_LORA + MLA_KV_LORA:]
    a1, a2 = kr[:, :half], kr[:, half:]
    win = jnp.concatenate([
        wb[:, :MLA_Q_LORA + MLA_KV_LORA],
        _rope_group(a1, a2, (D_MODEL,)),
        _rope_group(-a2, a1, (D_MODEL,)),
    ], axis=-1)
    wq = w_q_b.astype(BF16).reshape(MLA_Q_LORA, MLA_HEADS, MLA_NOPE + MLA_ROPE)
    nope, x1, x2 = wq[..., :MLA_NOPE], wq[..., MLA_NOPE:MLA_NOPE + half], wq[..., MLA_NOPE + half:]
    tail = jnp.zeros((MLA_Q_LORA, MLA_HEADS, LANES - MLA_NOPE - MLA_ROPE), BF16)
    wqa = jnp.concatenate([nope, x1, x2, tail], axis=-1).reshape(MLA_Q_LORA, MLA_HEADS * LANES)
    wqb = jnp.concatenate([jnp.zeros_like(nope), -x2, x1, tail], axis=-1).reshape(MLA_Q_LORA, MLA_HEADS * LANES)
    wkv = w_kv_b.astype(BF16).reshape(MLA_KV_LORA, MLA_HEADS, MLA_NOPE + MLA_V)
    pad = jnp.zeros((MLA_KV_LORA, MLA_HEADS, LANES - MLA_NOPE), BF16)
    wk = jnp.concatenate([wkv[..., :MLA_NOPE], pad], axis=-1).reshape(MLA_KV_LORA, MLA_HEADS * LANES)
    wv = jnp.concatenate([wkv[..., MLA_NOPE:], pad], axis=-1).reshape(MLA_KV_LORA, MLA_HEADS * LANES)
    lane = jnp.arange(MLA_HEADS * LANES) % LANES
    vone = jnp.where(lane == MLA_V, 1.0, 0.0).astype(F32)[None, :]
    return {
        "win": win, "qn": q_norm.astype(F32)[None, :], "kvn": kv_norm.astype(F32)[None, :],
        "wqa": wqa, "wqb": wqb, "wk": wk, "wv": wv, "vone": vone, "wo": w_out.astype(BF16),
    }


def _attn_kernel(qt_ref, k_ref, vt_ref, o_ref, m_ref, acc_ref, s_ref, *, tq, tk, chunk, head_dim):
    i = pl.program_id(2)
    shift = chunk.bit_length() - 1
    tiles_per_block = tq // tk
    key = lax.broadcasted_iota(jnp.int32, (tk, tq), 0)
    qry = lax.broadcasted_iota(jnp.int32, (tk, tq), 1)

    m_ref[...] = jnp.full(m_ref.shape, MASKED, F32)
    acc_ref[...] = jnp.zeros_like(acc_ref)
    groups = [slice(hh * LANES, (hh + 1) * LANES) for hh in range(HEADS_PER_STEP)]

    def scores_into(j, slot):
        off = pl.multiple_of(j * tk, tk)
        for hh, grp in enumerate(groups):
            s_ref[slot, hh] = jnp.dot(k_ref[0, pl.ds(off, tk), grp], qt_ref[0, grp, :],
                                      preferred_element_type=F32)

    def softmax_pv(j, slot, diag=None):
        off = pl.multiple_of(j * tk, tk)
        for hh in range(HEADS_PER_STEP):
            s = s_ref[slot, hh]
            vt = vt_ref[0, hh * LANES:hh * LANES + V_ROWS, pl.ds(off, tk)]
            if diag is not None:
                s = jnp.where(((key + diag * tk) >> shift) <= (qry >> shift), s, MASKED)
            m_prev = m_ref[hh]
            m_new = jnp.maximum(m_prev, jnp.max(s, axis=0, keepdims=True))
            alpha = jnp.exp2(m_prev - m_new)
            p = jnp.exp2(s - m_new)
            acc_ref[hh] = acc_ref[hh] * alpha + jnp.dot(vt, p.astype(BF16), preferred_element_type=F32)
            m_ref[hh] = m_new

    first_diag = i * tiles_per_block
    scores_into(0, 0)

    def tile_pair(pair, carry):
        j = 2 * pair
        scores_into(j + 1, 1)
        softmax_pv(j, 0)
        scores_into(j + 2, 0)
        softmax_pv(j + 1, 1)
        return carry

    lax.fori_loop(0, first_diag // 2, tile_pair, 0)
    for d in range(tiles_per_block):
        if d + 1 < tiles_per_block:
            scores_into(first_diag + d + 1, (d + 1) % 2)
        softmax_pv(first_diag + d, d % 2, diag=d)

    outs = []
    for hh in range(HEADS_PER_STEP):
        acc = acc_ref[hh]
        outs.append(acc[:head_dim, :] / acc[head_dim:head_dim + 1, :])
    o_ref[0] = jnp.concatenate(outs, axis=0).T.astype(BF16)


def _attention(qt, k, vt, heads, chunk, head_dim):
    b, s, _ = k.shape
    tq, tk = ATT_TQ, ATT_TK
    assert tq % (2 * tk) == 0 and tk % chunk == 0 and s % tq == 0 and heads % HEADS_PER_STEP == 0
    assert (HEADS_PER_STEP * head_dim) % LANES == 0
    gw = HEADS_PER_STEP * LANES
    return pl.pallas_call(
        functools.partial(_attn_kernel, tq=tq, tk=tk, chunk=chunk, head_dim=head_dim),
        name="attention",
        grid=(b, heads // HEADS_PER_STEP, s // tq),
        in_specs=[
            pl.BlockSpec((1, gw, tq), lambda bi, hp, i: (bi, hp, i)),
            pl.BlockSpec((1, s, gw), lambda bi, hp, i: (bi, 0, hp)),
            pl.BlockSpec((1, gw, s), lambda bi, hp, i: (bi, hp, 0)),
        ],
        out_specs=pl.BlockSpec((1, tq, HEADS_PER_STEP * head_dim), lambda bi, hp, i: (bi, i, hp)),
        out_shape=jax.ShapeDtypeStruct((b, s, heads * head_dim), BF16),
        scratch_shapes=[
            pltpu.VMEM((HEADS_PER_STEP, 1, tq), F32),
            pltpu.VMEM((HEADS_PER_STEP, V_ROWS, tq), F32),
            pltpu.VMEM((2, HEADS_PER_STEP, tk, tq), F32),
        ],
        compiler_params=_params("parallel", "parallel", "arbitrary"),
    )(qt, k, vt)


def _rope_lane_tables(positions):
    inv_freq = ROPE_THETA ** (-jnp.arange(0, MLA_ROPE, 2, dtype=F32) / MLA_ROPE)
    ang = positions.astype(F32)[..., None] * inv_freq
    cos, sin = jnp.cos(ang), jnp.sin(ang)
    lead = positions.shape
    return _rope_group(cos, cos, lead), _rope_group(sin, sin, lead)


def kernel(x, positions, norm_ffn, norm_mix, norm_final, ffn_w_gate, ffn_w_up, ffn_w_down,
           ab_w_in, ab_b_forget, pool_w, pool_scale, ab_w_out,
           mla_w_in, mla_q_norm, mla_kv_norm, mla_w_q_b, mla_w_kv_b, mla_w_out):
    b, s, d = x.shape
    t = b * s
    cos_l, sin_l = _rope_lane_tables(positions)
    h = x.astype(F32).reshape(t, d)
    for layer in range(DEPTH):
        idx = layer // 2
        wgu, wd = _prep_ffn(ffn_w_gate[layer, 0], ffn_w_up[layer, 0], ffn_w_down[layer, 0])
        h = _ffn(h, norm_ffn[layer, 0][None, :], wgu, wd)
        g_mix = norm_mix[layer][None, :]
        h3 = h.reshape(b, s, d)
        if layer % 2 == 0:
            w = _prep_ab(ab_w_in[idx], ab_b_forget[idx], pool_w[idx], pool_scale[idx], ab_w_out[idx])
            y_pool, q, k, v = _ab_in(h3, g_mix, w)
            y_fox = _attention(q, k, v, FOX_HEADS, 1, FOX_HEAD_DIM)
            mix = ((y_pool.reshape(t, -1), w["wo_pool"]), (y_fox.reshape(t, -1), w["wo_fox"]))
        else:
            w = _prep_mla(mla_w_in[idx], mla_q_norm[idx], mla_kv_norm[idx], mla_w_q_b[idx],
                          mla_w_kv_b[idx], mla_w_out[idx])
            q, k, v = _mla_in(h3, g_mix, w, cos_l, sin_l)
            y = _attention(q, k, v, MLA_HEADS, CHUNK, MLA_V)
            mix = ((y.reshape(t, -1), w["wo"]),)
        wgu, wd = _prep_ffn(ffn_w_gate[layer, 1], ffn_w_up[layer, 1], ffn_w_down[layer, 1])
        h = _ffn(h, norm_ffn[layer, 1][None, :], wgu, wd, mix=mix,
                 final_g=norm_final[None, :] if layer == DEPTH - 1 else None)
    return h.reshape(b, s, d)
```

```python
import functools
import math

import jax
import jax.numpy as jnp
from jax import lax
from jax.experimental import pallas as pl
from jax.experimental.pallas import tpu as pltpu

F32 = jnp.float32
BF16 = jnp.bfloat16

D_MODEL = 1024
DEPTH = 4
CHUNK = 64
RMS_EPS = 1e-6
D_FF = 2816
POOL_WINDOWS = (2, 4, 8, 16)
POOL_GROUP = 128
POOL_WIDTH = 512
FOX_HEADS = 8
FOX_HEAD_DIM = 64
FOX_WIDTH = 512
MLA_HEADS = 16
MLA_NOPE = 64
MLA_ROPE = 32
MLA_V = 64
MLA_Q_LORA = 256
MLA_KV_LORA = 128
ROPE_THETA = 10000.0

LANES = 128
SUBLANES = 8
VMEM_LIMIT_BYTES = 56 * 1024 * 1024

FFN_TM = 1024
FFN_FC = 256
FFN_NC = D_FF // FFN_FC
PROJ_TM = 512
ATT_TQ = 1024
ATT_TK = 512
HEADS_PER_STEP = 2
V_ROWS = 80
POOL_HALO = 16

LOG2E = math.log2(math.e)
MASKED = -1e30


def _rms(x, g):
    return x * lax.rsqrt(jnp.mean(x * x, axis=-1, keepdims=True) + RMS_EPS) * g


def _params(*sem):
    return pltpu.CompilerParams(dimension_semantics=sem, vmem_limit_bytes=VMEM_LIMIT_BYTES)


def _ffn_kernel(*refs, n_mix, final_norm):
    h_ref = refs[0]
    y_refs = refs[1:1 + n_mix]
    wo_refs = refs[1 + n_mix:1 + 2 * n_mix]
    rest = refs[1 + 2 * n_mix:]
    g_ref, wgu_ref, wd_ref = rest[:3]
    gf_ref = rest[3] if final_norm else None
    o_ref, xn_ref, acc_ref, act_ref = rest[-4:]

    h = h_ref[0]
    for y_ref, wo_ref in zip(y_refs, wo_refs):
        h = h + jnp.dot(y_ref[0], wo_ref[...], preferred_element_type=F32)
    xn_ref[...] = _rms(h, g_ref[...]).astype(BF16)
    if n_mix:
        o_ref[0] = h

    def hidden(c):
        gu = jnp.dot(xn_ref[...], wgu_ref[c], preferred_element_type=F32)
        gate = gu[:, :FFN_FC]
        up = gu[:, FFN_FC:]
        return (gate * jax.nn.sigmoid(gate) * up).astype(BF16)

    act_ref[0] = hidden(0)
    acc_ref[...] = jnp.zeros_like(acc_ref)

    def chunk_pair(pair, carry):
        c = 2 * pair
        act_ref[1] = hidden(c + 1)
        acc_ref[...] += jnp.dot(act_ref[0], wd_ref[c], preferred_element_type=F32)
        act_ref[0] = hidden(c + 2)
        acc_ref[...] += jnp.dot(act_ref[1], wd_ref[c + 1], preferred_element_type=F32)
        return carry

    assert FFN_NC % 2 == 1
    lax.fori_loop(0, FFN_NC // 2, chunk_pair, 0)
    resid = o_ref[0] if n_mix else h_ref[0]
    out = resid + 0.5 * (acc_ref[...] + jnp.dot(act_ref[0], wd_ref[FFN_NC - 1],
                                                preferred_element_type=F32))
    o_ref[0] = _rms(out, gf_ref[...]) if final_norm else out


def _ffn(h, g, wgu, wd, mix=(), final_g=None):
    b, s, _ = h.shape
    tm = FFN_TM
    row = lambda width: pl.BlockSpec((1, tm, width), lambda bi, i: (bi, i, 0))
    const = lambda shape: pl.BlockSpec(shape, lambda bi, i: (0,) * len(shape), pipeline_mode=pl.Buffered(1))
    ys = [y for y, _ in mix]
    wos = [w for _, w in mix]
    in_specs = [row(D_MODEL)] + [row(y.shape[2]) for y in ys] + [const(w.shape) for w in wos]
    in_specs += [const((1, D_MODEL)), const((FFN_NC, D_MODEL, 2 * FFN_FC)), const((FFN_NC, FFN_FC, D_MODEL))]
    args = [h, *ys, *wos, g, wgu, wd]
    if final_g is not None:
        in_specs.append(const((1, D_MODEL)))
        args.append(final_g)
    return pl.pallas_call(
        functools.partial(_ffn_kernel, n_mix=len(mix), final_norm=final_g is not None),
        name="ffn",
        grid=(b, s // tm),
        in_specs=in_specs,
        out_specs=row(D_MODEL),
        out_shape=jax.ShapeDtypeStruct(h.shape, F32),
        scratch_shapes=[
            pltpu.VMEM((tm, D_MODEL), BF16),
            pltpu.VMEM((tm, D_MODEL), F32),
            pltpu.VMEM((2, tm, FFN_FC), BF16),
        ],
        compiler_params=_params("parallel", "parallel"),
    )(*args)


def _prep_ffn(w_gate, w_up, w_down):
    wg = w_gate.astype(BF16).reshape(D_MODEL, FFN_NC, FFN_FC)
    wu = w_up.astype(BF16).reshape(D_MODEL, FFN_NC, FFN_FC)
    wgu = jnp.concatenate([wg, wu], axis=-1).transpose(1, 0, 2)
    wd = w_down.astype(BF16).reshape(FFN_NC, FFN_FC, D_MODEL)
    return wgu, wd


def _ab_in_kernel(h_ref, g_ref, wu_ref, wq_ref, wk_ref, wv_ref, wf_ref, bf_ref, tri_ref,
                  pqt_ref, pk_ref, wpool_ref, pscale_ref,
                  ypool_ref, q_ref, k_ref, v_ref, halo_ref, fcarry_ref):
    i = pl.program_id(1)
    tm = h_ref.shape[1]

    @pl.when(i == 0)
    def _():
        halo_ref[...] = jnp.zeros_like(halo_ref)
        fcarry_ref[...] = jnp.zeros_like(fcarry_ref)

    hn = _rms(h_ref[0], g_ref[...]).astype(BF16)

    logit = jnp.dot(hn, wf_ref[...], preferred_element_type=F32) + bf_ref[...]
    log_f = jnp.minimum(logit, 0.0) - jnp.log1p(jnp.exp(-jnp.abs(logit)))
    tri = tri_ref[...]

    def split3(x):
        hi = x.astype(BF16)
        r1 = x - hi.astype(F32)
        mid = r1.astype(BF16)
        lo = (r1 - mid.astype(F32)).astype(BF16)
        return hi, mid, lo

    hi, mid, lo = split3(log_f)
    csum = (jnp.dot(tri, hi, preferred_element_type=F32)
            + jnp.dot(tri, mid, preferred_element_type=F32)
            + jnp.dot(tri, lo, preferred_element_type=F32))
    cum_f = csum + fcarry_ref[0:1, :]
    fcarry_ref[...] = jnp.broadcast_to(cum_f[tm - 1:tm, :], fcarry_ref.shape)

    fh, fm, fl = (x.astype(F32) for x in split3(cum_f * LOG2E))
    lane = lax.broadcasted_iota(jnp.int32, fh.shape, 1)
    xterms = jnp.where(lane < 8, fh, jnp.where(lane < 16, fm, jnp.where(
        lane < 24, fl, jnp.where(lane == 24, 1.0, 0.0)))).astype(BF16)

    k = jnp.dot(hn, wk_ref[...], preferred_element_type=F32)
    k = k + jnp.dot(xterms, pk_ref[...], preferred_element_type=F32)
    k_ref[0] = k.astype(BF16)

    def head_rows(xt, filler):
        parts = []
        for hd in range(FOX_HEADS):
            parts += [xt[hd * FOX_HEAD_DIM:(hd + 1) * FOX_HEAD_DIM], filler]
        return jnp.concatenate(parts, axis=0)

    pad_rows = LANES - FOX_HEAD_DIM
    nt = (((1,), (1,)), ((), ()))
    qt = lax.dot_general(wq_ref[...], hn, nt, preferred_element_type=F32) * (FOX_HEAD_DIM ** -0.5 * LOG2E)
    q_extra = lax.dot_general(pqt_ref[...], xterms, (((1,), (1,)), ((), ())),
                              preferred_element_type=F32)
    q_ref[0] = (head_rows(qt, jnp.zeros((pad_rows, tm), F32)) + q_extra).astype(BF16)
    vt = lax.dot_general(wv_ref[...], hn, nt, preferred_element_type=F32)
    ones_row = jnp.where(lax.broadcasted_iota(jnp.int32, (pad_rows, tm), 0) == 0, 1.0, 0.0)
    v_ref[0] = head_rows(vt, ones_row).astype(BF16)

    u = jnp.dot(hn, wu_ref[...], preferred_element_type=F32)
    ext = jnp.concatenate([halo_ref[...], u], axis=0)
    halo_ref[...] = u[tm - POOL_HALO:, :]
    t_pos = i * tm + lax.broadcasted_iota(jnp.int32, (tm, POOL_GROUP), 0)
    sums = ext
    outs = []
    for g, w in enumerate(POOL_WINDOWS):
        sums = sums + pltpu.roll(sums, w // 2, axis=0)
        win = sums[POOL_HALO:, :POOL_GROUP]
        count = jnp.minimum(t_pos + 1, w).astype(F32)
        diff = win / count - u[:, g * POOL_GROUP:(g + 1) * POOL_GROUP]
        outs.append(jnp.dot(diff.astype(BF16), wpool_ref[g], preferred_element_type=F32))
        if g + 1 < len(POOL_WINDOWS):
            sums = sums[:, POOL_GROUP:]
    y = jnp.concatenate(outs, axis=-1) * pscale_ref[...]
    ypool_ref[0] = y.astype(BF16)


def _ab_in(h, g, w):
    b, s, _ = h.shape
    tm = PROJ_TM
    hw = FOX_HEADS * LANES
    const2 = lambda shape: pl.BlockSpec(shape, lambda bi, i: (0,) * len(shape))
    tok = lambda width: pl.BlockSpec((1, tm, width), lambda bi, i: (bi, i, 0))
    tok_t = lambda width: pl.BlockSpec((1, width, tm), lambda bi, i: (bi, 0, i))
    return pl.pallas_call(
        _ab_in_kernel,
        name="ab_in",
        grid=(b, s // tm),
        in_specs=[
            tok(D_MODEL), const2((1, D_MODEL)),
            const2((D_MODEL, POOL_WIDTH)), const2((FOX_WIDTH, D_MODEL)), const2((D_MODEL, hw)),
            const2((FOX_WIDTH, D_MODEL)), const2((D_MODEL, LANES)), const2((1, LANES)),
            const2((tm, tm)), const2((hw, LANES)), const2((LANES, hw)),
            const2((len(POOL_WINDOWS), POOL_GROUP, POOL_GROUP)), const2((1, POOL_WIDTH)),
        ],
        out_specs=[tok(POOL_WIDTH), tok_t(hw), tok(hw), tok_t(hw)],
        out_shape=[
            jax.ShapeDtypeStruct((b, s, POOL_WIDTH), BF16),
            jax.ShapeDtypeStruct((b, hw, s), BF16),
            jax.ShapeDtypeStruct((b, s, hw), BF16),
            jax.ShapeDtypeStruct((b, hw, s), BF16),
        ],
        scratch_shapes=[
            pltpu.VMEM((POOL_HALO, POOL_WIDTH), F32),
            pltpu.VMEM((SUBLANES, LANES), F32),
        ],
        compiler_params=_params("arbitrary", "arbitrary"),
    )(h, g, w["wu"], w["wq"], w["wk"], w["wv"], w["wf"], w["bf"], w["tri"],
      w["pqt"], w["pk"], w["wpool"], w["pscale"])


def _head_groups(w, heads, width):
    rows = w.shape[0]
    w = w.reshape(rows, heads, width)
    w = jnp.pad(w, ((0, 0), (0, 0), (0, LANES - width)))
    return w.reshape(rows, heads * LANES)


def _prep_ab(w_in, b_forget, w_pool, pool_scale, w_out):
    o1, o2, o3, o4 = POOL_WIDTH, POOL_WIDTH + FOX_WIDTH, POOL_WIDTH + 2 * FOX_WIDTH, POOL_WIDTH + 3 * FOX_WIDTH
    wb = w_in.astype(BF16)
    hw = FOX_HEADS * LANES
    wf = jnp.pad(jnp.tile(wb[:, o4:], (1, 3)), ((0, 0), (0, LANES - 3 * FOX_HEADS)))
    bf = jnp.pad(jnp.tile(b_forget.astype(F32), 3), (0, LANES - 3 * FOX_HEADS))[None, :]
    r = jnp.arange(LANES)[:, None]
    c = jnp.arange(hw)[None, :]
    head, lane = c // LANES, c % LANES
    is_term = r < 3 * FOX_HEADS
    pq = jnp.where(is_term & (head == r % FOX_HEADS) & (lane == FOX_HEAD_DIM + r // FOX_HEADS), 1.0, 0.0)
    pq = pq + jnp.where((r == 3 * FOX_HEADS) & (lane >= FOX_HEAD_DIM + 3) & (lane < FOX_HEAD_DIM + 6), 1.0, 0.0)
    pk = jnp.where(is_term & (head == r % FOX_HEADS) & (lane == FOX_HEAD_DIM + 3 + r // FOX_HEADS), -1.0, 0.0)
    pk = pk + jnp.where((r == 3 * FOX_HEADS) & (lane >= FOX_HEAD_DIM) & (lane < FOX_HEAD_DIM + 3), 1.0, 0.0)
    tri = jnp.tril(jnp.ones((PROJ_TM, PROJ_TM), BF16))
    wo = w_out.astype(BF16)
    return {
        "wu": wb[:, :o1],
        "wq": wb[:, o1:o2].T,
        "wk": _head_groups(wb[:, o2:o3], FOX_HEADS, FOX_HEAD_DIM),
        "wv": wb[:, o3:o4].T,
        "wf": wf, "bf": bf, "tri": tri,
        "pqt": pq.astype(BF16).T, "pk": pk.astype(BF16),
        "wpool": w_pool.astype(BF16), "pscale": pool_scale.astype(F32)[None, :],
        "wo_pool": wo[:POOL_WIDTH], "wo_fox": wo[POOL_WIDTH:],
    }


def _mla_in_kernel(h_ref, g_ref, win_ref, qn_ref, kvn_ref, wqa_ref, wqb_ref, wk_ref, wv_ref,
                   vone_ref, cos_ref, sin_ref, q_ref, k_ref, v_ref):
    hn = _rms(h_ref[0], g_ref[...]).astype(BF16)
    proj = jnp.dot(hn, win_ref[...], preferred_element_type=F32)
    c_q = proj[:, :MLA_Q_LORA]
    c_kv = proj[:, MLA_Q_LORA:MLA_Q_LORA + MLA_KV_LORA]
    kr_a = proj[:, MLA_Q_LORA + MLA_KV_LORA:MLA_Q_LORA + MLA_KV_LORA + LANES]
    kr_b = proj[:, MLA_Q_LORA + MLA_KV_LORA + LANES:]
    cos = cos_ref[0]
    sin = sin_ref[0]
    k_rope = kr_a * cos + kr_b * sin

    qn = _rms(c_q, qn_ref[...]).astype(BF16)
    kvn = _rms(c_kv, kvn_ref[...]).astype(BF16)

    scale = (MLA_NOPE + MLA_ROPE) ** -0.5 * LOG2E
    lane = lax.broadcasted_iota(jnp.int32, cos.shape, 1)
    cos_q = (cos + jnp.where(lane < MLA_NOPE, 1.0, 0.0)) * scale
    sin_q = sin * scale
    q_a = jnp.dot(qn, wqa_ref[...], preferred_element_type=F32)
    q_b = jnp.dot(qn, wqb_ref[...], preferred_element_type=F32)
    k_all = jnp.dot(kvn, wk_ref[...], preferred_element_type=F32)
    v_all = jnp.dot(kvn, wv_ref[...], preferred_element_type=F32) + vone_ref[...]
    for hd in range(MLA_HEADS):
        grp = slice(hd * LANES, (hd + 1) * LANES)
        q_ref[0, grp, :] = (q_a[:, grp] * cos_q + q_b[:, grp] * sin_q).T.astype(BF16)
        k_ref[0, :, grp] = (k_all[:, grp] + k_rope).astype(BF16)
        v_ref[0, grp, :] = v_all[:, grp].T.astype(BF16)


def _mla_in(h, g, w, cos_l, sin_l):
    b, s, _ = h.shape
    tm = PROJ_TM
    hw = MLA_HEADS * LANES
    nin = MLA_Q_LORA + MLA_KV_LORA + 2 * LANES
    const2 = lambda shape: pl.BlockSpec(shape, lambda bi, i: (0,) * len(shape))
    tok = lambda width: pl.BlockSpec((1, tm, width), lambda bi, i: (bi, i, 0))
    tok_t = lambda width: pl.BlockSpec((1, width, tm), lambda bi, i: (bi, 0, i))
    return pl.pallas_call(
        _mla_in_kernel,
        name="mla_in",
        grid=(b, s // tm),
        in_specs=[
            tok(D_MODEL), const2((1, D_MODEL)), const2((D_MODEL, nin)),
            const2((1, MLA_Q_LORA)), const2((1, MLA_KV_LORA)),
            const2((MLA_Q_LORA, hw)), const2((MLA_Q_LORA, hw)),
            const2((MLA_KV_LORA, hw)), const2((MLA_KV_LORA, hw)), const2((1, hw)),
            tok(LANES), tok(LANES),
        ],
        out_specs=[tok_t(hw), tok(hw), tok_t(hw)],
        out_shape=[jax.ShapeDtypeStruct((b, hw, s), BF16), jax.ShapeDtypeStruct((b, s, hw), BF16),
                   jax.ShapeDtypeStruct((b, hw, s), BF16)],
        compiler_params=_params("parallel", "parallel"),
    )(h, g, w["win"], w["qn"], w["kvn"], w["wqa"], w["wqb"], w["wk"], w["wv"], w["vone"],
      cos_l, sin_l)


def _rope_group(x1, x2, lead):
    z0 = jnp.zeros(lead + (MLA_NOPE,), x1.dtype)
    z1 = jnp.zeros(lead + (LANES - MLA_NOPE - MLA_ROPE,), x1.dtype)
    return jnp.concatenate([z0, x1, x2, z1], axis=-1)


def _prep_mla(w_in, q_norm, kv_norm, w_q_b, w_kv_b, w_out):
    half = MLA_ROPE // 2
    wb = w_in.astype(BF16)
    kr = wb[:, MLA_Q_LORA + MLA_KV_LORA:]
    a1, a2 = kr[:, :half], kr[:, half:]
    win = jnp.concatenate([
        wb[:, :MLA_Q_LORA + MLA_KV_LORA],
        _rope_group(a1, a2, (D_MODEL,)),
        _rope_group(-a2, a1, (D_MODEL,)),
    ], axis=-1)
    wq = w_q_b.astype(BF16).reshape(MLA_Q_LORA, MLA_HEADS, MLA_NOPE + MLA_ROPE)
    nope, x1, x2 = wq[..., :MLA_NOPE], wq[..., MLA_NOPE:MLA_NOPE + half], wq[..., MLA_NOPE + half:]
    tail = jnp.zeros((MLA_Q_LORA, MLA_HEADS, LANES - MLA_NOPE - MLA_ROPE), BF16)
    wqa = jnp.concatenate([nope, x1, x2, tail], axis=-1).reshape(MLA_Q_LORA, MLA_HEADS * LANES)
    wqb = jnp.concatenate([jnp.zeros_like(nope), -x2, x1, tail], axis=-1).reshape(MLA_Q_LORA, MLA_HEADS * LANES)
    wkv = w_kv_b.astype(BF16).reshape(MLA_KV_LORA, MLA_HEADS, MLA_NOPE + MLA_V)
    pad = jnp.zeros((MLA_KV_LORA, MLA_HEADS, LANES - MLA_NOPE), BF16)
    wk = jnp.concatenate([wkv[..., :MLA_NOPE], pad], axis=-1).reshape(MLA_KV_LORA, MLA_HEADS * LANES)
    wv = jnp.concatenate([wkv[..., MLA_NOPE:], pad], axis=-1).reshape(MLA_KV_LORA, MLA_HEADS * LANES)
    lane = jnp.arange(MLA_HEADS * LANES) % LANES
    vone = jnp.where(lane == MLA_V, 1.0, 0.0).astype(F32)[None, :]
    return {
        "win": win, "qn": q_norm.astype(F32)[None, :], "kvn": kv_norm.astype(F32)[None, :],
        "wqa": wqa, "wqb": wqb, "wk": wk, "wv": wv, "vone": vone, "wo": w_out.astype(BF16),
    }


def _attn_kernel(qt_ref, k_ref, vt_ref, o_ref, m_ref, acc_ref, s_ref, *, tq, tk, chunk, head_dim):
    i = pl.program_id(2)
    shift = chunk.bit_length() - 1
    tiles_per_block = tq // tk

    m_ref[...] = jnp.full(m_ref.shape, MASKED, F32)
    acc_ref[...] = jnp.zeros_like(acc_ref)
    groups = [slice(hh * LANES, (hh + 1) * LANES) for hh in range(HEADS_PER_STEP)]

    def scores_into(j, slot, first_col=0):
        off = pl.multiple_of(j * tk, tk)
        cols = slice(first_col, tq)
        for hh, grp in enumerate(groups):
            s_ref[slot, hh, :, cols] = jnp.dot(k_ref[0, pl.ds(off, tk), grp], qt_ref[0, grp, cols],
                                               preferred_element_type=F32)

    def softmax_pv(j, slot, diag=None):
        off = pl.multiple_of(j * tk, tk)
        cols = slice(0 if diag is None else diag * tk, tq)
        for hh in range(HEADS_PER_STEP):
            s = s_ref[slot, hh, :, cols]
            vt = vt_ref[0, hh * LANES:hh * LANES + V_ROWS, pl.ds(off, tk)]
            if diag is not None:
                key = lax.broadcasted_iota(jnp.int32, s.shape, 0) + diag * tk
                qry = lax.broadcasted_iota(jnp.int32, s.shape, 1) + cols.start
                s = jnp.where((key >> shift) <= (qry >> shift), s, MASKED)
            m_prev = m_ref[hh, 0:1, cols]
            m_new = jnp.maximum(m_prev, jnp.max(s, axis=0, keepdims=True))
            alpha = jnp.exp2(m_prev - m_new)
            p = jnp.exp2(s - m_new)
            acc_ref[hh, :, cols] = acc_ref[hh, :, cols] * alpha + jnp.dot(
                vt, p.astype(BF16), preferred_element_type=F32)
            m_ref[hh, :, cols] = jnp.broadcast_to(m_new, (SUBLANES, m_new.shape[1]))

    first_diag = i * tiles_per_block
    scores_into(0, 0)

    def tile_pair(pair, carry):
        j = 2 * pair
        scores_into(j + 1, 1)
        softmax_pv(j, 0)
        scores_into(j + 2, 0)
        softmax_pv(j + 1, 1)
        return carry

    lax.fori_loop(0, first_diag // 2, tile_pair, 0)
    for d in range(tiles_per_block):
        if d + 1 < tiles_per_block:
            scores_into(first_diag + d + 1, (d + 1) % 2, first_col=(d + 1) * tk)
        softmax_pv(first_diag + d, d % 2, diag=d)

    outs = []
    for hh in range(HEADS_PER_STEP):
        acc = acc_ref[hh]
        outs.append(acc[:head_dim, :] / acc[head_dim:head_dim + 1, :])
    o_ref[0] = jnp.concatenate(outs, axis=0).T.astype(BF16)


def _attention(qt, k, vt, heads, chunk, head_dim):
    b, s, _ = k.shape
    tq, tk = ATT_TQ, ATT_TK
    assert tq % (2 * tk) == 0 and tk % chunk == 0 and s % tq == 0 and heads % HEADS_PER_STEP == 0
    assert (HEADS_PER_STEP * head_dim) % LANES == 0
    gw = HEADS_PER_STEP * LANES
    return pl.pallas_call(
        functools.partial(_attn_kernel, tq=tq, tk=tk, chunk=chunk, head_dim=head_dim),
        name="attention",
        grid=(b, heads // HEADS_PER_STEP, s // tq),
        in_specs=[
            pl.BlockSpec((1, gw, tq), lambda bi, hp, i: (bi, hp, i)),
            pl.BlockSpec((1, s, gw), lambda bi, hp, i: (bi, 0, hp)),
            pl.BlockSpec((1, gw, s), lambda bi, hp, i: (bi, hp, 0)),
        ],
        out_specs=pl.BlockSpec((1, tq, HEADS_PER_STEP * head_dim), lambda bi, hp, i: (bi, i, hp)),
        out_shape=jax.ShapeDtypeStruct((b, s, heads * head_dim), BF16),
        scratch_shapes=[
            pltpu.VMEM((HEADS_PER_STEP, SUBLANES, tq), F32),
            pltpu.VMEM((HEADS_PER_STEP, V_ROWS, tq), F32),
            pltpu.VMEM((2, HEADS_PER_STEP, tk, tq), F32),
        ],
        compiler_params=_params("parallel", "parallel", "arbitrary"),
    )(qt, k, vt)


def _rope_lane_tables(positions):
    inv_freq = ROPE_THETA ** (-jnp.arange(0, MLA_ROPE, 2, dtype=F32) / MLA_ROPE)
    ang = positions.astype(F32)[..., None] * inv_freq
    cos, sin = jnp.cos(ang), jnp.sin(ang)
    lead = positions.shape
    return _rope_group(cos, cos, lead), _rope_group(sin, sin, lead)


def kernel(x, positions, norm_ffn, norm_mix, norm_final, ffn_w_gate, ffn_w_up, ffn_w_down,
           ab_w_in, ab_b_forget, pool_w, pool_scale, ab_w_out,
           mla_w_in, mla_q_norm, mla_kv_norm, mla_w_q_b, mla_w_kv_b, mla_w_out):
    cos_l, sin_l = _rope_lane_tables(positions)
    h = x.astype(F32)
    for layer in range(DEPTH):
        idx = layer // 2
        wgu, wd = _prep_ffn(ffn_w_gate[layer, 0], ffn_w_up[layer, 0], ffn_w_down[layer, 0])
        h = _ffn(h, norm_ffn[layer, 0][None, :], wgu, wd)
        g_mix = norm_mix[layer][None, :]
        if layer % 2 == 0:
            w = _prep_ab(ab_w_in[idx], ab_b_forget[idx], pool_w[idx], pool_scale[idx], ab_w_out[idx])
            y_pool, q, k, v = _ab_in(h, g_mix, w)
            y_fox = _attention(q, k, v, FOX_HEADS, 1, FOX_HEAD_DIM)
            mix = ((y_pool, w["wo_pool"]), (y_fox, w["wo_fox"]))
        else:
            w = _prep_mla(mla_w_in[idx], mla_q_norm[idx], mla_kv_norm[idx], mla_w_q_b[idx],
                          mla_w_kv_b[idx], mla_w_out[idx])
            q, k, v = _mla_in(h, g_mix, w, cos_l, sin_l)
            y = _attention(q, k, v, MLA_HEADS, CHUNK, MLA_V)
            mix = ((y, w["wo"]),)
        wgu, wd = _prep_ffn(ffn_w_gate[layer, 1], ffn_w_up[layer, 1], ffn_w_down[layer, 1])
        h = _ffn(h, norm_ffn[layer, 1][None, :], wgu, wd, mix=mix,
                 final_g=norm_final[None, :] if layer == DEPTH - 1 else None)
    return h
```

```python
import functools
import math

import jax
import jax.numpy as jnp
from jax import lax
from jax.experimental import pallas as pl
from jax.experimental.pallas import tpu as pltpu

F32 = jnp.float32
BF16 = jnp.bfloat16

D_MODEL = 1024
DEPTH = 4
CHUNK = 64
RMS_EPS = 1e-6
D_FF = 2816
POOL_WINDOWS = (2, 4, 8, 16)
POOL_GROUP = 128
POOL_WIDTH = 512
FOX_HEADS = 8
FOX_HEAD_DIM = 64
FOX_WIDTH = 512
MLA_HEADS = 16
MLA_NOPE = 64
MLA_ROPE = 32
MLA_V = 64
MLA_Q_LORA = 256
MLA_KV_LORA = 128
ROPE_THETA = 10000.0

LANES = 128
SUBLANES = 8
VMEM_LIMIT_BYTES = 56 * 1024 * 1024

FFN_TM = 1024
FFN_FC = 256
FFN_NC = D_FF // FFN_FC
PROJ_TM = 512
ATT_TQ = 1024
ATT_TK = 512
HEADS_PER_STEP = 2
V_ROWS = 80
POOL_HALO = 16

LOG2E = math.log2(math.e)
MASKED = -1e30


def _rms(x, g):
    return x * lax.rsqrt(jnp.mean(x * x, axis=-1, keepdims=True) + RMS_EPS) * g


def _params(*sem):
    return pltpu.CompilerParams(dimension_semantics=sem, vmem_limit_bytes=VMEM_LIMIT_BYTES)


def _ffn_kernel(*refs, n_mix, final_norm):
    h_ref = refs[0]
    y_refs = refs[1:1 + n_mix]
    wo_refs = refs[1 + n_mix:1 + 2 * n_mix]
    rest = refs[1 + 2 * n_mix:]
    g_ref, wgu_ref, wd_ref = rest[:3]
    gf_ref = rest[3] if final_norm else None
    o_ref, xn_ref, acc_ref, act_ref = rest[-4:]

    h = h_ref[0]
    for y_ref, wo_ref in zip(y_refs, wo_refs):
        h = h + jnp.dot(y_ref[0], wo_ref[...], preferred_element_type=F32)
    xn_ref[...] = _rms(h, g_ref[...]).astype(BF16)
    if n_mix:
        o_ref[0] = h

    def hidden(c):
        gu = jnp.dot(xn_ref[...], wgu_ref[c], preferred_element_type=F32)
        gate = gu[:, :FFN_FC]
        up = gu[:, FFN_FC:]
        return (gate * jax.nn.sigmoid(gate) * up).astype(BF16)

    act_ref[0] = hidden(0)
    acc_ref[...] = jnp.zeros_like(acc_ref)

    def chunk_pair(pair, carry):
        c = 2 * pair
        act_ref[1] = hidden(c + 1)
        acc_ref[...] += jnp.dot(act_ref[0], wd_ref[c], preferred_element_type=F32)
        act_ref[0] = hidden(c + 2)
        acc_ref[...] += jnp.dot(act_ref[1], wd_ref[c + 1], preferred_element_type=F32)
        return carry

    assert FFN_NC % 2 == 1
    lax.fori_loop(0, FFN_NC // 2, chunk_pair, 0)
    resid = o_ref[0] if n_mix else h_ref[0]
    out = resid + 0.5 * (acc_ref[...] + jnp.dot(act_ref[0], wd_ref[FFN_NC - 1],
                                                preferred_element_type=F32))
    o_ref[0] = _rms(out, gf_ref[...]) if final_norm else out


def _ffn(h, g, wgu, wd, mix=(), final_g=None):
    b, s, _ = h.shape
    tm = FFN_TM
    row = lambda width: pl.BlockSpec((1, tm, width), lambda bi, i: (bi, i, 0))
    const = lambda shape: pl.BlockSpec(shape, lambda bi, i: (0,) * len(shape), pipeline_mode=pl.Buffered(1))
    ys = [y for y, _ in mix]
    wos = [w for _, w in mix]
    in_specs = [row(D_MODEL)] + [row(y.shape[2]) for y in ys] + [const(w.shape) for w in wos]
    in_specs += [const((1, D_MODEL)), const((FFN_NC, D_MODEL, 2 * FFN_FC)), const((FFN_NC, FFN_FC, D_MODEL))]
    args = [h, *ys, *wos, g, wgu, wd]
    if final_g is not None:
        in_specs.append(const((1, D_MODEL)))
        args.append(final_g)
    return pl.pallas_call(
        functools.partial(_ffn_kernel, n_mix=len(mix), final_norm=final_g is not None),
        name="ffn",
        grid=(b, s // tm),
        in_specs=in_specs,
        out_specs=row(D_MODEL),
        out_shape=jax.ShapeDtypeStruct(h.shape, F32),
        scratch_shapes=[
            pltpu.VMEM((tm, D_MODEL), BF16),
            pltpu.VMEM((tm, D_MODEL), F32),
            pltpu.VMEM((2, tm, FFN_FC), BF16),
        ],
        compiler_params=_params("parallel", "parallel"),
    )(*args)


def _prep_ffn(w_gate, w_up, w_down):
    wg = w_gate.astype(BF16).reshape(D_MODEL, FFN_NC, FFN_FC)
    wu = w_up.astype(BF16).reshape(D_MODEL, FFN_NC, FFN_FC)
    wgu = jnp.concatenate([wg, wu], axis=-1).transpose(1, 0, 2)
    wd = w_down.astype(BF16).reshape(FFN_NC, FFN_FC, D_MODEL)
    return wgu, wd


def _ab_in_kernel(h_ref, g_ref, wu_ref, wq_ref, wk_ref, wv_ref, wf_ref, bf_ref, tri_ref,
                  pqt_ref, pk_ref, wpool_ref, pscale_ref,
                  ypool_ref, q_ref, k_ref, v_ref, halo_ref, fcarry_ref):
    i = pl.program_id(1)
    tm = h_ref.shape[1]

    @pl.when(i == 0)
    def _():
        halo_ref[...] = jnp.zeros_like(halo_ref)
        fcarry_ref[...] = jnp.zeros_like(fcarry_ref)

    hn = _rms(h_ref[0], g_ref[...]).astype(BF16)

    logit = jnp.dot(hn, wf_ref[...], preferred_element_type=F32) + bf_ref[...]
    log_f = jnp.minimum(logit, 0.0) - jnp.log1p(jnp.exp(-jnp.abs(logit)))
    tri = tri_ref[...]

    def split3(x):
        hi = x.astype(BF16)
        r1 = x - hi.astype(F32)
        mid = r1.astype(BF16)
        lo = (r1 - mid.astype(F32)).astype(BF16)
        return hi, mid, lo

    hi, mid, lo = split3(log_f)
    csum = (jnp.dot(tri, hi, preferred_element_type=F32)
            + jnp.dot(tri, mid, preferred_element_type=F32)
            + jnp.dot(tri, lo, preferred_element_type=F32))
    cum_f = csum + fcarry_ref[0:1, :]
    fcarry_ref[...] = jnp.broadcast_to(cum_f[tm - 1:tm, :], fcarry_ref.shape)

    fh, fm, fl = (x.astype(F32) for x in split3(cum_f * LOG2E))
    lane = lax.broadcasted_iota(jnp.int32, fh.shape, 1)
    xterms = jnp.where(lane < 8, fh, jnp.where(lane < 16, fm, jnp.where(
        lane < 24, fl, jnp.where(lane == 24, 1.0, 0.0)))).astype(BF16)

    k = jnp.dot(hn, wk_ref[...], preferred_element_type=F32)
    k = k + jnp.dot(xterms, pk_ref[...], preferred_element_type=F32)
    k_ref[0] = k.astype(BF16)

    def head_rows(xt, filler):
        parts = []
        for hd in range(FOX_HEADS):
            parts += [xt[hd * FOX_HEAD_DIM:(hd + 1) * FOX_HEAD_DIM], filler]
        return jnp.concatenate(parts, axis=0)

    pad_rows = LANES - FOX_HEAD_DIM
    nt = (((1,), (1,)), ((), ()))
    qt = lax.dot_general(wq_ref[...], hn, nt, preferred_element_type=F32) * (FOX_HEAD_DIM ** -0.5 * LOG2E)
    q_extra = lax.dot_general(pqt_ref[...], xterms, (((1,), (1,)), ((), ())),
                              preferred_element_type=F32)
    q_ref[0] = (head_rows(qt, jnp.zeros((pad_rows, tm), F32)) + q_extra).astype(BF16)
    vt = lax.dot_general(wv_ref[...], hn, nt, preferred_element_type=F32)
    ones_row = jnp.where(lax.broadcasted_iota(jnp.int32, (pad_rows, tm), 0) == 0, 1.0, 0.0)
    v_ref[0] = head_rows(vt, ones_row).astype(BF16)

    u = jnp.dot(hn, wu_ref[...], preferred_element_type=F32)
    ext = jnp.concatenate([halo_ref[...], u], axis=0)
    halo_ref[...] = u[tm - POOL_HALO:, :]
    t_pos = i * tm + lax.broadcasted_iota(jnp.int32, (tm, POOL_GROUP), 0)
    sums = ext
    outs = []
    for g, w in enumerate(POOL_WINDOWS):
        sums = sums + pltpu.roll(sums, w // 2, axis=0)
        win = sums[POOL_HALO:, :POOL_GROUP]
        count = jnp.minimum(t_pos + 1, w).astype(F32)
        diff = win / count - u[:, g * POOL_GROUP:(g + 1) * POOL_GROUP]
        outs.append(jnp.dot(diff.astype(BF16), wpool_ref[g], preferred_element_type=F32))
        if g + 1 < len(POOL_WINDOWS):
            sums = sums[:, POOL_GROUP:]
    y = jnp.concatenate(outs, axis=-1) * pscale_ref[...]
    ypool_ref[0] = y.astype(BF16)


def _ab_in(h, g, w):
    b, s, _ = h.shape
    tm = PROJ_TM
    hw = FOX_HEADS * LANES
    const2 = lambda shape: pl.BlockSpec(shape, lambda bi, i: (0,) * len(shape))
    tok = lambda width: pl.BlockSpec((1, tm, width), lambda bi, i: (bi, i, 0))
    tok_t = lambda width: pl.BlockSpec((1, width, tm), lambda bi, i: (bi, 0, i))
    return pl.pallas_call(
        _ab_in_kernel,
        name="ab_in",
        grid=(b, s // tm),
        in_specs=[
            tok(D_MODEL), const2((1, D_MODEL)),
            const2((D_MODEL, POOL_WIDTH)), const2((FOX_WIDTH, D_MODEL)), const2((D_MODEL, hw)),
            const2((FOX_WIDTH, D_MODEL)), const2((D_MODEL, LANES)), const2((1, LANES)),
            const2((tm, tm)), const2((hw, LANES)), const2((LANES, hw)),
            const2((len(POOL_WINDOWS), POOL_GROUP, POOL_GROUP)), const2((1, POOL_WIDTH)),
        ],
        out_specs=[tok(POOL_WIDTH), tok_t(hw), tok(hw), tok_t(hw)],
        out_shape=[
            jax.ShapeDtypeStruct((b, s, POOL_WIDTH), BF16),
            jax.ShapeDtypeStruct((b, hw, s), BF16),
            jax.ShapeDtypeStruct((b, s, hw), BF16),
            jax.ShapeDtypeStruct((b, hw, s), BF16),
        ],
        scratch_shapes=[
            pltpu.VMEM((POOL_HALO, POOL_WIDTH), F32),
            pltpu.VMEM((SUBLANES, LANES), F32),
        ],
        compiler_params=_params("arbitrary", "arbitrary"),
    )(h, g, w["wu"], w["wq"], w["wk"], w["wv"], w["wf"], w["bf"], w["tri"],
      w["pqt"], w["pk"], w["wpool"], w["pscale"])


def _head_groups(w, heads, width):
    rows = w.shape[0]
    w = w.reshape(rows, heads, width)
    w = jnp.pad(w, ((0, 0), (0, 0), (0, LANES - width)))
    return w.reshape(rows, heads * LANES)


def _prep_ab(w_in, b_forget, w_pool, pool_scale, w_out):
    o1, o2, o3, o4 = POOL_WIDTH, POOL_WIDTH + FOX_WIDTH, POOL_WIDTH + 2 * FOX_WIDTH, POOL_WIDTH + 3 * FOX_WIDTH
    wb = w_in.astype(BF16)
    hw = FOX_HEADS * LANES
    wf = jnp.pad(jnp.tile(wb[:, o4:], (1, 3)), ((0, 0), (0, LANES - 3 * FOX_HEADS)))
    bf = jnp.pad(jnp.tile(b_forget.astype(F32), 3), (0, LANES - 3 * FOX_HEADS))[None, :]
    r = jnp.arange(LANES)[:, None]
    c = jnp.arange(hw)[None, :]
    head, lane = c // LANES, c % LANES
    is_term = r < 3 * FOX_HEADS
    pq = jnp.where(is_term & (head == r % FOX_HEADS) & (lane == FOX_HEAD_DIM + r // FOX_HEADS), 1.0, 0.0)
    pq = pq + jnp.where((r == 3 * FOX_HEADS) & (lane >= FOX_HEAD_DIM + 3) & (lane < FOX_HEAD_DIM + 6), 1.0, 0.0)
    pk = jnp.where(is_term & (head == r % FOX_HEADS) & (lane == FOX_HEAD_DIM + 3 + r // FOX_HEADS), -1.0, 0.0)
    pk = pk + jnp.where((r == 3 * FOX_HEADS) & (lane >= FOX_HEAD_DIM) & (lane < FOX_HEAD_DIM + 3), 1.0, 0.0)
    tri = jnp.tril(jnp.ones((PROJ_TM, PROJ_TM), BF16))
    wo = w_out.astype(BF16)
    return {
        "wu": wb[:, :o1],
        "wq": wb[:, o1:o2].T,
        "wk": _head_groups(wb[:, o2:o3], FOX_HEADS, FOX_HEAD_DIM),
        "wv": wb[:, o3:o4].T,
        "wf": wf, "bf": bf, "tri": tri,
        "pqt": pq.astype(BF16).T, "pk": pk.astype(BF16),
        "wpool": w_pool.astype(BF16), "pscale": pool_scale.astype(F32)[None, :],
        "wo_pool": wo[:POOL_WIDTH], "wo_fox": wo[POOL_WIDTH:],
    }


def _mla_in_kernel(h_ref, g_ref, win_ref, qn_ref, kvn_ref, wqa_ref, wqb_ref, wk_ref, wv_ref,
                   vone_ref, cos_ref, sin_ref, q_ref, k_ref, v_ref):
    hn = _rms(h_ref[0], g_ref[...]).astype(BF16)
    proj = jnp.dot(hn, win_ref[...], preferred_element_type=F32)
    c_q = proj[:, :MLA_Q_LORA]
    c_kv = proj[:, MLA_Q_LORA:MLA_Q_LORA + MLA_KV_LORA]
    kr_a = proj[:, MLA_Q_LORA + MLA_KV_LORA:MLA_Q_LORA + MLA_KV_LORA + LANES]
    kr_b = proj[:, MLA_Q_LORA + MLA_KV_LORA + LANES:]
    cos = cos_ref[0]
    sin = sin_ref[0]
    k_rope = kr_a * cos + kr_b * sin

    qn = _rms(c_q, qn_ref[...]).astype(BF16)
    kvn = _rms(c_kv, kvn_ref[...]).astype(BF16)

    scale = (MLA_NOPE + MLA_ROPE) ** -0.5 * LOG2E
    lane = lax.broadcasted_iota(jnp.int32, cos.shape, 1)
    cos_q = (cos + jnp.where(lane < MLA_NOPE, 1.0, 0.0)) * scale
    sin_q = sin * scale
    q_a = jnp.dot(qn, wqa_ref[...], preferred_element_type=F32)
    q_b = jnp.dot(qn, wqb_ref[...], preferred_element_type=F32)
    k_all = jnp.dot(kvn, wk_ref[...], preferred_element_type=F32)
    v_all = jnp.dot(kvn, wv_ref[...], preferred_element_type=F32) + vone_ref[...]
    for hd in range(MLA_HEADS):
        grp = slice(hd * LANES, (hd + 1) * LANES)
        q_ref[0, grp, :] = (q_a[:, grp] * cos_q + q_b[:, grp] * sin_q).T.astype(BF16)
        k_ref[0, :, grp] = (k_all[:, grp] + k_rope).astype(BF16)
        v_ref[0, grp, :] = v_all[:, grp].T.astype(BF16)


def _mla_in(h, g, w, cos_l, sin_l):
    b, s, _ = h.shape
    tm = PROJ_TM
    hw = MLA_HEADS * LANES
    nin = MLA_Q_LORA + MLA_KV_LORA + 2 * LANES
    const2 = lambda shape: pl.BlockSpec(shape, lambda bi, i: (0,) * len(shape))
    tok = lambda width: pl.BlockSpec((1, tm, width), lambda bi, i: (bi, i, 0))
    tok_t = lambda width: pl.BlockSpec((1, width, tm), lambda bi, i: (bi, 0, i))
    return pl.pallas_call(
        _mla_in_kernel,
        name="mla_in",
        grid=(b, s // tm),
        in_specs=[
            tok(D_MODEL), const2((1, D_MODEL)), const2((D_MODEL, nin)),
            const2((1, MLA_Q_LORA)), const2((1, MLA_KV_LORA)),
            const2((MLA_Q_LORA, hw)), const2((MLA_Q_LORA, hw)),
            const2((MLA_KV_LORA, hw)), const2((MLA_KV_LORA, hw)), const2((1, hw)),
            tok(LANES), tok(LANES),
        ],
        out_specs=[tok_t(hw), tok(hw), tok_t(hw)],
        out_shape=[jax.ShapeDtypeStruct((b, hw, s), BF16), jax.ShapeDtypeStruct((b, s, hw), BF16),
                   jax.ShapeDtypeStruct((b, hw, s), BF16)],
        compiler_params=_params("parallel", "parallel"),
    )(h, g, w["win"], w["qn"], w["kvn"], w["wqa"], w["wqb"], w["wk"], w["wv"], w["vone"],
      cos_l, sin_l)


def _rope_group(x1, x2, lead):
    z0 = jnp.zeros(lead + (MLA_NOPE,), x1.dtype)
    z1 = jnp.zeros(lead + (LANES - MLA_NOPE - MLA_ROPE,), x1.dtype)
    return jnp.concatenate([z0, x1, x2, z1], axis=-1)


def _prep_mla(w_in, q_norm, kv_norm, w_q_b, w_kv_b, w_out):
    half = MLA_ROPE // 2
    wb = w_in.astype(BF16)
    kr = wb[:, MLA_Q_LORA + MLA_KV_LORA:]
    a1, a2 = kr[:, :half], kr[:, half:]
    win = jnp.concatenate([
        wb[:, :MLA_Q_LORA + MLA_KV_LORA],
        _rope_group(a1, a2, (D_MODEL,)),
        _rope_group(-a2, a1, (D_MODEL,)),
    ], axis=-1)
    wq = w_q_b.astype(BF16).reshape(MLA_Q_LORA, MLA_HEADS, MLA_NOPE + MLA_ROPE)
    nope, x1, x2 = wq[..., :MLA_NOPE], wq[..., MLA_NOPE:MLA_NOPE + half], wq[..., MLA_NOPE + half:]
    tail = jnp.zeros((MLA_Q_LORA, MLA_HEADS, LANES - MLA_NOPE - MLA_ROPE), BF16)
    wqa = jnp.concatenate([nope, x1, x2, tail], axis=-1).reshape(MLA_Q_LORA, MLA_HEADS * LANES)
    wqb = jnp.concatenate([jnp.zeros_like(nope), -x2, x1, tail], axis=-1).reshape(MLA_Q_LORA, MLA_HEADS * LANES)
    wkv = w_kv_b.astype(BF16).reshape(MLA_KV_LORA, MLA_HEADS, MLA_NOPE + MLA_V)
    pad = jnp.zeros((MLA_KV_LORA, MLA_HEADS, LANES - MLA_NOPE), BF16)
    wk = jnp.concatenate([wkv[..., :MLA_NOPE], pad], axis=-1).reshape(MLA_KV_LORA, MLA_HEADS * LANES)
    wv = jnp.concatenate([wkv[..., MLA_NOPE:], pad], axis=-1).reshape(MLA_KV_LORA, MLA_HEADS * LANES)
    lane = jnp.arange(MLA_HEADS * LANES) % LANES
    vone = jnp.where(lane == MLA_V, 1.0, 0.0).astype(F32)[None, :]
    return {
        "win": win, "qn": q_norm.astype(F32)[None, :], "kvn": kv_norm.astype(F32)[None, :],
        "wqa": wqa, "wqb": wqb, "wk": wk, "wv": wv, "vone": vone, "wo": w_out.astype(BF16),
    }


def _attn_kernel(qt_ref, k_ref, vt_ref, o_ref, m_ref, acc_ref, s_ref, smax_ref, *, tq, tk, chunk, head_dim):
    i = pl.program_id(2)
    shift = chunk.bit_length() - 1
    tiles_per_block = tq // tk

    m_ref[...] = jnp.full(m_ref.shape, MASKED, F32)
    acc_ref[...] = jnp.zeros_like(acc_ref)
    groups = [slice(hh * LANES, (hh + 1) * LANES) for hh in range(HEADS_PER_STEP)]

    def scores_into(j, slot, first_col=0):
        off = pl.multiple_of(j * tk, tk)
        cols = slice(first_col, tq)
        for hh, grp in enumerate(groups):
            s = jnp.dot(k_ref[0, pl.ds(off, tk), grp], qt_ref[0, grp, cols],
                        preferred_element_type=F32)
            s_ref[slot, hh, :, cols] = s
            if first_col == 0:
                smax_ref[slot, hh] = jnp.broadcast_to(jnp.max(s, axis=0, keepdims=True), (SUBLANES, tq))

    def softmax_pv(j, slot, diag=None):
        off = pl.multiple_of(j * tk, tk)
        cols = slice(0 if diag is None else diag * tk, tq)
        for hh in range(HEADS_PER_STEP):
            s = s_ref[slot, hh, :, cols]
            vt = vt_ref[0, hh * LANES:hh * LANES + V_ROWS, pl.ds(off, tk)]
            if diag is not None:
                key = lax.broadcasted_iota(jnp.int32, s.shape, 0) + diag * tk
                qry = lax.broadcasted_iota(jnp.int32, s.shape, 1) + cols.start
                s = jnp.where((key >> shift) <= (qry >> shift), s, MASKED)
            m_prev = m_ref[hh, 0:1, cols]
            s_max = smax_ref[slot, hh, 0:1, :] if diag is None else jnp.max(s, axis=0, keepdims=True)
            m_new = jnp.maximum(m_prev, s_max)
            alpha = jnp.exp2(m_prev - m_new)
            p = jnp.exp2(s - m_new)
            acc_ref[hh, :, cols] = acc_ref[hh, :, cols] * alpha + jnp.dot(
                vt, p.astype(BF16), preferred_element_type=F32)
            m_ref[hh, :, cols] = jnp.broadcast_to(m_new, (SUBLANES, m_new.shape[1]))

    first_diag = i * tiles_per_block
    scores_into(0, 0)

    def tile_pair(pair, carry):
        j = 2 * pair
        scores_into(j + 1, 1)
        softmax_pv(j, 0)
        scores_into(j + 2, 0)
        softmax_pv(j + 1, 1)
        return carry

    lax.fori_loop(0, first_diag // 2, tile_pair, 0)
    for d in range(tiles_per_block):
        if d + 1 < tiles_per_block:
            scores_into(first_diag + d + 1, (d + 1) % 2, first_col=(d + 1) * tk)
        softmax_pv(first_diag + d, d % 2, diag=d)

    outs = []
    for hh in range(HEADS_PER_STEP):
        acc = acc_ref[hh]
        outs.append(acc[:head_dim, :] / acc[head_dim:head_dim + 1, :])
    o_ref[0] = jnp.concatenate(outs, axis=0).T.astype(BF16)


def _attention(qt, k, vt, heads, chunk, head_dim):
    b, s, _ = k.shape
    tq, tk = ATT_TQ, ATT_TK
    assert tq % (2 * tk) == 0 and tk % chunk == 0 and s % tq == 0 and heads % HEADS_PER_STEP == 0
    assert (HEADS_PER_STEP * head_dim) % LANES == 0
    gw = HEADS_PER_STEP * LANES
    return pl.pallas_call(
        functools.partial(_attn_kernel, tq=tq, tk=tk, chunk=chunk, head_dim=head_dim),
        name="attention",
        grid=(b, heads // HEADS_PER_STEP, s // tq),
        in_specs=[
            pl.BlockSpec((1, gw, tq), lambda bi, hp, i: (bi, hp, i)),
            pl.BlockSpec((1, s, gw), lambda bi, hp, i: (bi, 0, hp)),
            pl.BlockSpec((1, gw, s), lambda bi, hp, i: (bi, hp, 0)),
        ],
        out_specs=pl.BlockSpec((1, tq, HEADS_PER_STEP * head_dim), lambda bi, hp, i: (bi, i, hp)),
        out_shape=jax.ShapeDtypeStruct((b, s, heads * head_dim), BF16),
        scratch_shapes=[
            pltpu.VMEM((HEADS_PER_STEP, SUBLANES, tq), F32),
            pltpu.VMEM((HEADS_PER_STEP, V_ROWS, tq), F32),
            pltpu.VMEM((2, HEADS_PER_STEP, tk, tq), F32),
            pltpu.VMEM((2, HEADS_PER_STEP, SUBLANES, tq), F32),
        ],
        compiler_params=_params("parallel", "parallel", "arbitrary"),
    )(qt, k, vt)


def _rope_lane_tables(positions):
    inv_freq = ROPE_THETA ** (-jnp.arange(0, MLA_ROPE, 2, dtype=F32) / MLA_ROPE)
    ang = positions.astype(F32)[..., None] * inv_freq
    cos, sin = jnp.cos(ang), jnp.sin(ang)
    lead = positions.shape
    return _rope_group(cos, cos, lead), _rope_group(sin, sin, lead)


def kernel(x, positions, norm_ffn, norm_mix, norm_final, ffn_w_gate, ffn_w_up, ffn_w_down,
           ab_w_in, ab_b_forget, pool_w, pool_scale, ab_w_out,
           mla_w_in, mla_q_norm, mla_kv_norm, mla_w_q_b, mla_w_kv_b, mla_w_out):
    cos_l, sin_l = _rope_lane_tables(positions)
    h = x.astype(F32)
    for layer in range(DEPTH):
        idx = layer // 2
        wgu, wd = _prep_ffn(ffn_w_gate[layer, 0], ffn_w_up[layer, 0], ffn_w_down[layer, 0])
        h = _ffn(h, norm_ffn[layer, 0][None, :], wgu, wd)
        g_mix = norm_mix[layer][None, :]
        if layer % 2 == 0:
            w = _prep_ab(ab_w_in[idx], ab_b_forget[idx], pool_w[idx], pool_scale[idx], ab_w_out[idx])
            y_pool, q, k, v = _ab_in(h, g_mix, w)
            y_fox = _attention(q, k, v, FOX_HEADS, 1, FOX_HEAD_DIM)
            mix = ((y_pool, w["wo_pool"]), (y_fox, w["wo_fox"]))
        else:
            w = _prep_mla(mla_w_in[idx], mla_q_norm[idx], mla_kv_norm[idx], mla_w_q_b[idx],
                          mla_w_kv_b[idx], mla_w_out[idx])
            q, k, v = _mla_in(h, g_mix, w, cos_l, sin_l)
            y = _attention(q, k, v, MLA_HEADS, CHUNK, MLA_V)
            mix = ((y, w["wo"]),)
        wgu, wd = _prep_ffn(ffn_w_gate[layer, 1], ffn_w_up[layer, 1], ffn_w_down[layer, 1])
        h = _ffn(h, norm_ffn[layer, 1][None, :], wgu, wd, mix=mix,
                 final_g=norm_final[None, :] if layer == DEPTH - 1 else None)
    return h
```

```python
import functools
import math

import jax
import jax.numpy as jnp
from jax import lax
from jax.experimental import pallas as pl
from jax.experimental.pallas import tpu as pltpu

F32 = jnp.float32
BF16 = jnp.bfloat16

D_MODEL = 1024
DEPTH = 4
CHUNK = 64
RMS_EPS = 1e-6
D_FF = 2816
POOL_WINDOWS = (2, 4, 8, 16)
POOL_GROUP = 128
POOL_WIDTH = 512
FOX_HEADS = 8
FOX_HEAD_DIM = 64
FOX_WIDTH = 512
MLA_HEADS = 16
MLA_NOPE = 64
MLA_ROPE = 32
MLA_V = 64
MLA_Q_LORA = 256
MLA_KV_LORA = 128
ROPE_THETA = 10000.0

LANES = 128
SUBLANES = 8
VMEM_LIMIT_BYTES = 56 * 1024 * 1024

FFN_TM = 1024
FFN_FC = 256
FFN_NC = D_FF // FFN_FC
PROJ_TM = 512
ATT_TQ = 1024
ATT_TK = 512
HEADS_PER_STEP = 2
V_ROWS = 80
POOL_HALO = 16

LOG2E = math.log2(math.e)
MASKED = -1e30


def _rms(x, g):
    return x * lax.rsqrt(jnp.mean(x * x, axis=-1, keepdims=True) + RMS_EPS) * g


def _params(*sem):
    return pltpu.CompilerParams(dimension_semantics=sem, vmem_limit_bytes=VMEM_LIMIT_BYTES)


def _ffn_kernel(*refs, n_mix, final_norm):
    h_ref = refs[0]
    y_refs = refs[1:1 + n_mix]
    wo_refs = refs[1 + n_mix:1 + 2 * n_mix]
    rest = refs[1 + 2 * n_mix:]
    g_ref, wgu_ref, wd_ref = rest[:3]
    gf_ref = rest[3] if final_norm else None
    o_ref, xn_ref, acc_ref, act_ref = rest[-4:]

    h = h_ref[0]
    for y_ref, wo_ref in zip(y_refs, wo_refs):
        h = h + jnp.dot(y_ref[0], wo_ref[...], preferred_element_type=F32)
    xn_ref[...] = _rms(h, g_ref[...]).astype(BF16)
    if n_mix:
        o_ref[0] = h

    def hidden(c):
        gu = jnp.dot(xn_ref[...], wgu_ref[c], preferred_element_type=F32)
        gate = gu[:, :FFN_FC]
        up = gu[:, FFN_FC:]
        return (gate * jax.nn.sigmoid(gate) * up).astype(BF16)

    act_ref[0] = hidden(0)
    acc_ref[...] = jnp.zeros_like(acc_ref)

    def chunk_pair(pair, carry):
        c = 2 * pair
        act_ref[1] = hidden(c + 1)
        acc_ref[...] += jnp.dot(act_ref[0], wd_ref[c], preferred_element_type=F32)
        act_ref[0] = hidden(c + 2)
        acc_ref[...] += jnp.dot(act_ref[1], wd_ref[c + 1], preferred_element_type=F32)
        return carry

    assert FFN_NC % 2 == 1
    lax.fori_loop(0, FFN_NC // 2, chunk_pair, 0)
    resid = o_ref[0] if n_mix else h_ref[0]
    out = resid + 0.5 * (acc_ref[...] + jnp.dot(act_ref[0], wd_ref[FFN_NC - 1],
                                                preferred_element_type=F32))
    o_ref[0] = _rms(out, gf_ref[...]) if final_norm else out


def _ffn(h, g, wgu, wd, mix=(), final_g=None):
    b, s, _ = h.shape
    tm = FFN_TM
    row = lambda width: pl.BlockSpec((1, tm, width), lambda bi, i: (bi, i, 0))
    const = lambda shape: pl.BlockSpec(shape, lambda bi, i: (0,) * len(shape), pipeline_mode=pl.Buffered(1))
    ys = [y for y, _ in mix]
    wos = [w for _, w in mix]
    in_specs = [row(D_MODEL)] + [row(y.shape[2]) for y in ys] + [const(w.shape) for w in wos]
    in_specs += [const((1, D_MODEL)), const((FFN_NC, D_MODEL, 2 * FFN_FC)), const((FFN_NC, FFN_FC, D_MODEL))]
    args = [h, *ys, *wos, g, wgu, wd]
    if final_g is not None:
        in_specs.append(const((1, D_MODEL)))
        args.append(final_g)
    return pl.pallas_call(
        functools.partial(_ffn_kernel, n_mix=len(mix), final_norm=final_g is not None),
        name="ffn",
        grid=(b, s // tm),
        in_specs=in_specs,
        out_specs=row(D_MODEL),
        out_shape=jax.ShapeDtypeStruct(h.shape, F32),
        scratch_shapes=[
            pltpu.VMEM((tm, D_MODEL), BF16),
            pltpu.VMEM((tm, D_MODEL), F32),
            pltpu.VMEM((2, tm, FFN_FC), BF16),
        ],
        compiler_params=_params("parallel", "parallel"),
    )(*args)


def _prep_ffn(w_gate, w_up, w_down):
    wg = w_gate.astype(BF16).reshape(D_MODEL, FFN_NC, FFN_FC)
    wu = w_up.astype(BF16).reshape(D_MODEL, FFN_NC, FFN_FC)
    wgu = jnp.concatenate([wg, wu], axis=-1).transpose(1, 0, 2)
    wd = w_down.astype(BF16).reshape(FFN_NC, FFN_FC, D_MODEL)
    return wgu, wd


def _ab_in_kernel(h_ref, g_ref, wu_ref, wq_ref, wk_ref, wv_ref, wf_ref, bf_ref, tri_ref,
                  pqt_ref, pk_ref, wpool_ref, pscale_ref,
                  ypool_ref, q_ref, k_ref, v_ref, halo_ref, fcarry_ref):
    i = pl.program_id(1)
    tm = h_ref.shape[1]

    @pl.when(i == 0)
    def _():
        halo_ref[...] = jnp.zeros_like(halo_ref)
        fcarry_ref[...] = jnp.zeros_like(fcarry_ref)

    hn = _rms(h_ref[0], g_ref[...]).astype(BF16)

    logit = jnp.dot(hn, wf_ref[...], preferred_element_type=F32) + bf_ref[...]
    log_f = jnp.minimum(logit, 0.0) - jnp.log1p(jnp.exp(-jnp.abs(logit)))
    tri = tri_ref[...]

    def split3(x):
        hi = x.astype(BF16)
        r1 = x - hi.astype(F32)
        mid = r1.astype(BF16)
        lo = (r1 - mid.astype(F32)).astype(BF16)
        return hi, mid, lo

    hi, mid, lo = split3(log_f)
    csum = (jnp.dot(tri, hi, preferred_element_type=F32)
            + jnp.dot(tri, mid, preferred_element_type=F32)
            + jnp.dot(tri, lo, preferred_element_type=F32))
    cum_f = csum + fcarry_ref[0:1, :]
    fcarry_ref[...] = jnp.broadcast_to(cum_f[tm - 1:tm, :], fcarry_ref.shape)

    fh, fm, fl = (x.astype(F32) for x in split3(cum_f * LOG2E))
    lane = lax.broadcasted_iota(jnp.int32, fh.shape, 1)
    xterms = jnp.where(lane < 8, fh, jnp.where(lane < 16, fm, jnp.where(
        lane < 24, fl, jnp.where(lane == 24, 1.0, 0.0)))).astype(BF16)

    k = jnp.dot(hn, wk_ref[...], preferred_element_type=F32)
    k = k + jnp.dot(xterms, pk_ref[...], preferred_element_type=F32)
    k_ref[0] = k.astype(BF16)

    def head_rows(xt, filler):
        parts = []
        for hd in range(FOX_HEADS):
            parts += [xt[hd * FOX_HEAD_DIM:(hd + 1) * FOX_HEAD_DIM], filler]
        return jnp.concatenate(parts, axis=0)

    pad_rows = LANES - FOX_HEAD_DIM
    nt = (((1,), (1,)), ((), ()))
    qt = lax.dot_general(wq_ref[...], hn, nt, preferred_element_type=F32) * (FOX_HEAD_DIM ** -0.5 * LOG2E)
    q_extra = lax.dot_general(pqt_ref[...], xterms, (((1,), (1,)), ((), ())),
                              preferred_element_type=F32)
    q_ref[0] = (head_rows(qt, jnp.zeros((pad_rows, tm), F32)) + q_extra).astype(BF16)
    vt = lax.dot_general(wv_ref[...], hn, nt, preferred_element_type=F32)
    ones_row = jnp.where(lax.broadcasted_iota(jnp.int32, (pad_rows, tm), 0) == 0, 1.0, 0.0)
    v_ref[0] = head_rows(vt, ones_row).astype(BF16)

    u = jnp.dot(hn, wu_ref[...], preferred_element_type=F32)
    ext = jnp.concatenate([halo_ref[...], u], axis=0)
    halo_ref[...] = u[tm - POOL_HALO:, :]
    t_pos = i * tm + lax.broadcasted_iota(jnp.int32, (tm, POOL_GROUP), 0)
    sums = ext
    outs = []
    for g, w in enumerate(POOL_WINDOWS):
        sums = sums + pltpu.roll(sums, w // 2, axis=0)
        win = sums[POOL_HALO:, :POOL_GROUP]
        count = jnp.minimum(t_pos + 1, w).astype(F32)
        diff = win / count - u[:, g * POOL_GROUP:(g + 1) * POOL_GROUP]
        outs.append(jnp.dot(diff.astype(BF16), wpool_ref[g], preferred_element_type=F32))
        if g + 1 < len(POOL_WINDOWS):
            sums = sums[:, POOL_GROUP:]
    y = jnp.concatenate(outs, axis=-1) * pscale_ref[...]
    ypool_ref[0] = y.astype(BF16)


def _ab_in(h, g, w):
    b, s, _ = h.shape
    tm = PROJ_TM
    hw = FOX_HEADS * LANES
    const2 = lambda shape: pl.BlockSpec(shape, lambda bi, i: (0,) * len(shape))
    tok = lambda width: pl.BlockSpec((1, tm, width), lambda bi, i: (bi, i, 0))
    tok_t = lambda width: pl.BlockSpec((1, width, tm), lambda bi, i: (bi, 0, i))
    return pl.pallas_call(
        _ab_in_kernel,
        name="ab_in",
        grid=(b, s // tm),
        in_specs=[
            tok(D_MODEL), const2((1, D_MODEL)),
            const2((D_MODEL, POOL_WIDTH)), const2((FOX_WIDTH, D_MODEL)), const2((D_MODEL, hw)),
            const2((FOX_WIDTH, D_MODEL)), const2((D_MODEL, LANES)), const2((1, LANES)),
            const2((tm, tm)), const2((hw, LANES)), const2((LANES, hw)),
            const2((len(POOL_WINDOWS), POOL_GROUP, POOL_GROUP)), const2((1, POOL_WIDTH)),
        ],
        out_specs=[tok(POOL_WIDTH), tok_t(hw), tok(hw), tok_t(hw)],
        out_shape=[
            jax.ShapeDtypeStruct((b, s, POOL_WIDTH), BF16),
            jax.ShapeDtypeStruct((b, hw, s), BF16),
            jax.ShapeDtypeStruct((b, s, hw), BF16),
            jax.ShapeDtypeStruct((b, hw, s), BF16),
        ],
        scratch_shapes=[
            pltpu.VMEM((POOL_HALO, POOL_WIDTH), F32),
            pltpu.VMEM((SUBLANES, LANES), F32),
        ],
        compiler_params=_params("arbitrary", "arbitrary"),
    )(h, g, w["wu"], w["wq"], w["wk"], w["wv"], w["wf"], w["bf"], w["tri"],
      w["pqt"], w["pk"], w["wpool"], w["pscale"])


def _head_groups(w, heads, width):
    rows = w.shape[0]
    w = w.reshape(rows, heads, width)
    w = jnp.pad(w, ((0, 0), (0, 0), (0, LANES - width)))
    return w.reshape(rows, heads * LANES)


def _prep_ab(w_in, b_forget, w_pool, pool_scale, w_out):
    o1, o2, o3, o4 = POOL_WIDTH, POOL_WIDTH + FOX_WIDTH, POOL_WIDTH + 2 * FOX_WIDTH, POOL_WIDTH + 3 * FOX_WIDTH
    wb = w_in.astype(BF16)
    hw = FOX_HEADS * LANES
    wf = jnp.pad(jnp.tile(wb[:, o4:], (1, 3)), ((0, 0), (0, LANES - 3 * FOX_HEADS)))
    bf = jnp.pad(jnp.tile(b_forget.astype(F32), 3), (0, LANES - 3 * FOX_HEADS))[None, :]
    r = jnp.arange(LANES)[:, None]
    c = jnp.arange(hw)[None, :]
    head, lane = c // LANES, c % LANES
    is_term = r < 3 * FOX_HEADS
    pq = jnp.where(is_term & (head == r % FOX_HEADS) & (lane == FOX_HEAD_DIM + r // FOX_HEADS), 1.0, 0.0)
    pq = pq + jnp.where((r == 3 * FOX_HEADS) & (lane >= FOX_HEAD_DIM + 3) & (lane < FOX_HEAD_DIM + 6), 1.0, 0.0)
    pk = jnp.where(is_term & (head == r % FOX_HEADS) & (lane == FOX_HEAD_DIM + 3 + r // FOX_HEADS), -1.0, 0.0)
    pk = pk + jnp.where((r == 3 * FOX_HEADS) & (lane >= FOX_HEAD_DIM) & (lane < FOX_HEAD_DIM + 3), 1.0, 0.0)
    tri = jnp.tril(jnp.ones((PROJ_TM, PROJ_TM), BF16))
    wo = w_out.astype(BF16)
    return {
        "wu": wb[:, :o1],
        "wq": wb[:, o1:o2].T,
        "wk": _head_groups(wb[:, o2:o3], FOX_HEADS, FOX_HEAD_DIM),
        "wv": wb[:, o3:o4].T,
        "wf": wf, "bf": bf, "tri": tri,
        "pqt": pq.astype(BF16).T, "pk": pk.astype(BF16),
        "wpool": w_pool.astype(BF16), "pscale": pool_scale.astype(F32)[None, :],
        "wo_pool": wo[:POOL_WIDTH], "wo_fox": wo[POOL_WIDTH:],
    }


def _mla_in_kernel(h_ref, g_ref, win_ref, qn_ref, kvn_ref, wqa_ref, wqb_ref, wk_ref, wv_ref,
                   cos_ref, sin_ref, q_ref, k_ref, v_ref):
    hn = _rms(h_ref[0], g_ref[...]).astype(BF16)
    proj = jnp.dot(hn, win_ref[...], preferred_element_type=F32)
    c_q = proj[:, :MLA_Q_LORA]
    c_kv = proj[:, MLA_Q_LORA:MLA_Q_LORA + MLA_KV_LORA]
    kr_a = proj[:, MLA_Q_LORA + MLA_KV_LORA:MLA_Q_LORA + MLA_KV_LORA + LANES]
    kr_b = proj[:, MLA_Q_LORA + MLA_KV_LORA + LANES:]
    cos = cos_ref[0]
    sin = sin_ref[0]
    k_rope = kr_a * cos + kr_b * sin

    qn = _rms(c_q, qn_ref[...]).astype(BF16)
    kvn = _rms(c_kv, kvn_ref[...]).astype(BF16)

    k_all = jnp.dot(kvn, wk_ref[...], preferred_element_type=F32)
    for hd in range(MLA_HEADS):
        grp = slice(hd * LANES, (hd + 1) * LANES)
        k_ref[0, :, grp] = (k_all[:, grp] + k_rope).astype(BF16)

    nt = (((1,), (1,)), ((), ()))
    scale = (MLA_NOPE + MLA_ROPE) ** -0.5 * LOG2E
    qk_dim = MLA_NOPE + MLA_ROPE
    q_a = lax.dot_general(wqa_ref[...], qn, nt, preferred_element_type=F32)
    q_b = lax.dot_general(wqb_ref[...], qn, nt, preferred_element_type=F32)
    v_t = lax.dot_general(wv_ref[...], kvn, nt, preferred_element_type=F32)
    cos_t = cos.T[MLA_NOPE:qk_dim] * scale
    sin_t = sin.T[MLA_NOPE:qk_dim] * scale
    tm = cos.shape[0]
    q_pad = jnp.zeros((LANES - qk_dim, tm), BF16)
    v_pad = jnp.where(lax.broadcasted_iota(jnp.int32, (LANES - MLA_V, tm), 0) == 0, 1.0, 0.0).astype(BF16)
    for hd in range(MLA_HEADS):
        row = hd * LANES
        qa = q_a[hd * qk_dim:(hd + 1) * qk_dim]
        q_ref[0, row:row + MLA_NOPE, :] = (qa[:MLA_NOPE] * scale).astype(BF16)
        q_ref[0, row + MLA_NOPE:row + qk_dim, :] = (
            qa[MLA_NOPE:] * cos_t + q_b[hd * MLA_ROPE:(hd + 1) * MLA_ROPE] * sin_t).astype(BF16)
        q_ref[0, row + qk_dim:row + LANES, :] = q_pad
        v_ref[0, row:row + MLA_V, :] = v_t[hd * MLA_V:(hd + 1) * MLA_V].astype(BF16)
        v_ref[0, row + MLA_V:row + LANES, :] = v_pad


def _mla_in(h, g, w, cos_l, sin_l):
    b, s, _ = h.shape
    tm = PROJ_TM
    hw = MLA_HEADS * LANES
    nin = MLA_Q_LORA + MLA_KV_LORA + 2 * LANES
    const2 = lambda shape: pl.BlockSpec(shape, lambda bi, i: (0,) * len(shape))
    tok = lambda width: pl.BlockSpec((1, tm, width), lambda bi, i: (bi, i, 0))
    tok_t = lambda width: pl.BlockSpec((1, width, tm), lambda bi, i: (bi, 0, i))
    return pl.pallas_call(
        _mla_in_kernel,
        name="mla_in",
        grid=(b, s // tm),
        in_specs=[
            tok(D_MODEL), const2((1, D_MODEL)), const2((D_MODEL, nin)),
            const2((1, MLA_Q_LORA)), const2((1, MLA_KV_LORA)),
            const2((MLA_HEADS * (MLA_NOPE + MLA_ROPE), MLA_Q_LORA)), const2((MLA_HEADS * MLA_ROPE, MLA_Q_LORA)),
            const2((MLA_KV_LORA, hw)), const2((MLA_HEADS * MLA_V, MLA_KV_LORA)),
            tok(LANES), tok(LANES),
        ],
        out_specs=[tok_t(hw), tok(hw), tok_t(hw)],
        out_shape=[jax.ShapeDtypeStruct((b, hw, s), BF16), jax.ShapeDtypeStruct((b, s, hw), BF16),
                   jax.ShapeDtypeStruct((b, hw, s), BF16)],
        compiler_params=_params("parallel", "parallel"),
    )(h, g, w["win"], w["qn"], w["kvn"], w["wqa"], w["wqb"], w["wk"], w["wv"], cos_l, sin_l)


def _rope_group(x1, x2, lead):
    z0 = jnp.zeros(lead + (MLA_NOPE,), x1.dtype)
    z1 = jnp.zeros(lead + (LANES - MLA_NOPE - MLA_ROPE,), x1.dtype)
    return jnp.concatenate([z0, x1, x2, z1], axis=-1)


def _prep_mla(w_in, q_norm, kv_norm, w_q_b, w_kv_b, w_out):
    half = MLA_ROPE // 2
    wb = w_in.astype(BF16)
    kr = wb[:, MLA_Q_LORA + MLA_KV_LORA:]
    a1, a2 = kr[:, :half], kr[:, half:]
    win = jnp.concatenate([
        wb[:, :MLA_Q_LORA + MLA_KV_LORA],
        _rope_group(a1, a2, (D_MODEL,)),
        _rope_group(-a2, a1, (D_MODEL,)),
    ], axis=-1)
    wq = w_q_b.astype(BF16).reshape(MLA_Q_LORA, MLA_HEADS, MLA_NOPE + MLA_ROPE)
    x1, x2 = wq[..., MLA_NOPE:MLA_NOPE + half], wq[..., MLA_NOPE + half:]
    wqa = w_q_b.astype(BF16).T
    wqb = jnp.concatenate([-x2, x1], axis=-1).reshape(MLA_Q_LORA, MLA_HEADS * MLA_ROPE).T
    wkv = w_kv_b.astype(BF16).reshape(MLA_KV_LORA, MLA_HEADS, MLA_NOPE + MLA_V)
    pad = jnp.zeros((MLA_KV_LORA, MLA_HEADS, LANES - MLA_NOPE), BF16)
    wk = jnp.concatenate([wkv[..., :MLA_NOPE], pad], axis=-1).reshape(MLA_KV_LORA, MLA_HEADS * LANES)
    wv = wkv[..., MLA_NOPE:].reshape(MLA_KV_LORA, MLA_HEADS * MLA_V).T
    return {
        "win": win, "qn": q_norm.astype(F32)[None, :], "kvn": kv_norm.astype(F32)[None, :],
        "wqa": wqa, "wqb": wqb, "wk": wk, "wv": wv, "wo": w_out.astype(BF16),
    }


def _attn_kernel(qt_ref, k_ref, vt_ref, o_ref, m_ref, acc_ref, s_ref, smax_ref, *, tq, tk, chunk, head_dim):
    i = pl.program_id(2)
    shift = chunk.bit_length() - 1
    tiles_per_block = tq // tk

    m_ref[...] = jnp.full(m_ref.shape, MASKED, F32)
    acc_ref[...] = jnp.zeros_like(acc_ref)
    groups = [slice(hh * LANES, (hh + 1) * LANES) for hh in range(HEADS_PER_STEP)]

    def scores_into(j, slot, first_col=0):
        off = pl.multiple_of(j * tk, tk)
        cols = slice(first_col, tq)
        for hh, grp in enumerate(groups):
            s = jnp.dot(k_ref[0, pl.ds(off, tk), grp], qt_ref[0, grp, cols],
                        preferred_element_type=F32)
            s_ref[slot, hh, :, cols] = s
            if first_col == 0:
                smax_ref[slot, hh] = jnp.broadcast_to(jnp.max(s, axis=0, keepdims=True), (SUBLANES, tq))

    def softmax_pv(j, slot, diag=None):
        off = pl.multiple_of(j * tk, tk)
        cols = slice(0 if diag is None else diag * tk, tq)
        for hh in range(HEADS_PER_STEP):
            s = s_ref[slot, hh, :, cols]
            vt = vt_ref[0, hh * LANES:hh * LANES + V_ROWS, pl.ds(off, tk)]
            if diag is not None:
                key = lax.broadcasted_iota(jnp.int32, s.shape, 0) + diag * tk
                qry = lax.broadcasted_iota(jnp.int32, s.shape, 1) + cols.start
                s = jnp.where((key >> shift) <= (qry >> shift), s, MASKED)
            m_prev = m_ref[hh, 0:1, cols]
            s_max = smax_ref[slot, hh, 0:1, :] if diag is None else jnp.max(s, axis=0, keepdims=True)
            m_new = jnp.maximum(m_prev, s_max)
            alpha = jnp.exp2(m_prev - m_new)
            p = jnp.exp2(s - m_new)
            acc_ref[hh, :, cols] = acc_ref[hh, :, cols] * alpha + jnp.dot(
                vt, p.astype(BF16), preferred_element_type=F32)
            m_ref[hh, :, cols] = jnp.broadcast_to(m_new, (SUBLANES, m_new.shape[1]))

    first_diag = i * tiles_per_block
    scores_into(0, 0)

    def tile_pair(pair, carry):
        j = 2 * pair
        scores_into(j + 1, 1)
        softmax_pv(j, 0)
        scores_into(j + 2, 0)
        softmax_pv(j + 1, 1)
        return carry

    lax.fori_loop(0, first_diag // 2, tile_pair, 0)
    for d in range(tiles_per_block):
        if d + 1 < tiles_per_block:
            scores_into(first_diag + d + 1, (d + 1) % 2, first_col=(d + 1) * tk)
        softmax_pv(first_diag + d, d % 2, diag=d)

    outs = []
    for hh in range(HEADS_PER_STEP):
        acc = acc_ref[hh]
        outs.append(acc[:head_dim, :] / acc[head_dim:head_dim + 1, :])
    o_ref[0] = jnp.concatenate(outs, axis=0).T.astype(BF16)


def _attention(qt, k, vt, heads, chunk, head_dim):
    b, s, _ = k.shape
    tq, tk = ATT_TQ, ATT_TK
    assert tq % (2 * tk) == 0 and tk % chunk == 0 and s % tq == 0 and heads % HEADS_PER_STEP == 0
    assert (HEADS_PER_STEP * head_dim) % LANES == 0
    gw = HEADS_PER_STEP * LANES
    return pl.pallas_call(
        functools.partial(_attn_kernel, tq=tq, tk=tk, chunk=chunk, head_dim=head_dim),
        name="attention",
        grid=(b, heads // HEADS_PER_STEP, s // tq),
        in_specs=[
            pl.BlockSpec((1, gw, tq), lambda bi, hp, i: (bi, hp, i)),
            pl.BlockSpec((1, s, gw), lambda bi, hp, i: (bi, 0, hp)),
            pl.BlockSpec((1, gw, s), lambda bi, hp, i: (bi, hp, 0)),
        ],
        out_specs=pl.BlockSpec((1, tq, HEADS_PER_STEP * head_dim), lambda bi, hp, i: (bi, i, hp)),
        out_shape=jax.ShapeDtypeStruct((b, s, heads * head_dim), BF16),
        scratch_shapes=[
            pltpu.VMEM((HEADS_PER_STEP, SUBLANES, tq), F32),
            pltpu.VMEM((HEADS_PER_STEP, V_ROWS, tq), F32),
            pltpu.VMEM((2, HEADS_PER_STEP, tk, tq + LANES), F32),
            pltpu.VMEM((2, HEADS_PER_STEP, SUBLANES, tq), F32),
        ],
        compiler_params=_params("parallel", "parallel", "arbitrary"),
    )(qt, k, vt)


def _rope_lane_tables(positions):
    inv_freq = ROPE_THETA ** (-jnp.arange(0, MLA_ROPE, 2, dtype=F32) / MLA_ROPE)
    ang = positions.astype(F32)[..., None] * inv_freq
    cos, sin = jnp.cos(ang), jnp.sin(ang)
    lead = positions.shape
    return _rope_group(cos, cos, lead), _rope_group(sin, sin, lead)


def kernel(x, positions, norm_ffn, norm_mix, norm_final, ffn_w_gate, ffn_w_up, ffn_w_down,
           ab_w_in, ab_b_forget, pool_w, pool_scale, ab_w_out,
           mla_w_in, mla_q_norm, mla_kv_norm, mla_w_q_b, mla_w_kv_b, mla_w_out):
    cos_l, sin_l = _rope_lane_tables(positions)
    h = x.astype(F32)
    for layer in range(DEPTH):
        idx = layer // 2
        wgu, wd = _prep_ffn(ffn_w_gate[layer, 0], ffn_w_up[layer, 0], ffn_w_down[layer, 0])
        h = _ffn(h, norm_ffn[layer, 0][None, :], wgu, wd)
        g_mix = norm_mix[layer][None, :]
        if layer % 2 == 0:
            w = _prep_ab(ab_w_in[idx], ab_b_forget[idx], pool_w[idx], pool_scale[idx], ab_w_out[idx])
            y_pool, q, k, v = _ab_in(h, g_mix, w)
            y_fox = _attention(q, k, v, FOX_HEADS, 1, FOX_HEAD_DIM)
            mix = ((y_pool, w["wo_pool"]), (y_fox, w["wo_fox"]))
        else:
            w = _prep_mla(mla_w_in[idx], mla_q_norm[idx], mla_kv_norm[idx], mla_w_q_b[idx],
                          mla_w_kv_b[idx], mla_w_out[idx])
            q, k, v = _mla_in(h, g_mix, w, cos_l, sin_l)
            y = _attention(q, k, v, MLA_HEADS, CHUNK, MLA_V)
            mix = ((y, w["wo"]),)
        wgu, wd = _prep_ffn(ffn_w_gate[layer, 1], ffn_w_up[layer, 1], ffn_w_down[layer, 1])
        h = _ffn(h, norm_ffn[layer, 1][None, :], wgu, wd, mix=mix,
                 final_g=norm_final[None, :] if layer == DEPTH - 1 else None)
    return h
```

```python
import functools
import math

import jax
import jax.numpy as jnp
from jax import lax
from jax.experimental import pallas as pl
from jax.experimental.pallas import tpu as pltpu

F32 = jnp.float32
BF16 = jnp.bfloat16

D_MODEL = 1024
DEPTH = 4
CHUNK = 64
RMS_EPS = 1e-6
D_FF = 2816
POOL_WINDOWS = (2, 4, 8, 16)
POOL_GROUP = 128
POOL_WIDTH = 512
FOX_HEADS = 8
FOX_HEAD_DIM = 64
FOX_WIDTH = 512
MLA_HEADS = 16
MLA_NOPE = 64
MLA_ROPE = 32
MLA_V = 64
MLA_Q_LORA = 256
MLA_KV_LORA = 128
ROPE_THETA = 10000.0

LANES = 128
SUBLANES = 8
VMEM_LIMIT_BYTES = 56 * 1024 * 1024

FFN_TM = 1024
FFN_FC = 256
FFN_NC = D_FF // FFN_FC
PROJ_TM = 512
ATT_TQ = 1024
ATT_TK = 512
HEADS_PER_STEP = 2
V_ROWS = 80
POOL_HALO = 16

LOG2E = math.log2(math.e)
MASKED = -1e30


def _rms(x, g):
    return x * lax.rsqrt(jnp.mean(x * x, axis=-1, keepdims=True) + RMS_EPS) * g


def _params(*sem):
    return pltpu.CompilerParams(dimension_semantics=sem, vmem_limit_bytes=VMEM_LIMIT_BYTES)


def _ffn_kernel(*refs, n_mix, final_norm):
    h_ref = refs[0]
    y_refs = refs[1:1 + n_mix]
    wo_refs = refs[1 + n_mix:1 + 2 * n_mix]
    rest = refs[1 + 2 * n_mix:]
    g_ref, wg_ref, wu_ref, wd_ref = rest[:4]
    gf_ref = rest[4] if final_norm else None
    o_ref, xn_ref, acc_ref, act_ref = rest[-4:]

    h = h_ref[0]
    for y_ref, wo_ref in zip(y_refs, wo_refs):
        h = h + jnp.dot(y_ref[0], wo_ref[...], preferred_element_type=F32)
    xn_ref[...] = _rms(h, g_ref[...]).astype(BF16)
    if n_mix:
        o_ref[0] = h

    def hidden(c):
        cols = pl.ds(pl.multiple_of(c * FFN_FC, FFN_FC), FFN_FC)
        gate = jnp.dot(xn_ref[...], wg_ref[:, cols], preferred_element_type=F32)
        up = jnp.dot(xn_ref[...], wu_ref[:, cols], preferred_element_type=F32)
        return (gate * jax.nn.sigmoid(gate) * up).astype(BF16)

    def down(c):
        return wd_ref[pl.ds(pl.multiple_of(c * FFN_FC, FFN_FC), FFN_FC), :]

    act_ref[0] = hidden(0)
    acc_ref[...] = jnp.zeros_like(acc_ref)

    def chunk_pair(pair, carry):
        c = 2 * pair
        act_ref[1] = hidden(c + 1)
        acc_ref[...] += jnp.dot(act_ref[0], down(c), preferred_element_type=F32)
        act_ref[0] = hidden(c + 2)
        acc_ref[...] += jnp.dot(act_ref[1], down(c + 1), preferred_element_type=F32)
        return carry

    assert FFN_NC % 2 == 1
    lax.fori_loop(0, FFN_NC // 2, chunk_pair, 0)
    resid = o_ref[0] if n_mix else h_ref[0]
    out = resid + 0.5 * (acc_ref[...] + jnp.dot(act_ref[0], down(FFN_NC - 1), preferred_element_type=F32))
    o_ref[0] = _rms(out, gf_ref[...]) if final_norm else out


def _ffn(h, g, wg, wu, wd, which, mix=(), final_g=None):
    b, s, _ = h.shape
    tm = FFN_TM
    row = lambda width: pl.BlockSpec((1, tm, width), lambda bi, i: (bi, i, 0))
    const = lambda shape: pl.BlockSpec(shape, lambda bi, i: (0,) * len(shape), pipeline_mode=pl.Buffered(1))
    ys = [y for y, _ in mix]
    wos = [w for _, w in mix]
    in_specs = [row(D_MODEL)] + [row(y.shape[2]) for y in ys] + [const(w.shape) for w in wos]
    pick = lambda rows, cols: pl.BlockSpec((None, None, rows, cols), lambda bi, i: (*which, 0, 0),
                                            pipeline_mode=pl.Buffered(1))
    in_specs += [const((1, D_MODEL)), pick(D_MODEL, D_FF), pick(D_MODEL, D_FF), pick(D_FF, D_MODEL)]
    args = [h, *ys, *wos, g, wg, wu, wd]
    if final_g is not None:
        in_specs.append(const((1, D_MODEL)))
        args.append(final_g)
    return pl.pallas_call(
        functools.partial(_ffn_kernel, n_mix=len(mix), final_norm=final_g is not None),
        name="ffn",
        grid=(b, s // tm),
        in_specs=in_specs,
        out_specs=row(D_MODEL),
        out_shape=jax.ShapeDtypeStruct(h.shape, F32),
        scratch_shapes=[
            pltpu.VMEM((tm, D_MODEL), BF16),
            pltpu.VMEM((tm, D_MODEL), F32),
            pltpu.VMEM((2, tm, FFN_FC), BF16),
        ],
        compiler_params=_params("parallel", "parallel"),
    )(*args)


def _ab_in_kernel(h_ref, g_ref, wu_ref, wq_ref, wk_ref, wv_ref, wf_ref, bf_ref, tri_ref,
                  pqt_ref, pk_ref, wpool_ref, pscale_ref,
                  ypool_ref, q_ref, k_ref, v_ref, halo_ref, fcarry_ref):
    i = pl.program_id(1)
    tm = h_ref.shape[1]

    @pl.when(i == 0)
    def _():
        halo_ref[...] = jnp.zeros_like(halo_ref)
        fcarry_ref[...] = jnp.zeros_like(fcarry_ref)

    hn = _rms(h_ref[0], g_ref[...]).astype(BF16)

    logit = jnp.dot(hn, wf_ref[...], preferred_element_type=F32) + bf_ref[...]
    log_f = jnp.minimum(logit, 0.0) - jnp.log1p(jnp.exp(-jnp.abs(logit)))
    tri = tri_ref[...]

    def split3(x):
        hi = x.astype(BF16)
        r1 = x - hi.astype(F32)
        mid = r1.astype(BF16)
        lo = (r1 - mid.astype(F32)).astype(BF16)
        return hi, mid, lo

    hi, mid, lo = split3(log_f)
    csum = (jnp.dot(tri, hi, preferred_element_type=F32)
            + jnp.dot(tri, mid, preferred_element_type=F32)
            + jnp.dot(tri, lo, preferred_element_type=F32))
    cum_f = csum + fcarry_ref[0:1, :]
    fcarry_ref[...] = jnp.broadcast_to(cum_f[tm - 1:tm, :], fcarry_ref.shape)

    fh, fm, fl = (x.astype(F32) for x in split3(cum_f * LOG2E))
    lane = lax.broadcasted_iota(jnp.int32, fh.shape, 1)
    xterms = jnp.where(lane < 8, fh, jnp.where(lane < 16, fm, jnp.where(
        lane < 24, fl, jnp.where(lane == 24, 1.0, 0.0)))).astype(BF16)

    k = jnp.dot(hn, wk_ref[...], preferred_element_type=F32)
    k = k + jnp.dot(xterms, pk_ref[...], preferred_element_type=F32)
    k_ref[0] = k.astype(BF16)

    def head_rows(xt, filler):
        parts = []
        for hd in range(FOX_HEADS):
            parts += [xt[hd * FOX_HEAD_DIM:(hd + 1) * FOX_HEAD_DIM], filler]
        return jnp.concatenate(parts, axis=0)

    pad_rows = LANES - FOX_HEAD_DIM
    nt = (((1,), (1,)), ((), ()))
    qt = lax.dot_general(wq_ref[...], hn, nt, preferred_element_type=F32) * (FOX_HEAD_DIM ** -0.5 * LOG2E)
    q_extra = lax.dot_general(pqt_ref[...], xterms, (((1,), (1,)), ((), ())),
                              preferred_element_type=F32)
    q_ref[0] = (head_rows(qt, jnp.zeros((pad_rows, tm), F32)) + q_extra).astype(BF16)
    vt = lax.dot_general(wv_ref[...], hn, nt, preferred_element_type=F32)
    ones_row = jnp.where(lax.broadcasted_iota(jnp.int32, (pad_rows, tm), 0) == 0, 1.0, 0.0)
    v_ref[0] = head_rows(vt, ones_row).astype(BF16)

    u = jnp.dot(hn, wu_ref[...], preferred_element_type=F32)
    ext = jnp.concatenate([halo_ref[...], u], axis=0)
    halo_ref[...] = u[tm - POOL_HALO:, :]
    t_pos = i * tm + lax.broadcasted_iota(jnp.int32, (tm, POOL_GROUP), 0)
    sums = ext
    outs = []
    for g, w in enumerate(POOL_WINDOWS):
        sums = sums + pltpu.roll(sums, w // 2, axis=0)
        win = sums[POOL_HALO:, :POOL_GROUP]
        count = jnp.minimum(t_pos + 1, w).astype(F32)
        diff = win / count - u[:, g * POOL_GROUP:(g + 1) * POOL_GROUP]
        outs.append(jnp.dot(diff.astype(BF16), wpool_ref[g], preferred_element_type=F32))
        if g + 1 < len(POOL_WINDOWS):
            sums = sums[:, POOL_GROUP:]
    y = jnp.concatenate(outs, axis=-1) * pscale_ref[...]
    ypool_ref[0] = y.astype(BF16)


def _ab_in(h, g, w):
    b, s, _ = h.shape
    tm = PROJ_TM
    hw = FOX_HEADS * LANES
    const2 = lambda shape: pl.BlockSpec(shape, lambda bi, i: (0,) * len(shape))
    tok = lambda width: pl.BlockSpec((1, tm, width), lambda bi, i: (bi, i, 0))
    tok_t = lambda width: pl.BlockSpec((1, width, tm), lambda bi, i: (bi, 0, i))
    return pl.pallas_call(
        _ab_in_kernel,
        name="ab_in",
        grid=(b, s // tm),
        in_specs=[
            tok(D_MODEL), const2((1, D_MODEL)),
            const2((D_MODEL, POOL_WIDTH)), const2((FOX_WIDTH, D_MODEL)), const2((D_MODEL, hw)),
            const2((FOX_WIDTH, D_MODEL)), const2((D_MODEL, LANES)), const2((1, LANES)),
            const2((tm, tm)), const2((hw, LANES)), const2((LANES, hw)),
            const2((len(POOL_WINDOWS), POOL_GROUP, POOL_GROUP)), const2((1, POOL_WIDTH)),
        ],
        out_specs=[tok(POOL_WIDTH), tok_t(hw), tok(hw), tok_t(hw)],
        out_shape=[
            jax.ShapeDtypeStruct((b, s, POOL_WIDTH), BF16),
            jax.ShapeDtypeStruct((b, hw, s), BF16),
            jax.ShapeDtypeStruct((b, s, hw), BF16),
            jax.ShapeDtypeStruct((b, hw, s), BF16),
        ],
        scratch_shapes=[
            pltpu.VMEM((POOL_HALO, POOL_WIDTH), F32),
            pltpu.VMEM((SUBLANES, LANES), F32),
        ],
        compiler_params=_params("arbitrary", "arbitrary"),
    )(h, g, w["wu"], w["wq"], w["wk"], w["wv"], w["wf"], w["bf"], w["tri"],
      w["pqt"], w["pk"], w["wpool"], w["pscale"])


def _head_groups(w, heads, width):
    rows = w.shape[0]
    w = w.reshape(rows, heads, width)
    w = jnp.pad(w, ((0, 0), (0, 0), (0, LANES - width)))
    return w.reshape(rows, heads * LANES)


def _prep_ab(w_in, b_forget, w_pool, pool_scale, w_out):
    o1, o2, o3, o4 = POOL_WIDTH, POOL_WIDTH + FOX_WIDTH, POOL_WIDTH + 2 * FOX_WIDTH, POOL_WIDTH + 3 * FOX_WIDTH
    wb = w_in.astype(BF16)
    hw = FOX_HEADS * LANES
    wf = jnp.pad(jnp.tile(wb[:, o4:], (1, 3)), ((0, 0), (0, LANES - 3 * FOX_HEADS)))
    bf = jnp.pad(jnp.tile(b_forget.astype(F32), 3), (0, LANES - 3 * FOX_HEADS))[None, :]
    r = jnp.arange(LANES)[:, None]
    c = jnp.arange(hw)[None, :]
    head, lane = c // LANES, c % LANES
    is_term = r < 3 * FOX_HEADS
    pq = jnp.where(is_term & (head == r % FOX_HEADS) & (lane == FOX_HEAD_DIM + r // FOX_HEADS), 1.0, 0.0)
    pq = pq + jnp.where((r == 3 * FOX_HEADS) & (lane >= FOX_HEAD_DIM + 3) & (lane < FOX_HEAD_DIM + 6), 1.0, 0.0)
    pk = jnp.where(is_term & (head == r % FOX_HEADS) & (lane == FOX_HEAD_DIM + 3 + r // FOX_HEADS), -1.0, 0.0)
    pk = pk + jnp.where((r == 3 * FOX_HEADS) & (lane >= FOX_HEAD_DIM) & (lane < FOX_HEAD_DIM + 3), 1.0, 0.0)
    tri = jnp.tril(jnp.ones((PROJ_TM, PROJ_TM), BF16))
    wo = w_out.astype(BF16)
    return {
        "wu": wb[:, :o1],
        "wq": wb[:, o1:o2].T,
        "wk": _head_groups(wb[:, o2:o3], FOX_HEADS, FOX_HEAD_DIM),
        "wv": wb[:, o3:o4].T,
        "wf": wf, "bf": bf, "tri": tri,
        "pqt": pq.astype(BF16).T, "pk": pk.astype(BF16),
        "wpool": w_pool.astype(BF16), "pscale": pool_scale.astype(F32)[None, :],
        "wo_pool": wo[:POOL_WIDTH], "wo_fox": wo[POOL_WIDTH:],
    }


def _mla_in_kernel(h_ref, g_ref, win_ref, qn_ref, kvn_ref, wqa_ref, wqb_ref, wk_ref, wv_ref,
                   cos_ref, sin_ref, q_ref, k_ref, v_ref):
    hn = _rms(h_ref[0], g_ref[...]).astype(BF16)
    proj = jnp.dot(hn, win_ref[...], preferred_element_type=F32)
    c_q = proj[:, :MLA_Q_LORA]
    c_kv = proj[:, MLA_Q_LORA:MLA_Q_LORA + MLA_KV_LORA]
    kr_a = proj[:, MLA_Q_LORA + MLA_KV_LORA:MLA_Q_LORA + MLA_KV_LORA + LANES]
    kr_b = proj[:, MLA_Q_LORA + MLA_KV_LORA + LANES:]
    cos = cos_ref[0]
    sin = sin_ref[0]
    k_rope = kr_a * cos + kr_b * sin

    qn = _rms(c_q, qn_ref[...]).astype(BF16)
    kvn = _rms(c_kv, kvn_ref[...]).astype(BF16)

    k_all = jnp.dot(kvn, wk_ref[...], preferred_element_type=F32)
    for hd in range(MLA_HEADS):
        grp = slice(hd * LANES, (hd + 1) * LANES)
        k_ref[0, :, grp] = (k_all[:, grp] + k_rope).astype(BF16)

    nt = (((1,), (1,)), ((), ()))
    scale = (MLA_NOPE + MLA_ROPE) ** -0.5 * LOG2E
    qk_dim = MLA_NOPE + MLA_ROPE
    q_a = lax.dot_general(wqa_ref[...], qn, nt, preferred_element_type=F32)
    q_b = lax.dot_general(wqb_ref[...], qn, nt, preferred_element_type=F32)
    v_t = lax.dot_general(wv_ref[...], kvn, nt, preferred_element_type=F32)
    cos_t = cos.T[MLA_NOPE:qk_dim] * scale
    sin_t = sin.T[MLA_NOPE:qk_dim] * scale
    tm = cos.shape[0]
    q_pad = jnp.zeros((LANES - qk_dim, tm), BF16)
    v_pad = jnp.where(lax.broadcasted_iota(jnp.int32, (LANES - MLA_V, tm), 0) == 0, 1.0, 0.0).astype(BF16)
    for hd in range(MLA_HEADS):
        row = hd * LANES
        qa = q_a[hd * qk_dim:(hd + 1) * qk_dim]
        q_ref[0, row:row + MLA_NOPE, :] = (qa[:MLA_NOPE] * scale).astype(BF16)
        q_ref[0, row + MLA_NOPE:row + qk_dim, :] = (
            qa[MLA_NOPE:] * cos_t + q_b[hd * MLA_ROPE:(hd + 1) * MLA_ROPE] * sin_t).astype(BF16)
        q_ref[0, row + qk_dim:row + LANES, :] = q_pad
        v_ref[0, row:row + MLA_V, :] = v_t[hd * MLA_V:(hd + 1) * MLA_V].astype(BF16)
        v_ref[0, row + MLA_V:row + LANES, :] = v_pad


def _mla_in(h, g, w, cos_l, sin_l):
    b, s, _ = h.shape
    tm = PROJ_TM
    hw = MLA_HEADS * LANES
    nin = MLA_Q_LORA + MLA_KV_LORA + 2 * LANES
    const2 = lambda shape: pl.BlockSpec(shape, lambda bi, i: (0,) * len(shape))
    tok = lambda width: pl.BlockSpec((1, tm, width), lambda bi, i: (bi, i, 0))
    tok_t = lambda width: pl.BlockSpec((1, width, tm), lambda bi, i: (bi, 0, i))
    return pl.pallas_call(
        _mla_in_kernel,
        name="mla_in",
        grid=(b, s // tm),
        in_specs=[
            tok(D_MODEL), const2((1, D_MODEL)), const2((D_MODEL, nin)),
            const2((1, MLA_Q_LORA)), const2((1, MLA_KV_LORA)),
            const2((MLA_HEADS * (MLA_NOPE + MLA_ROPE), MLA_Q_LORA)), const2((MLA_HEADS * MLA_ROPE, MLA_Q_LORA)),
            const2((MLA_KV_LORA, hw)), const2((MLA_HEADS * MLA_V, MLA_KV_LORA)),
            tok(LANES), tok(LANES),
        ],
        out_specs=[tok_t(hw), tok(hw), tok_t(hw)],
        out_shape=[jax.ShapeDtypeStruct((b, hw, s), BF16), jax.ShapeDtypeStruct((b, s, hw), BF16),
                   jax.ShapeDtypeStruct((b, hw, s), BF16)],
        compiler_params=_params("parallel", "parallel"),
    )(h, g, w["win"], w["qn"], w["kvn"], w["wqa"], w["wqb"], w["wk"], w["wv"], cos_l, sin_l)


def _rope_group(x1, x2, lead):
    z0 = jnp.zeros(lead + (MLA_NOPE,), x1.dtype)
    z1 = jnp.zeros(lead + (LANES - MLA_NOPE - MLA_ROPE,), x1.dtype)
    return jnp.concatenate([z0, x1, x2, z1], axis=-1)


def _prep_mla(w_in, q_norm, kv_norm, w_q_b, w_kv_b, w_out):
    half = MLA_ROPE // 2
    wb = w_in.astype(BF16)
    kr = wb[:, MLA_Q_LORA + MLA_KV_LORA:]
    a1, a2 = kr[:, :half], kr[:, half:]
    win = jnp.concatenate([
        wb[:, :MLA_Q_LORA + MLA_KV_LORA],
        _rope_group(a1, a2, (D_MODEL,)),
        _rope_group(-a2, a1, (D_MODEL,)),
    ], axis=-1)
    wq = w_q_b.astype(BF16).reshape(MLA_Q_LORA, MLA_HEADS, MLA_NOPE + MLA_ROPE)
    x1, x2 = wq[..., MLA_NOPE:MLA_NOPE + half], wq[..., MLA_NOPE + half:]
    wqa = w_q_b.astype(BF16).T
    wqb = jnp.concatenate([-x2, x1], axis=-1).reshape(MLA_Q_LORA, MLA_HEADS * MLA_ROPE).T
    wkv = w_kv_b.astype(BF16).reshape(MLA_KV_LORA, MLA_HEADS, MLA_NOPE + MLA_V)
    pad = jnp.zeros((MLA_KV_LORA, MLA_HEADS, LANES - MLA_NOPE), BF16)
    wk = jnp.concatenate([wkv[..., :MLA_NOPE], pad], axis=-1).reshape(MLA_KV_LORA, MLA_HEADS * LANES)
    wv = wkv[..., MLA_NOPE:].reshape(MLA_KV_LORA, MLA_HEADS * MLA_V).T
    return {
        "win": win, "qn": q_norm.astype(F32)[None, :], "kvn": kv_norm.astype(F32)[None, :],
        "wqa": wqa, "wqb": wqb, "wk": wk, "wv": wv, "wo": w_out.astype(BF16),
    }


def _attn_kernel(qt_ref, k_ref, vt_ref, o_ref, m_ref, acc_ref, s_ref, smax_ref, *, tq, tk, chunk, head_dim):
    i = pl.program_id(2)
    shift = chunk.bit_length() - 1
    tiles_per_block = tq // tk

    m_ref[...] = jnp.full(m_ref.shape, MASKED, F32)
    acc_ref[...] = jnp.zeros_like(acc_ref)
    groups = [slice(hh * LANES, (hh + 1) * LANES) for hh in range(HEADS_PER_STEP)]

    def scores_into(j, slot, first_col=0):
        off = pl.multiple_of(j * tk, tk)
        cols = slice(first_col, tq)
        for hh, grp in enumerate(groups):
            s = jnp.dot(k_ref[0, pl.ds(off, tk), grp], qt_ref[0, grp, cols],
                        preferred_element_type=F32)
            s_ref[slot, hh, :, cols] = s
            if first_col == 0:
                smax_ref[slot, hh] = jnp.broadcast_to(jnp.max(s, axis=0, keepdims=True), (SUBLANES, tq))

    def softmax_pv(j, slot, diag=None):
        off = pl.multiple_of(j * tk, tk)
        cols = slice(0 if diag is None else diag * tk, tq)
        for hh in range(HEADS_PER_STEP):
            s = s_ref[slot, hh, :, cols]
            vt = vt_ref[0, hh * LANES:hh * LANES + V_ROWS, pl.ds(off, tk)]
            if diag is not None:
                key = lax.broadcasted_iota(jnp.int32, s.shape, 0) + diag * tk
                qry = lax.broadcasted_iota(jnp.int32, s.shape, 1) + cols.start
                s = jnp.where((key >> shift) <= (qry >> shift), s, MASKED)
            m_prev = m_ref[hh, 0:1, cols]
            s_max = smax_ref[slot, hh, 0:1, :] if diag is None else jnp.max(s, axis=0, keepdims=True)
            m_new = jnp.maximum(m_prev, s_max)
            alpha = jnp.exp2(m_prev - m_new)
            p = jnp.exp2(s - m_new)
            acc_ref[hh, :, cols] = acc_ref[hh, :, cols] * alpha + jnp.dot(
                vt, p.astype(BF16), preferred_element_type=F32)
            m_ref[hh, :, cols] = jnp.broadcast_to(m_new, (SUBLANES, m_new.shape[1]))

    first_diag = i * tiles_per_block
    scores_into(0, 0)

    def tile_pair(pair, carry):
        j = 2 * pair
        scores_into(j + 1, 1)
        softmax_pv(j, 0)
        scores_into(j + 2, 0)
        softmax_pv(j + 1, 1)
        return carry

    lax.fori_loop(0, first_diag // 2, tile_pair, 0)
    for d in range(tiles_per_block):
        if d + 1 < tiles_per_block:
            scores_into(first_diag + d + 1, (d + 1) % 2, first_col=(d + 1) * tk)
        softmax_pv(first_diag + d, d % 2, diag=d)

    outs = []
    for hh in range(HEADS_PER_STEP):
        acc = acc_ref[hh]
        outs.append(acc[:head_dim, :] / acc[head_dim:head_dim + 1, :])
    o_ref[0] = jnp.concatenate(outs, axis=0).T.astype(BF16)


def _attention(qt, k, vt, heads, chunk, head_dim):
    b, s, _ = k.shape
    tq, tk = ATT_TQ, ATT_TK
    assert tq % (2 * tk) == 0 and tk % chunk == 0 and s % tq == 0 and heads % HEADS_PER_STEP == 0
    assert (HEADS_PER_STEP * head_dim) % LANES == 0
    gw = HEADS_PER_STEP * LANES
    return pl.pallas_call(
        functools.partial(_attn_kernel, tq=tq, tk=tk, chunk=chunk, head_dim=head_dim),
        name="attention",
        grid=(b, heads // HEADS_PER_STEP, s // tq),
        in_specs=[
            pl.BlockSpec((1, gw, tq), lambda bi, hp, i: (bi, hp, i)),
            pl.BlockSpec((1, s, gw), lambda bi, hp, i: (bi, 0, hp)),
            pl.BlockSpec((1, gw, s), lambda bi, hp, i: (bi, hp, 0)),
        ],
        out_specs=pl.BlockSpec((1, tq, HEADS_PER_STEP * head_dim), lambda bi, hp, i: (bi, i, hp)),
        out_shape=jax.ShapeDtypeStruct((b, s, heads * head_dim), BF16),
        scratch_shapes=[
            pltpu.VMEM((HEADS_PER_STEP, SUBLANES, tq), F32),
            pltpu.VMEM((HEADS_PER_STEP, V_ROWS, tq), F32),
            pltpu.VMEM((2, HEADS_PER_STEP, tk, tq), F32),
            pltpu.VMEM((2, HEADS_PER_STEP, SUBLANES, tq), F32),
        ],
        compiler_params=_params("parallel", "parallel", "arbitrary"),
    )(qt, k, vt)


def _rope_lane_tables(positions):
    inv_freq = ROPE_THETA ** (-jnp.arange(0, MLA_ROPE, 2, dtype=F32) / MLA_ROPE)
    ang = positions.astype(F32)[..., None] * inv_freq
    cos, sin = jnp.cos(ang), jnp.sin(ang)
    lead = positions.shape
    return _rope_group(cos, cos, lead), _rope_group(sin, sin, lead)


def kernel(x, positions, norm_ffn, norm_mix, norm_final, ffn_w_gate, ffn_w_up, ffn_w_down,
           ab_w_in, ab_b_forget, pool_w, pool_scale, ab_w_out,
           mla_w_in, mla_q_norm, mla_kv_norm, mla_w_q_b, mla_w_kv_b, mla_w_out):
    cos_l, sin_l = _rope_lane_tables(positions)
    w_gate, w_up, w_down = (w.astype(BF16) for w in (ffn_w_gate, ffn_w_up, ffn_w_down))
    h = x.astype(F32)
    for layer in range(DEPTH):
        idx = layer // 2
        h = _ffn(h, norm_ffn[layer, 0][None, :], w_gate, w_up, w_down, (layer, 0))
        g_mix = norm_mix[layer][None, :]
        if layer % 2 == 0:
            w = _prep_ab(ab_w_in[idx], ab_b_forget[idx], pool_w[idx], pool_scale[idx], ab_w_out[idx])
            y_pool, q, k, v = _ab_in(h, g_mix, w)
            y_fox = _attention(q, k, v, FOX_HEADS, 1, FOX_HEAD_DIM)
            mix = ((y_pool, w["wo_pool"]), (y_fox, w["wo_fox"]))
        else:
            w = _prep_mla(mla_w_in[idx], mla_q_norm[idx], mla_kv_norm[idx], mla_w_q_b[idx],
                          mla_w_kv_b[idx], mla_w_out[idx])
            q, k, v = _mla_in(h, g_mix, w, cos_l, sin_l)
            y = _attention(q, k, v, MLA_HEADS, CHUNK, MLA_V)
            mix = ((y, w["wo"]),)
        h = _ffn(h, norm_ffn[layer, 1][None, :], w_gate, w_up, w_down, (layer, 1), mix=mix,
                 final_g=norm_final[None, :] if layer == DEPTH - 1 else None)
    return h
```

```python
import functools
import math

import jax
import jax.numpy as jnp
from jax import lax
from jax.experimental import pallas as pl
from jax.experimental.pallas import tpu as pltpu

F32 = jnp.float32
BF16 = jnp.bfloat16

D_MODEL = 1024
DEPTH = 4
CHUNK = 64
RMS_EPS = 1e-6
D_FF = 2816
POOL_WINDOWS = (2, 4, 8, 16)
POOL_GROUP = 128
POOL_WIDTH = 512
FOX_HEADS = 8
FOX_HEAD_DIM = 64
FOX_WIDTH = 512
MLA_HEADS = 16
MLA_NOPE = 64
MLA_ROPE = 32
MLA_V = 64
MLA_Q_LORA = 256
MLA_KV_LORA = 128
ROPE_THETA = 10000.0

LANES = 128
SUBLANES = 8
VMEM_LIMIT_BYTES = 56 * 1024 * 1024

FFN_TM = 1024
FFN_FC = 256
FFN_NC = D_FF // FFN_FC
PROJ_TM = 512
ATT_TQ = 1024
ATT_TK = 512
HEADS_PER_STEP = 2
ATT_COLS = 256
V_ROWS = 80
POOL_HALO = 16

LOG2E = math.log2(math.e)
MASKED = -1e30


def _rms(x, g):
    return x * lax.rsqrt(jnp.mean(x * x, axis=-1, keepdims=True) + RMS_EPS) * g


def _params(*sem):
    return pltpu.CompilerParams(dimension_semantics=sem, vmem_limit_bytes=VMEM_LIMIT_BYTES)


def _ffn_kernel(*refs, n_mix, final_norm):
    h_ref = refs[0]
    y_refs = refs[1:1 + n_mix]
    wo_refs = refs[1 + n_mix:1 + 2 * n_mix]
    rest = refs[1 + 2 * n_mix:]
    g_ref, wg_ref, wu_ref, wd_ref = rest[:4]
    gf_ref = rest[4] if final_norm else None
    o_ref, xn_ref, acc_ref, act_ref = rest[-4:]

    h = h_ref[0]
    for y_ref, wo_ref in zip(y_refs, wo_refs):
        h = h + jnp.dot(y_ref[0], wo_ref[...], preferred_element_type=F32)
    xn_ref[...] = _rms(h, g_ref[...]).astype(BF16)
    if n_mix:
        o_ref[0] = h

    def hidden(c):
        cols = pl.ds(pl.multiple_of(c * FFN_FC, FFN_FC), FFN_FC)
        gate = jnp.dot(xn_ref[...], wg_ref[:, cols], preferred_element_type=F32)
        up = jnp.dot(xn_ref[...], wu_ref[:, cols], preferred_element_type=F32)
        return (gate * jax.nn.sigmoid(gate) * up).astype(BF16)

    def down(c):
        return wd_ref[pl.ds(pl.multiple_of(c * FFN_FC, FFN_FC), FFN_FC), :]

    act_ref[0] = hidden(0)
    acc_ref[...] = jnp.zeros_like(acc_ref)

    def chunk_pair(pair, carry):
        c = 2 * pair
        act_ref[1] = hidden(c + 1)
        acc_ref[...] += jnp.dot(act_ref[0], down(c), preferred_element_type=F32)
        act_ref[0] = hidden(c + 2)
        acc_ref[...] += jnp.dot(act_ref[1], down(c + 1), preferred_element_type=F32)
        return carry

    assert FFN_NC % 2 == 1
    lax.fori_loop(0, FFN_NC // 2, chunk_pair, 0)
    resid = o_ref[0] if n_mix else h_ref[0]
    out = resid + 0.5 * (acc_ref[...] + jnp.dot(act_ref[0], down(FFN_NC - 1), preferred_element_type=F32))
    o_ref[0] = _rms(out, gf_ref[...]) if final_norm else out


def _ffn(h, g, wg, wu, wd, which, mix=(), final_g=None):
    b, s, _ = h.shape
    tm = FFN_TM
    row = lambda width: pl.BlockSpec((1, tm, width), lambda bi, i: (bi, i, 0))
    const = lambda shape: pl.BlockSpec(shape, lambda bi, i: (0,) * len(shape), pipeline_mode=pl.Buffered(1))
    ys = [y for y, _ in mix]
    wos = [w for _, w in mix]
    in_specs = [row(D_MODEL)] + [row(y.shape[2]) for y in ys] + [const(w.shape) for w in wos]
    pick = lambda rows, cols: pl.BlockSpec((None, None, rows, cols), lambda bi, i: (*which, 0, 0),
                                            pipeline_mode=pl.Buffered(1))
    in_specs += [const((1, D_MODEL)), pick(D_MODEL, D_FF), pick(D_MODEL, D_FF), pick(D_FF, D_MODEL)]
    args = [h, *ys, *wos, g, wg, wu, wd]
    if final_g is not None:
        in_specs.append(const((1, D_MODEL)))
        args.append(final_g)
    return pl.pallas_call(
        functools.partial(_ffn_kernel, n_mix=len(mix), final_norm=final_g is not None),
        name="ffn",
        grid=(b, s // tm),
        in_specs=in_specs,
        out_specs=row(D_MODEL),
        out_shape=jax.ShapeDtypeStruct(h.shape, F32),
        scratch_shapes=[
            pltpu.VMEM((tm, D_MODEL), BF16),
            pltpu.VMEM((tm, D_MODEL), F32),
            pltpu.VMEM((2, tm, FFN_FC), BF16),
        ],
        compiler_params=_params("parallel", "parallel"),
    )(*args)


def _ab_in_kernel(h_ref, g_ref, wu_ref, wq_ref, wk_ref, wv_ref, wf_ref, bf_ref, tri_ref,
                  pqt_ref, pk_ref, wpool_ref, pscale_ref,
                  ypool_ref, q_ref, k_ref, v_ref, halo_ref, fcarry_ref):
    i = pl.program_id(1)
    tm = h_ref.shape[1]

    @pl.when(i == 0)
    def _():
        halo_ref[...] = jnp.zeros_like(halo_ref)
        fcarry_ref[...] = jnp.zeros_like(fcarry_ref)

    hn = _rms(h_ref[0], g_ref[...]).astype(BF16)

    logit = jnp.dot(hn, wf_ref[...], preferred_element_type=F32) + bf_ref[...]
    log_f = jnp.minimum(logit, 0.0) - jnp.log1p(jnp.exp(-jnp.abs(logit)))
    tri = tri_ref[...]

    def split3(x):
        hi = x.astype(BF16)
        r1 = x - hi.astype(F32)
        mid = r1.astype(BF16)
        lo = (r1 - mid.astype(F32)).astype(BF16)
        return hi, mid, lo

    hi, mid, lo = split3(log_f)
    csum = (jnp.dot(tri, hi, preferred_element_type=F32)
            + jnp.dot(tri, mid, preferred_element_type=F32)
            + jnp.dot(tri, lo, preferred_element_type=F32))
    cum_f = csum + fcarry_ref[0:1, :]
    fcarry_ref[...] = jnp.broadcast_to(cum_f[tm - 1:tm, :], fcarry_ref.shape)

    fh, fm, fl = (x.astype(F32) for x in split3(cum_f * LOG2E))
    lane = lax.broadcasted_iota(jnp.int32, fh.shape, 1)
    xterms = jnp.where(lane < 8, fh, jnp.where(lane < 16, fm, jnp.where(
        lane < 24, fl, jnp.where(lane == 24, 1.0, 0.0)))).astype(BF16)

    k = jnp.dot(hn, wk_ref[...], preferred_element_type=F32)
    k = k + jnp.dot(xterms, pk_ref[...], preferred_element_type=F32)
    k_ref[0] = k.astype(BF16)

    def head_rows(xt, filler):
        parts = []
        for hd in range(FOX_HEADS):
            parts += [xt[hd * FOX_HEAD_DIM:(hd + 1) * FOX_HEAD_DIM], filler]
        return jnp.concatenate(parts, axis=0)

    pad_rows = LANES - FOX_HEAD_DIM
    nt = (((1,), (1,)), ((), ()))
    qt = lax.dot_general(wq_ref[...], hn, nt, preferred_element_type=F32) * (FOX_HEAD_DIM ** -0.5 * LOG2E)
    q_extra = lax.dot_general(pqt_ref[...], xterms, (((1,), (1,)), ((), ())),
                              preferred_element_type=F32)
    q_ref[0] = (head_rows(qt, jnp.zeros((pad_rows, tm), F32)) + q_extra).astype(BF16)
    vt = lax.dot_general(wv_ref[...], hn, nt, preferred_element_type=F32)
    ones_row = jnp.where(lax.broadcasted_iota(jnp.int32, (pad_rows, tm), 0) == 0, 1.0, 0.0)
    v_ref[0] = head_rows(vt, ones_row).astype(BF16)

    u = jnp.dot(hn, wu_ref[...], preferred_element_type=F32)
    ext = jnp.concatenate([halo_ref[...], u], axis=0)
    halo_ref[...] = u[tm - POOL_HALO:, :]
    t_pos = i * tm + lax.broadcasted_iota(jnp.int32, (tm, POOL_GROUP), 0)
    sums = ext
    outs = []
    for g, w in enumerate(POOL_WINDOWS):
        sums = sums + pltpu.roll(sums, w // 2, axis=0)
        win = sums[POOL_HALO:, :POOL_GROUP]
        count = jnp.minimum(t_pos + 1, w).astype(F32)
        diff = win / count - u[:, g * POOL_GROUP:(g + 1) * POOL_GROUP]
        outs.append(jnp.dot(diff.astype(BF16), wpool_ref[g], preferred_element_type=F32))
        if g + 1 < len(POOL_WINDOWS):
            sums = sums[:, POOL_GROUP:]
    y = jnp.concatenate(outs, axis=-1) * pscale_ref[...]
    ypool_ref[0] = y.astype(BF16)


def _ab_in(h, g, w):
    b, s, _ = h.shape
    tm = PROJ_TM
    hw = FOX_HEADS * LANES
    const2 = lambda shape: pl.BlockSpec(shape, lambda bi, i: (0,) * len(shape))
    tok = lambda width: pl.BlockSpec((1, tm, width), lambda bi, i: (bi, i, 0))
    tok_t = lambda width: pl.BlockSpec((1, width, tm), lambda bi, i: (bi, 0, i))
    return pl.pallas_call(
        _ab_in_kernel,
        name="ab_in",
        grid=(b, s // tm),
        in_specs=[
            tok(D_MODEL), const2((1, D_MODEL)),
            const2((D_MODEL, POOL_WIDTH)), const2((FOX_WIDTH, D_MODEL)), const2((D_MODEL, hw)),
            const2((FOX_WIDTH, D_MODEL)), const2((D_MODEL, LANES)), const2((1, LANES)),
            const2((tm, tm)), const2((hw, LANES)), const2((LANES, hw)),
            const2((len(POOL_WINDOWS), POOL_GROUP, POOL_GROUP)), const2((1, POOL_WIDTH)),
        ],
        out_specs=[tok(POOL_WIDTH), tok_t(hw), tok(hw), tok_t(hw)],
        out_shape=[
            jax.ShapeDtypeStruct((b, s, POOL_WIDTH), BF16),
            jax.ShapeDtypeStruct((b, hw, s), BF16),
            jax.ShapeDtypeStruct((b, s, hw), BF16),
            jax.ShapeDtypeStruct((b, hw, s), BF16),
        ],
        scratch_shapes=[
            pltpu.VMEM((POOL_HALO, POOL_WIDTH), F32),
            pltpu.VMEM((SUBLANES, LANES), F32),
        ],
        compiler_params=_params("arbitrary", "arbitrary"),
    )(h, g, w["wu"], w["wq"], w["wk"], w["wv"], w["wf"], w["bf"], w["tri"],
      w["pqt"], w["pk"], w["wpool"], w["pscale"])


def _head_groups(w, heads, width):
    rows = w.shape[0]
    w = w.reshape(rows, heads, width)
    w = jnp.pad(w, ((0, 0), (0, 0), (0, LANES - width)))
    return w.reshape(rows, heads * LANES)


def _prep_ab(w_in, b_forget, w_pool, pool_scale, w_out):
    o1, o2, o3, o4 = POOL_WIDTH, POOL_WIDTH + FOX_WIDTH, POOL_WIDTH + 2 * FOX_WIDTH, POOL_WIDTH + 3 * FOX_WIDTH
    wb = w_in.astype(BF16)
    hw = FOX_HEADS * LANES
    wf = jnp.pad(jnp.tile(wb[:, o4:], (1, 3)), ((0, 0), (0, LANES - 3 * FOX_HEADS)))
    bf = jnp.pad(jnp.tile(b_forget.astype(F32), 3), (0, LANES - 3 * FOX_HEADS))[None, :]
    r = jnp.arange(LANES)[:, None]
    c = jnp.arange(hw)[None, :]
    head, lane = c // LANES, c % LANES
    is_term = r < 3 * FOX_HEADS
    pq = jnp.where(is_term & (head == r % FOX_HEADS) & (lane == FOX_HEAD_DIM + r // FOX_HEADS), 1.0, 0.0)
    pq = pq + jnp.where((r == 3 * FOX_HEADS) & (lane >= FOX_HEAD_DIM + 3) & (lane < FOX_HEAD_DIM + 6), 1.0, 0.0)
    pk = jnp.where(is_term & (head == r % FOX_HEADS) & (lane == FOX_HEAD_DIM + 3 + r // FOX_HEADS), -1.0, 0.0)
    pk = pk + jnp.where((r == 3 * FOX_HEADS) & (lane >= FOX_HEAD_DIM) & (lane < FOX_HEAD_DIM + 3), 1.0, 0.0)
    tri = jnp.tril(jnp.ones((PROJ_TM, PROJ_TM), BF16))
    wo = w_out.astype(BF16)
    return {
        "wu": wb[:, :o1],
        "wq": wb[:, o1:o2].T,
        "wk": _head_groups(wb[:, o2:o3], FOX_HEADS, FOX_HEAD_DIM),
        "wv": wb[:, o3:o4].T,
        "wf": wf, "bf": bf, "tri": tri,
        "pqt": pq.astype(BF16).T, "pk": pk.astype(BF16),
        "wpool": w_pool.astype(BF16), "pscale": pool_scale.astype(F32)[None, :],
        "wo_pool": wo[:POOL_WIDTH], "wo_fox": wo[POOL_WIDTH:],
    }


def _mla_in_kernel(h_ref, g_ref, win_ref, qn_ref, kvn_ref, wqa_ref, wqb_ref, wk_ref, wv_ref,
                   cos_ref, sin_ref, q_ref, k_ref, v_ref):
    hn = _rms(h_ref[0], g_ref[...]).astype(BF16)
    proj = jnp.dot(hn, win_ref[...], preferred_element_type=F32)
    c_q = proj[:, :MLA_Q_LORA]
    c_kv = proj[:, MLA_Q_LORA:MLA_Q_LORA + MLA_KV_LORA]
    kr_a = proj[:, MLA_Q_LORA + MLA_KV_LORA:MLA_Q_LORA + MLA_KV_LORA + LANES]
    kr_b = proj[:, MLA_Q_LORA + MLA_KV_LORA + LANES:]
    cos = cos_ref[0]
    sin = sin_ref[0]
    k_rope = kr_a * cos + kr_b * sin

    qn = _rms(c_q, qn_ref[...]).astype(BF16)
    kvn = _rms(c_kv, kvn_ref[...]).astype(BF16)

    k_all = jnp.dot(kvn, wk_ref[...], preferred_element_type=F32)
    for hd in range(MLA_HEADS):
        grp = slice(hd * LANES, (hd + 1) * LANES)
        k_ref[0, :, grp] = (k_all[:, grp] + k_rope).astype(BF16)

    nt = (((1,), (1,)), ((), ()))
    scale = (MLA_NOPE + MLA_ROPE) ** -0.5 * LOG2E
    qk_dim = MLA_NOPE + MLA_ROPE
    q_a = lax.dot_general(wqa_ref[...], qn, nt, preferred_element_type=F32)
    q_b = lax.dot_general(wqb_ref[...], qn, nt, preferred_element_type=F32)
    v_t = lax.dot_general(wv_ref[...], kvn, nt, preferred_element_type=F32)
    cos_t = cos.T[MLA_NOPE:qk_dim] * scale
    sin_t = sin.T[MLA_NOPE:qk_dim] * scale
    tm = cos.shape[0]
    q_pad = jnp.zeros((LANES - qk_dim, tm), BF16)
    v_pad = jnp.where(lax.broadcasted_iota(jnp.int32, (LANES - MLA_V, tm), 0) == 0, 1.0, 0.0).astype(BF16)
    for hd in range(MLA_HEADS):
        row = hd * LANES
        qa = q_a[hd * qk_dim:(hd + 1) * qk_dim]
        q_ref[0, row:row + MLA_NOPE, :] = (qa[:MLA_NOPE] * scale).astype(BF16)
        q_ref[0, row + MLA_NOPE:row + qk_dim, :] = (
            qa[MLA_NOPE:] * cos_t + q_b[hd * MLA_ROPE:(hd + 1) * MLA_ROPE] * sin_t).astype(BF16)
        q_ref[0, row + qk_dim:row + LANES, :] = q_pad
        v_ref[0, row:row + MLA_V, :] = v_t[hd * MLA_V:(hd + 1) * MLA_V].astype(BF16)
        v_ref[0, row + MLA_V:row + LANES, :] = v_pad


def _mla_in(h, g, w, cos_l, sin_l):
    b, s, _ = h.shape
    tm = PROJ_TM
    hw = MLA_HEADS * LANES
    nin = MLA_Q_LORA + MLA_KV_LORA + 2 * LANES
    const2 = lambda shape: pl.BlockSpec(shape, lambda bi, i: (0,) * len(shape))
    tok = lambda width: pl.BlockSpec((1, tm, width), lambda bi, i: (bi, i, 0))
    tok_t = lambda width: pl.BlockSpec((1, width, tm), lambda bi, i: (bi, 0, i))
    return pl.pallas_call(
        _mla_in_kernel,
        name="mla_in",
        grid=(b, s // tm),
        in_specs=[
            tok(D_MODEL), const2((1, D_MODEL)), const2((D_MODEL, nin)),
            const2((1, MLA_Q_LORA)), const2((1, MLA_KV_LORA)),
            const2((MLA_HEADS * (MLA_NOPE + MLA_ROPE), MLA_Q_LORA)), const2((MLA_HEADS * MLA_ROPE, MLA_Q_LORA)),
            const2((MLA_KV_LORA, hw)), const2((MLA_HEADS * MLA_V, MLA_KV_LORA)),
            tok(LANES), tok(LANES),
        ],
        out_specs=[tok_t(hw), tok(hw), tok_t(hw)],
        out_shape=[jax.ShapeDtypeStruct((b, hw, s), BF16), jax.ShapeDtypeStruct((b, s, hw), BF16),
                   jax.ShapeDtypeStruct((b, hw, s), BF16)],
        compiler_params=_params("parallel", "parallel"),
    )(h, g, w["win"], w["qn"], w["kvn"], w["wqa"], w["wqb"], w["wk"], w["wv"], cos_l, sin_l)


def _rope_group(x1, x2, lead):
    z0 = jnp.zeros(lead + (MLA_NOPE,), x1.dtype)
    z1 = jnp.zeros(lead + (LANES - MLA_NOPE - MLA_ROPE,), x1.dtype)
    return jnp.concatenate([z0, x1, x2, z1], axis=-1)


def _prep_mla(w_in, q_norm, kv_norm, w_q_b, w_kv_b, w_out):
    half = MLA_ROPE // 2
    wb = w_in.astype(BF16)
    kr = wb[:, MLA_Q_LORA + MLA_KV_LORA:]
    a1, a2 = kr[:, :half], kr[:, half:]
    win = jnp.concatenate([
        wb[:, :MLA_Q_LORA + MLA_KV_LORA],
        _rope_group(a1, a2, (D_MODEL,)),
        _rope_group(-a2, a1, (D_MODEL,)),
    ], axis=-1)
    wq = w_q_b.astype(BF16).reshape(MLA_Q_LORA, MLA_HEADS, MLA_NOPE + MLA_ROPE)
    x1, x2 = wq[..., MLA_NOPE:MLA_NOPE + half], wq[..., MLA_NOPE + half:]
    wqa = w_q_b.astype(BF16).T
    wqb = jnp.concatenate([-x2, x1], axis=-1).reshape(MLA_Q_LORA, MLA_HEADS * MLA_ROPE).T
    wkv = w_kv_b.astype(BF16).reshape(MLA_KV_LORA, MLA_HEADS, MLA_NOPE + MLA_V)
    pad = jnp.zeros((MLA_KV_LORA, MLA_HEADS, LANES - MLA_NOPE), BF16)
    wk = jnp.concatenate([wkv[..., :MLA_NOPE], pad], axis=-1).reshape(MLA_KV_LORA, MLA_HEADS * LANES)
    wv = wkv[..., MLA_NOPE:].reshape(MLA_KV_LORA, MLA_HEADS * MLA_V).T
    return {
        "win": win, "qn": q_norm.astype(F32)[None, :], "kvn": kv_norm.astype(F32)[None, :],
        "wqa": wqa, "wqb": wqb, "wk": wk, "wv": wv, "wo": w_out.astype(BF16),
    }


def _attn_kernel(qt_ref, k_ref, vt_ref, o_ref, m_ref, acc_ref, s_ref, smax_ref, *, tq, tk, chunk, head_dim):
    i = pl.program_id(2)
    shift = chunk.bit_length() - 1
    tiles_per_block = tq // tk

    m_ref[...] = jnp.full(m_ref.shape, MASKED, F32)
    acc_ref[...] = jnp.zeros_like(acc_ref)
    groups = [slice(hh * LANES, (hh + 1) * LANES) for hh in range(HEADS_PER_STEP)]

    units = [(hh, slice(c, c + ATT_COLS)) for hh in range(HEADS_PER_STEP) for c in range(0, tq, ATT_COLS)]

    def scores_into(j, slot, hh, cols):
        off = pl.multiple_of(j * tk, tk)
        grp = groups[hh]
        s = jnp.dot(k_ref[0, pl.ds(off, tk), grp], qt_ref[0, grp, cols],
                    preferred_element_type=F32)
        s_ref[slot, hh, :, cols] = s
        smax_ref[slot, hh, :, cols] = jnp.broadcast_to(jnp.max(s, axis=0, keepdims=True),
                                                       (SUBLANES, s.shape[1]))

    def softmax_pv(j, slot, hh, cols, diag=None):
        off = pl.multiple_of(j * tk, tk)
        s = s_ref[slot, hh, :, cols]
        vt = vt_ref[0, hh * LANES:hh * LANES + V_ROWS, pl.ds(off, tk)]
        if diag is not None:
            key = lax.broadcasted_iota(jnp.int32, s.shape, 0) + diag * tk
            qry = lax.broadcasted_iota(jnp.int32, s.shape, 1) + cols.start
            s = jnp.where((key >> shift) <= (qry >> shift), s, MASKED)
        m_prev = m_ref[hh, 0:1, cols]
        s_max = smax_ref[slot, hh, 0:1, cols] if diag is None else jnp.max(s, axis=0, keepdims=True)
        m_new = jnp.maximum(m_prev, s_max)
        alpha = jnp.exp2(m_prev - m_new)
        p = jnp.exp2(s - m_new)
        acc_ref[hh, :, cols] = acc_ref[hh, :, cols] * alpha + jnp.dot(
            vt, p.astype(BF16), preferred_element_type=F32)
        m_ref[hh, :, cols] = jnp.broadcast_to(m_new, (SUBLANES, m_new.shape[1]))

    first_diag = i * tiles_per_block
    for hh, cols in units:
        scores_into(0, 0, hh, cols)

    def tile_pair(pair, carry):
        j = 2 * pair
        for hh, cols in units:
            scores_into(j + 1, 1, hh, cols)
            softmax_pv(j, 0, hh, cols)
        for hh, cols in units:
            scores_into(j + 2, 0, hh, cols)
            softmax_pv(j + 1, 1, hh, cols)
        return carry

    lax.fori_loop(0, first_diag // 2, tile_pair, 0)
    for d in range(tiles_per_block):
        for hh, cols in units:
            if d + 1 < tiles_per_block and cols.start >= (d + 1) * tk:
                scores_into(first_diag + d + 1, (d + 1) % 2, hh, cols)
            if cols.start >= d * tk:
                softmax_pv(first_diag + d, d % 2, hh, cols, diag=d)

    outs = []
    for hh in range(HEADS_PER_STEP):
        acc = acc_ref[hh]
        outs.append(acc[:head_dim, :] / acc[head_dim:head_dim + 1, :])
    o_ref[0] = jnp.concatenate(outs, axis=0).T.astype(BF16)


def _attention(qt, k, vt, heads, chunk, head_dim):
    b, s, _ = k.shape
    tq, tk = ATT_TQ, ATT_TK
    assert tq % (2 * tk) == 0 and tk % chunk == 0 and s % tq == 0 and heads % HEADS_PER_STEP == 0
    assert tk % ATT_COLS == 0 and tq % ATT_COLS == 0
    assert (HEADS_PER_STEP * head_dim) % LANES == 0
    gw = HEADS_PER_STEP * LANES
    return pl.pallas_call(
        functools.partial(_attn_kernel, tq=tq, tk=tk, chunk=chunk, head_dim=head_dim),
        name="attention",
        grid=(b, heads // HEADS_PER_STEP, s // tq),
        in_specs=[
            pl.BlockSpec((1, gw, tq), lambda bi, hp, i: (bi, hp, i)),
            pl.BlockSpec((1, s, gw), lambda bi, hp, i: (bi, 0, hp)),
            pl.BlockSpec((1, gw, s), lambda bi, hp, i: (bi, hp, 0)),
        ],
        out_specs=pl.BlockSpec((1, tq, HEADS_PER_STEP * head_dim), lambda bi, hp, i: (bi, i, hp)),
        out_shape=jax.ShapeDtypeStruct((b, s, heads * head_dim), BF16),
        scratch_shapes=[
            pltpu.VMEM((HEADS_PER_STEP, SUBLANES, tq), F32),
            pltpu.VMEM((HEADS_PER_STEP, V_ROWS, tq), F32),
            pltpu.VMEM((2, HEADS_PER_STEP, tk, tq), F32),
            pltpu.VMEM((2, HEADS_PER_STEP, SUBLANES, tq), F32),
        ],
        compiler_params=_params("parallel", "parallel", "arbitrary"),
    )(qt, k, vt)


def _rope_lane_tables(positions):
    inv_freq = ROPE_THETA ** (-jnp.arange(0, MLA_ROPE, 2, dtype=F32) / MLA_ROPE)
    ang = positions.astype(F32)[..., None] * inv_freq
    cos, sin = jnp.cos(ang), jnp.sin(ang)
    lead = positions.shape
    return _rope_group(cos, cos, lead), _rope_group(sin, sin, lead)


def kernel(x, positions, norm_ffn, norm_mix, norm_final, ffn_w_gate, ffn_w_up, ffn_w_down,
           ab_w_in, ab_b_forget, pool_w, pool_scale, ab_w_out,
           mla_w_in, mla_q_norm, mla_kv_norm, mla_w_q_b, mla_w_kv_b, mla_w_out):
    cos_l, sin_l = _rope_lane_tables(positions)
    w_gate, w_up, w_down = (w.astype(BF16) for w in (ffn_w_gate, ffn_w_up, ffn_w_down))
    h = x.astype(F32)
    for layer in range(DEPTH):
        idx = layer // 2
        h = _ffn(h, norm_ffn[layer, 0][None, :], w_gate, w_up, w_down, (layer, 0))
        g_mix = norm_mix[layer][None, :]
        if layer % 2 == 0:
            w = _prep_ab(ab_w_in[idx], ab_b_forget[idx], pool_w[idx], pool_scale[idx], ab_w_out[idx])
            y_pool, q, k, v = _ab_in(h, g_mix, w)
            y_fox = _attention(q, k, v, FOX_HEADS, 1, FOX_HEAD_DIM)
            mix = ((y_pool, w["wo_pool"]), (y_fox, w["wo_fox"]))
        else:
            w = _prep_mla(mla_w_in[idx], mla_q_norm[idx], mla_kv_norm[idx], mla_w_q_b[idx],
                          mla_w_kv_b[idx], mla_w_out[idx])
            q, k, v = _mla_in(h, g_mix, w, cos_l, sin_l)
            y = _attention(q, k, v, MLA_HEADS, CHUNK, MLA_V)
            mix = ((y, w["wo"]),)
        h = _ffn(h, norm_ffn[layer, 1][None, :], w_gate, w_up, w_down, (layer, 1), mix=mix,
                 final_g=norm_final[None, :] if layer == DEPTH - 1 else None)
    return h
```

```python
import functools
import math

import jax
import jax.numpy as jnp
from jax import lax
from jax.experimental import pallas as pl
from jax.experimental.pallas import tpu as pltpu

F32 = jnp.float32
BF16 = jnp.bfloat16

D_MODEL = 1024
DEPTH = 4
CHUNK = 64
RMS_EPS = 1e-6
D_FF = 2816
POOL_WINDOWS = (2, 4, 8, 16)
POOL_GROUP = 128
POOL_WIDTH = 512
FOX_HEADS = 8
FOX_HEAD_DIM = 64
FOX_WIDTH = 512
MLA_HEADS = 16
MLA_NOPE = 64
MLA_ROPE = 32
MLA_V = 64
MLA_Q_LORA = 256
MLA_KV_LORA = 128
ROPE_THETA = 10000.0

LANES = 128
SUBLANES = 8
VMEM_LIMIT_BYTES = 56 * 1024 * 1024

FFN_TM = 1024
FFN_FC = 256
FFN_NC = D_FF // FFN_FC
PROJ_TM = 512
ATT_TQ = 1024
ATT_TK = 512
HEADS_PER_STEP = 2
ATT_COLS = 256
V_ROWS = 80
POOL_HALO = 16

LOG2E = math.log2(math.e)
MASKED = -1e30


def _rms(x, g):
    return x * lax.rsqrt(jnp.mean(x * x, axis=-1, keepdims=True) + RMS_EPS) * g


def _params(*sem):
    return pltpu.CompilerParams(dimension_semantics=sem, vmem_limit_bytes=VMEM_LIMIT_BYTES)


def _ffn_kernel(*refs, n_mix, final_norm):
    h_ref = refs[0]
    y_refs = refs[1:1 + n_mix]
    wo_refs = refs[1 + n_mix:1 + 2 * n_mix]
    rest = refs[1 + 2 * n_mix:]
    g_ref, wg_ref, wu_ref, wd_ref = rest[:4]
    gf_ref = rest[4] if final_norm else None
    o_ref, xn_ref, acc_ref, act_ref = rest[-4:]

    h = h_ref[0]
    for y_ref, wo_ref in zip(y_refs, wo_refs):
        h = h + jnp.dot(y_ref[0], wo_ref[...], preferred_element_type=F32)
    xn_ref[...] = _rms(h, g_ref[...]).astype(BF16)
    if n_mix:
        o_ref[0] = h

    def hidden(c):
        cols = pl.ds(pl.multiple_of(c * FFN_FC, FFN_FC), FFN_FC)
        gate = jnp.dot(xn_ref[...], wg_ref[:, cols], preferred_element_type=F32)
        up = jnp.dot(xn_ref[...], wu_ref[:, cols], preferred_element_type=F32)
        return (gate * jax.nn.sigmoid(gate) * up).astype(BF16)

    def down(c):
        return wd_ref[pl.ds(pl.multiple_of(c * FFN_FC, FFN_FC), FFN_FC), :]

    act_ref[0] = hidden(0)
    acc_ref[...] = jnp.zeros_like(acc_ref)

    def chunk_pair(pair, carry):
        c = 2 * pair
        act_ref[1] = hidden(c + 1)
        acc_ref[...] += jnp.dot(act_ref[0], down(c), preferred_element_type=F32)
        act_ref[0] = hidden(c + 2)
        acc_ref[...] += jnp.dot(act_ref[1], down(c + 1), preferred_element_type=F32)
        return carry

    assert FFN_NC % 2 == 1
    lax.fori_loop(0, FFN_NC // 2, chunk_pair, 0)
    resid = o_ref[0] if n_mix else h_ref[0]
    out = resid + 0.5 * (acc_ref[...] + jnp.dot(act_ref[0], down(FFN_NC - 1), preferred_element_type=F32))
    o_ref[0] = _rms(out, gf_ref[...]) if final_norm else out


def _ffn(h, g, wg, wu, wd, which, mix=(), final_g=None):
    b, s, _ = h.shape
    tm = FFN_TM
    row = lambda width: pl.BlockSpec((1, tm, width), lambda bi, i: (bi, i, 0))
    const = lambda shape: pl.BlockSpec(shape, lambda bi, i: (0,) * len(shape), pipeline_mode=pl.Buffered(1))
    ys = [y for y, _ in mix]
    wos = [w for _, w in mix]
    in_specs = [row(D_MODEL)] + [row(y.shape[2]) for y in ys] + [const(w.shape) for w in wos]
    pick = lambda rows, cols: pl.BlockSpec((None, None, rows, cols), lambda bi, i: (*which, 0, 0),
                                            pipeline_mode=pl.Buffered(1))
    in_specs += [const((1, D_MODEL)), pick(D_MODEL, D_FF), pick(D_MODEL, D_FF), pick(D_FF, D_MODEL)]
    args = [h, *ys, *wos, g, wg, wu, wd]
    if final_g is not None:
        in_specs.append(const((1, D_MODEL)))
        args.append(final_g)
    return pl.pallas_call(
        functools.partial(_ffn_kernel, n_mix=len(mix), final_norm=final_g is not None),
        name="ffn",
        grid=(b, s // tm),
        in_specs=in_specs,
        out_specs=row(D_MODEL),
        out_shape=jax.ShapeDtypeStruct(h.shape, F32),
        scratch_shapes=[
            pltpu.VMEM((tm, D_MODEL), BF16),
            pltpu.VMEM((tm, D_MODEL), F32),
            pltpu.VMEM((2, tm, FFN_FC), BF16),
        ],
        compiler_params=_params("parallel", "parallel"),
    )(*args)


def _ab_in_kernel(h_ref, g_ref, wu_ref, wq_ref, wk_ref, wv_ref, wf_ref, bf_ref, tri_ref,
                  pqt_ref, pk_ref, wpool_ref, pscale_ref,
                  ypool_ref, q_ref, k_ref, v_ref, halo_ref, fcarry_ref):
    i = pl.program_id(1)
    tm = h_ref.shape[1]

    @pl.when(i == 0)
    def _():
        halo_ref[...] = jnp.zeros_like(halo_ref)
        fcarry_ref[...] = jnp.zeros_like(fcarry_ref)

    hn = _rms(h_ref[0], g_ref[...]).astype(BF16)

    logit = jnp.dot(hn, wf_ref[...], preferred_element_type=F32) + bf_ref[...]
    log_f = jnp.minimum(logit, 0.0) - jnp.log1p(jnp.exp(-jnp.abs(logit)))
    tri = tri_ref[...]

    def split3(x):
        hi = x.astype(BF16)
        r1 = x - hi.astype(F32)
        mid = r1.astype(BF16)
        lo = (r1 - mid.astype(F32)).astype(BF16)
        return hi, mid, lo

    hi, mid, lo = split3(log_f)
    csum = (jnp.dot(tri, hi, preferred_element_type=F32)
            + jnp.dot(tri, mid, preferred_element_type=F32)
            + jnp.dot(tri, lo, preferred_element_type=F32))
    cum_f = csum + fcarry_ref[0:1, :]
    fcarry_ref[...] = jnp.broadcast_to(cum_f[tm - 1:tm, :], fcarry_ref.shape)

    fh, fm, fl = (x.astype(F32) for x in split3(cum_f * LOG2E))
    lane = lax.broadcasted_iota(jnp.int32, fh.shape, 1)
    xterms = jnp.where(lane < 8, fh, jnp.where(lane < 16, fm, jnp.where(
        lane < 24, fl, jnp.where(lane == 24, 1.0, 0.0)))).astype(BF16)

    k = jnp.dot(hn, wk_ref[...], preferred_element_type=F32)
    k = k + jnp.dot(xterms, pk_ref[...], preferred_element_type=F32)
    k_ref[0] = k.astype(BF16)

    def head_rows(xt, filler):
        parts = []
        for hd in range(FOX_HEADS):
            parts += [xt[hd * FOX_HEAD_DIM:(hd + 1) * FOX_HEAD_DIM], filler]
        return jnp.concatenate(parts, axis=0)

    pad_rows = LANES - FOX_HEAD_DIM
    nt = (((1,), (1,)), ((), ()))
    qt = lax.dot_general(wq_ref[...], hn, nt, preferred_element_type=F32) * (FOX_HEAD_DIM ** -0.5 * LOG2E)
    q_extra = lax.dot_general(pqt_ref[...], xterms, (((1,), (1,)), ((), ())),
                              preferred_element_type=F32)
    q_ref[0] = (head_rows(qt, jnp.zeros((pad_rows, tm), F32)) + q_extra).astype(BF16)
    vt = lax.dot_general(wv_ref[...], hn, nt, preferred_element_type=F32)
    ones_row = jnp.where(lax.broadcasted_iota(jnp.int32, (pad_rows, tm), 0) == 0, 1.0, 0.0)
    v_ref[0] = head_rows(vt, ones_row).astype(BF16)

    u = jnp.dot(hn, wu_ref[...], preferred_element_type=F32)
    ext = jnp.concatenate([halo_ref[...], u], axis=0)
    halo_ref[...] = u[tm - POOL_HALO:, :]
    t_pos = i * tm + lax.broadcasted_iota(jnp.int32, (tm, POOL_GROUP), 0)
    sums = ext
    outs = []
    for g, w in enumerate(POOL_WINDOWS):
        sums = sums + pltpu.roll(sums, w // 2, axis=0)
        win = sums[POOL_HALO:, :POOL_GROUP]
        count = jnp.minimum(t_pos + 1, w).astype(F32)
        diff = win / count - u[:, g * POOL_GROUP:(g + 1) * POOL_GROUP]
        outs.append(jnp.dot(diff.astype(BF16), wpool_ref[g], preferred_element_type=F32))
        if g + 1 < len(POOL_WINDOWS):
            sums = sums[:, POOL_GROUP:]
    y = jnp.concatenate(outs, axis=-1) * pscale_ref[...]
    ypool_ref[0] = y.astype(BF16)


def _ab_in(h, g, w):
    b, s, _ = h.shape
    tm = PROJ_TM
    hw = FOX_HEADS * LANES
    const2 = lambda shape: pl.BlockSpec(shape, lambda bi, i: (0,) * len(shape))
    tok = lambda width: pl.BlockSpec((1, tm, width), lambda bi, i: (bi, i, 0))
    tok_t = lambda width: pl.BlockSpec((1, width, tm), lambda bi, i: (bi, 0, i))
    return pl.pallas_call(
        _ab_in_kernel,
        name="ab_in",
        grid=(b, s // tm),
        in_specs=[
            tok(D_MODEL), const2((1, D_MODEL)),
            const2((D_MODEL, POOL_WIDTH)), const2((FOX_WIDTH, D_MODEL)), const2((D_MODEL, hw)),
            const2((FOX_WIDTH, D_MODEL)), const2((D_MODEL, LANES)), const2((1, LANES)),
            const2((tm, tm)), const2((hw, LANES)), const2((LANES, hw)),
            const2((len(POOL_WINDOWS), POOL_GROUP, POOL_GROUP)), const2((1, POOL_WIDTH)),
        ],
        out_specs=[tok(POOL_WIDTH), tok_t(hw), tok(hw), tok_t(hw)],
        out_shape=[
            jax.ShapeDtypeStruct((b, s, POOL_WIDTH), BF16),
            jax.ShapeDtypeStruct((b, hw, s), BF16),
            jax.ShapeDtypeStruct((b, s, hw), BF16),
            jax.ShapeDtypeStruct((b, hw, s), BF16),
        ],
        scratch_shapes=[
            pltpu.VMEM((POOL_HALO, POOL_WIDTH), F32),
            pltpu.VMEM((SUBLANES, LANES), F32),
        ],
        compiler_params=_params("arbitrary", "arbitrary"),
    )(h, g, w["wu"], w["wq"], w["wk"], w["wv"], w["wf"], w["bf"], w["tri"],
      w["pqt"], w["pk"], w["wpool"], w["pscale"])


def _head_groups(w, heads, width):
    rows = w.shape[0]
    w = w.reshape(rows, heads, width)
    w = jnp.pad(w, ((0, 0), (0, 0), (0, LANES - width)))
    return w.reshape(rows, heads * LANES)


def _prep_ab(w_in, b_forget, w_pool, pool_scale, w_out):
    o1, o2, o3, o4 = POOL_WIDTH, POOL_WIDTH + FOX_WIDTH, POOL_WIDTH + 2 * FOX_WIDTH, POOL_WIDTH + 3 * FOX_WIDTH
    wb = w_in.astype(BF16)
    hw = FOX_HEADS * LANES
    wf = jnp.pad(jnp.tile(wb[:, o4:], (1, 3)), ((0, 0), (0, LANES - 3 * FOX_HEADS)))
    bf = jnp.pad(jnp.tile(b_forget.astype(F32), 3), (0, LANES - 3 * FOX_HEADS))[None, :]
    r = jnp.arange(LANES)[:, None]
    c = jnp.arange(hw)[None, :]
    head, lane = c // LANES, c % LANES
    is_term = r < 3 * FOX_HEADS
    pq = jnp.where(is_term & (head == r % FOX_HEADS) & (lane == FOX_HEAD_DIM + r // FOX_HEADS), 1.0, 0.0)
    pq = pq + jnp.where((r == 3 * FOX_HEADS) & (lane >= FOX_HEAD_DIM + 3) & (lane < FOX_HEAD_DIM + 6), 1.0, 0.0)
    pk = jnp.where(is_term & (head == r % FOX_HEADS) & (lane == FOX_HEAD_DIM + 3 + r // FOX_HEADS), -1.0, 0.0)
    pk = pk + jnp.where((r == 3 * FOX_HEADS) & (lane >= FOX_HEAD_DIM) & (lane < FOX_HEAD_DIM + 3), 1.0, 0.0)
    tri = jnp.tril(jnp.ones((PROJ_TM, PROJ_TM), BF16))
    wo = w_out.astype(BF16)
    return {
        "wu": wb[:, :o1],
        "wq": wb[:, o1:o2].T,
        "wk": _head_groups(wb[:, o2:o3], FOX_HEADS, FOX_HEAD_DIM),
        "wv": wb[:, o3:o4].T,
        "wf": wf, "bf": bf, "tri": tri,
        "pqt": pq.astype(BF16).T, "pk": pk.astype(BF16),
        "wpool": w_pool.astype(BF16), "pscale": pool_scale.astype(F32)[None, :],
        "wo_pool": wo[:POOL_WIDTH], "wo_fox": wo[POOL_WIDTH:],
    }


def _mla_in_kernel(h_ref, g_ref, win_ref, qn_ref, kvn_ref, wqa_ref, wqb_ref, wk_ref, wv_ref,
                   cos_ref, sin_ref, q_ref, k_ref, v_ref):
    hn = _rms(h_ref[0], g_ref[...]).astype(BF16)
    proj = jnp.dot(hn, win_ref[...], preferred_element_type=F32)
    c_q = proj[:, :MLA_Q_LORA]
    c_kv = proj[:, MLA_Q_LORA:MLA_Q_LORA + MLA_KV_LORA]
    kr_a = proj[:, MLA_Q_LORA + MLA_KV_LORA:MLA_Q_LORA + MLA_KV_LORA + LANES]
    kr_b = proj[:, MLA_Q_LORA + MLA_KV_LORA + LANES:]
    cos = cos_ref[0]
    sin = sin_ref[0]
    k_rope = kr_a * cos + kr_b * sin

    qn = _rms(c_q, qn_ref[...]).astype(BF16)
    kvn = _rms(c_kv, kvn_ref[...]).astype(BF16)

    k_all = jnp.dot(kvn, wk_ref[...], preferred_element_type=F32)
    for hd in range(MLA_HEADS):
        grp = slice(hd * LANES, (hd + 1) * LANES)
        k_ref[0, :, grp] = (k_all[:, grp] + k_rope).astype(BF16)

    nt = (((1,), (1,)), ((), ()))
    scale = (MLA_NOPE + MLA_ROPE) ** -0.5 * LOG2E
    qk_dim = MLA_NOPE + MLA_ROPE
    q_a = lax.dot_general(wqa_ref[...], qn, nt, preferred_element_type=F32)
    q_b = lax.dot_general(wqb_ref[...], qn, nt, preferred_element_type=F32)
    v_t = lax.dot_general(wv_ref[...], kvn, nt, preferred_element_type=F32)
    cos_t = cos.T[MLA_NOPE:qk_dim] * scale
    sin_t = sin.T[MLA_NOPE:qk_dim] * scale
    tm = cos.shape[0]
    q_pad = jnp.zeros((LANES - qk_dim, tm), BF16)
    v_pad = jnp.where(lax.broadcasted_iota(jnp.int32, (LANES - MLA_V, tm), 0) == 0, 1.0, 0.0).astype(BF16)
    for hd in range(MLA_HEADS):
        row = hd * LANES
        qa = q_a[hd * qk_dim:(hd + 1) * qk_dim]
        q_ref[0, row:row + MLA_NOPE, :] = (qa[:MLA_NOPE] * scale).astype(BF16)
        q_ref[0, row + MLA_NOPE:row + qk_dim, :] = (
            qa[MLA_NOPE:] * cos_t + q_b[hd * MLA_ROPE:(hd + 1) * MLA_ROPE] * sin_t).astype(BF16)
        q_ref[0, row + qk_dim:row + LANES, :] = q_pad
        v_ref[0, row:row + MLA_V, :] = v_t[hd * MLA_V:(hd + 1) * MLA_V].astype(BF16)
        v_ref[0, row + MLA_V:row + LANES, :] = v_pad


def _mla_in(h, g, w, cos_l, sin_l):
    b, s, _ = h.shape
    tm = PROJ_TM
    hw = MLA_HEADS * LANES
    nin = MLA_Q_LORA + MLA_KV_LORA + 2 * LANES
    const2 = lambda shape: pl.BlockSpec(shape, lambda bi, i: (0,) * len(shape))
    tok = lambda width: pl.BlockSpec((1, tm, width), lambda bi, i: (bi, i, 0))
    tok_t = lambda width: pl.BlockSpec((1, width, tm), lambda bi, i: (bi, 0, i))
    return pl.pallas_call(
        _mla_in_kernel,
        name="mla_in",
        grid=(b, s // tm),
        in_specs=[
            tok(D_MODEL), const2((1, D_MODEL)), const2((D_MODEL, nin)),
            const2((1, MLA_Q_LORA)), const2((1, MLA_KV_LORA)),
            const2((MLA_HEADS * (MLA_NOPE + MLA_ROPE), MLA_Q_LORA)), const2((MLA_HEADS * MLA_ROPE, MLA_Q_LORA)),
            const2((MLA_KV_LORA, hw)), const2((MLA_HEADS * MLA_V, MLA_KV_LORA)),
            tok(LANES), tok(LANES),
        ],
        out_specs=[tok_t(hw), tok(hw), tok_t(hw)],
        out_shape=[jax.ShapeDtypeStruct((b, hw, s), BF16), jax.ShapeDtypeStruct((b, s, hw), BF16),
                   jax.ShapeDtypeStruct((b, hw, s), BF16)],
        compiler_params=_params("parallel", "parallel"),
    )(h, g, w["win"], w["qn"], w["kvn"], w["wqa"], w["wqb"], w["wk"], w["wv"], cos_l, sin_l)


def _rope_group(x1, x2, lead):
    z0 = jnp.zeros(lead + (MLA_NOPE,), x1.dtype)
    z1 = jnp.zeros(lead + (LANES - MLA_NOPE - MLA_ROPE,), x1.dtype)
    return jnp.concatenate([z0, x1, x2, z1], axis=-1)


def _prep_mla(w_in, q_norm, kv_norm, w_q_b, w_kv_b, w_out):
    half = MLA_ROPE // 2
    wb = w_in.astype(BF16)
    kr = wb[:, MLA_Q_LORA + MLA_KV_LORA:]
    a1, a2 = kr[:, :half], kr[:, half:]
    win = jnp.concatenate([
        wb[:, :MLA_Q_LORA + MLA_KV_LORA],
        _rope_group(a1, a2, (D_MODEL,)),
        _rope_group(-a2, a1, (D_MODEL,)),
    ], axis=-1)
    wq = w_q_b.astype(BF16).reshape(MLA_Q_LORA, MLA_HEADS, MLA_NOPE + MLA_ROPE)
    x1, x2 = wq[..., MLA_NOPE:MLA_NOPE + half], wq[..., MLA_NOPE + half:]
    wqa = w_q_b.astype(BF16).T
    wqb = jnp.concatenate([-x2, x1], axis=-1).reshape(MLA_Q_LORA, MLA_HEADS * MLA_ROPE).T
    wkv = w_kv_b.astype(BF16).reshape(MLA_KV_LORA, MLA_HEADS, MLA_NOPE + MLA_V)
    pad = jnp.zeros((MLA_KV_LORA, MLA_HEADS, LANES - MLA_NOPE), BF16)
    wk = jnp.concatenate([wkv[..., :MLA_NOPE], pad], axis=-1).reshape(MLA_KV_LORA, MLA_HEADS * LANES)
    wv = wkv[..., MLA_NOPE:].reshape(MLA_KV_LORA, MLA_HEADS * MLA_V).T
    return {
        "win": win, "qn": q_norm.astype(F32)[None, :], "kvn": kv_norm.astype(F32)[None, :],
        "wqa": wqa, "wqb": wqb, "wk": wk, "wv": wv, "wo": w_out.astype(BF16),
    }


def _attn_kernel(qt_ref, k_ref, vt_ref, o_ref, m_ref, acc_ref, s_ref, smax_ref, *, tq, tk, chunk, head_dim):
    i = pl.program_id(2)
    shift = chunk.bit_length() - 1
    tiles_per_block = tq // tk

    m_ref[...] = jnp.full(m_ref.shape, MASKED, F32)
    acc_ref[...] = jnp.zeros_like(acc_ref)
    groups = [slice(hh * LANES, (hh + 1) * LANES) for hh in range(HEADS_PER_STEP)]

    units = [(hh, slice(c, c + ATT_COLS)) for hh in range(HEADS_PER_STEP) for c in range(0, tq, ATT_COLS)]

    def scores_into(j, slot, hh, cols):
        off = pl.multiple_of(j * tk, tk)
        grp = groups[hh]
        s = jnp.dot(k_ref[0, pl.ds(off, tk), grp], qt_ref[0, grp, cols],
                    preferred_element_type=F32)
        s_ref[slot, hh, :, cols] = s
        smax_ref[slot, hh, :, cols] = jnp.broadcast_to(jnp.max(s, axis=0, keepdims=True),
                                                       (SUBLANES, s.shape[1]))

    def softmax_pv(j, slot, hh, cols, diag=None):
        off = pl.multiple_of(j * tk, tk)
        s = s_ref[slot, hh, :, cols]
        vt = vt_ref[0, hh * LANES:hh * LANES + V_ROWS, pl.ds(off, tk)]
        if diag is not None:
            key = lax.broadcasted_iota(jnp.int32, s.shape, 0) + diag * tk
            qry = lax.broadcasted_iota(jnp.int32, s.shape, 1) + cols.start
            s = jnp.where((key >> shift) <= (qry >> shift), s, MASKED)
        m_prev = m_ref[hh, 0:1, cols]
        s_max = smax_ref[slot, hh, 0:1, cols] if diag is None else jnp.max(s, axis=0, keepdims=True)
        m_new = jnp.maximum(m_prev, s_max)
        alpha = jnp.exp2(m_prev - m_new)
        p = jnp.exp2(s - m_new)
        acc_ref[hh, :, cols] = acc_ref[hh, :, cols] * alpha + jnp.dot(
            vt, p.astype(BF16), preferred_element_type=F32)
        m_ref[hh, :, cols] = jnp.broadcast_to(m_new, (SUBLANES, m_new.shape[1]))

    first_diag = i * tiles_per_block
    for hh, cols in units:
        scores_into(0, 0, hh, cols)

    def tile_pair(pair, carry):
        j = 2 * pair
        for hh, cols in units:
            scores_into(j + 1, 1, hh, cols)
            softmax_pv(j, 0, hh, cols)
        for hh, cols in units:
            scores_into(j + 2, 0, hh, cols)
            softmax_pv(j + 1, 1, hh, cols)
        return carry

    lax.fori_loop(0, first_diag // 2, tile_pair, 0)
    for d in range(tiles_per_block):
        for hh, cols in units:
            if d + 1 < tiles_per_block and cols.start >= (d + 1) * tk:
                scores_into(first_diag + d + 1, (d + 1) % 2, hh, cols)
            if cols.start >= (d + 1) * tk:
                softmax_pv(first_diag + d, d % 2, hh, cols)
            elif cols.start >= d * tk:
                softmax_pv(first_diag + d, d % 2, hh, cols, diag=d)

    outs = []
    for hh in range(HEADS_PER_STEP):
        acc = acc_ref[hh]
        outs.append(acc[:head_dim, :] / acc[head_dim:head_dim + 1, :])
    o_ref[0] = jnp.concatenate(outs, axis=0).T.astype(BF16)


def _attention(qt, k, vt, heads, chunk, head_dim):
    b, s, _ = k.shape
    tq, tk = ATT_TQ, ATT_TK
    assert tq % (2 * tk) == 0 and tk % chunk == 0 and s % tq == 0 and heads % HEADS_PER_STEP == 0
    assert tk % ATT_COLS == 0 and tq % ATT_COLS == 0
    assert (HEADS_PER_STEP * head_dim) % LANES == 0
    gw = HEADS_PER_STEP * LANES
    return pl.pallas_call(
        functools.partial(_attn_kernel, tq=tq, tk=tk, chunk=chunk, head_dim=head_dim),
        name="attention",
        grid=(b, heads // HEADS_PER_STEP, s // tq),
        in_specs=[
            pl.BlockSpec((1, gw, tq), lambda bi, hp, i: (bi, hp, i)),
            pl.BlockSpec((1, s, gw), lambda bi, hp, i: (bi, 0, hp)),
            pl.BlockSpec((1, gw, s), lambda bi, hp, i: (bi, hp, 0)),
        ],
        out_specs=pl.BlockSpec((1, tq, HEADS_PER_STEP * head_dim), lambda bi, hp, i: (bi, i, hp)),
        out_shape=jax.ShapeDtypeStruct((b, s, heads * head_dim), BF16),
        scratch_shapes=[
            pltpu.VMEM((HEADS_PER_STEP, SUBLANES, tq), F32),
            pltpu.VMEM((HEADS_PER_STEP, V_ROWS, tq), F32),
            pltpu.VMEM((2, HEADS_PER_STEP, tk, tq), F32),
            pltpu.VMEM((2, HEADS_PER_STEP, SUBLANES, tq), F32),
        ],
        compiler_params=_params("parallel", "parallel", "arbitrary"),
    )(qt, k, vt)


def _rope_lane_tables(positions):
    inv_freq = ROPE_THETA ** (-jnp.arange(0, MLA_ROPE, 2, dtype=F32) / MLA_ROPE)
    ang = positions.astype(F32)[..., None] * inv_freq
    cos, sin = jnp.cos(ang), jnp.sin(ang)
    lead = positions.shape
    return _rope_group(cos, cos, lead), _rope_group(sin, sin, lead)


def kernel(x, positions, norm_ffn, norm_mix, norm_final, ffn_w_gate, ffn_w_up, ffn_w_down,
           ab_w_in, ab_b_forget, pool_w, pool_scale, ab_w_out,
           mla_w_in, mla_q_norm, mla_kv_norm, mla_w_q_b, mla_w_kv_b, mla_w_out):
    cos_l, sin_l = _rope_lane_tables(positions)
    w_gate, w_up, w_down = (w.astype(BF16) for w in (ffn_w_gate, ffn_w_up, ffn_w_down))
    h = x.astype(F32)
    for layer in range(DEPTH):
        idx = layer // 2
        h = _ffn(h, norm_ffn[layer, 0][None, :], w_gate, w_up, w_down, (layer, 0))
        g_mix = norm_mix[layer][None, :]
        if layer % 2 == 0:
            w = _prep_ab(ab_w_in[idx], ab_b_forget[idx], pool_w[idx], pool_scale[idx], ab_w_out[idx])
            y_pool, q, k, v = _ab_in(h, g_mix, w)
            y_fox = _attention(q, k, v, FOX_HEADS, 1, FOX_HEAD_DIM)
            mix = ((y_pool, w["wo_pool"]), (y_fox, w["wo_fox"]))
        else:
            w = _prep_mla(mla_w_in[idx], mla_q_norm[idx], mla_kv_norm[idx], mla_w_q_b[idx],
                          mla_w_kv_b[idx], mla_w_out[idx])
            q, k, v = _mla_in(h, g_mix, w, cos_l, sin_l)
            y = _attention(q, k, v, MLA_HEADS, CHUNK, MLA_V)
            mix = ((y, w["wo"]),)
        h = _ffn(h, norm_ffn[layer, 1][None, :], w_gate, w_up, w_down, (layer, 1), mix=mix,
                 final_g=norm_final[None, :] if layer == DEPTH - 1 else None)
    return h
```

```python
import functools
import math

import jax
import jax.numpy as jnp
from jax import lax
from jax.experimental import pallas as pl
from jax.experimental.pallas import tpu as pltpu

F32 = jnp.float32
BF16 = jnp.bfloat16

D_MODEL = 1024
DEPTH = 4
CHUNK = 64
RMS_EPS = 1e-6
D_FF = 2816
POOL_WINDOWS = (2, 4, 8, 16)
POOL_GROUP = 128
POOL_WIDTH = 512
FOX_HEADS = 8
FOX_HEAD_DIM = 64
FOX_WIDTH = 512
MLA_HEADS = 16
MLA_NOPE = 64
MLA_ROPE = 32
MLA_V = 64
MLA_Q_LORA = 256
MLA_KV_LORA = 128
ROPE_THETA = 10000.0

LANES = 128
SUBLANES = 8
VMEM_LIMIT_BYTES = 56 * 1024 * 1024

FFN_TM = 1024
FFN_FC = 256
FFN_NC = D_FF // FFN_FC
PROJ_TM = 512
ATT_TQ = 1024
ATT_TK = 512
HEADS_PER_STEP = 2
ATT_COLS = 256
V_ROWS = 80
POOL_HALO = 16

LOG2E = math.log2(math.e)
MASKED = -1e30


def _rms(x, g):
    return x * lax.rsqrt(jnp.mean(x * x, axis=-1, keepdims=True) + RMS_EPS) * g


def _params(*sem):
    return pltpu.CompilerParams(dimension_semantics=sem, vmem_limit_bytes=VMEM_LIMIT_BYTES)


def _ffn_kernel(*refs, n_mix, final_norm):
    h_ref = refs[0]
    y_refs = refs[1:1 + n_mix]
    wo_refs = refs[1 + n_mix:1 + 2 * n_mix]
    rest = refs[1 + 2 * n_mix:]
    g_ref, wg_ref, wu_ref, wd_ref = rest[:4]
    gf_ref = rest[4] if final_norm else None
    o_ref, xn_ref, acc_ref, act_ref = rest[-4:]

    h = h_ref[0]
    for y_ref, wo_ref in zip(y_refs, wo_refs):
        h = h + jnp.dot(y_ref[0], wo_ref[...], preferred_element_type=F32)
    xn_ref[...] = _rms(h, g_ref[...]).astype(BF16)
    if n_mix:
        o_ref[0] = h

    def hidden(c):
        cols = pl.ds(pl.multiple_of(c * FFN_FC, FFN_FC), FFN_FC)
        gate = jnp.dot(xn_ref[...], wg_ref[:, cols], preferred_element_type=F32)
        up = jnp.dot(xn_ref[...], wu_ref[:, cols], preferred_element_type=F32)
        return (gate * jax.nn.sigmoid(gate) * up).astype(BF16)

    def down(c):
        return wd_ref[pl.ds(pl.multiple_of(c * FFN_FC, FFN_FC), FFN_FC), :]

    act_ref[0] = hidden(0)
    acc_ref[...] = jnp.zeros_like(acc_ref)

    def chunk_pair(pair, carry):
        c = 2 * pair
        act_ref[1] = hidden(c + 1)
        acc_ref[...] += jnp.dot(act_ref[0], down(c), preferred_element_type=F32)
        act_ref[0] = hidden(c + 2)
        acc_ref[...] += jnp.dot(act_ref[1], down(c + 1), preferred_element_type=F32)
        return carry

    assert FFN_NC % 2 == 1
    lax.fori_loop(0, FFN_NC // 2, chunk_pair, 0)
    resid = o_ref[0] if n_mix else h_ref[0]
    out = resid + 0.5 * (acc_ref[...] + jnp.dot(act_ref[0], down(FFN_NC - 1), preferred_element_type=F32))
    o_ref[0] = _rms(out, gf_ref[...]) if final_norm else out


def _ffn(h, g, wg, wu, wd, which, mix=(), final_g=None):
    b, s, _ = h.shape
    tm = FFN_TM
    row = lambda width: pl.BlockSpec((1, tm, width), lambda bi, i: (bi, i, 0))
    const = lambda shape: pl.BlockSpec(shape, lambda bi, i: (0,) * len(shape), pipeline_mode=pl.Buffered(1))
    ys = [y for y, _ in mix]
    wos = [w for _, w in mix]
    in_specs = [row(D_MODEL)] + [row(y.shape[2]) for y in ys] + [const(w.shape) for w in wos]
    pick = lambda rows, cols: pl.BlockSpec((None, None, rows, cols), lambda bi, i: (*which, 0, 0),
                                            pipeline_mode=pl.Buffered(1))
    in_specs += [const((1, D_MODEL)), pick(D_MODEL, D_FF), pick(D_MODEL, D_FF), pick(D_FF, D_MODEL)]
    args = [h, *ys, *wos, g, wg, wu, wd]
    if final_g is not None:
        in_specs.append(const((1, D_MODEL)))
        args.append(final_g)
    return pl.pallas_call(
        functools.partial(_ffn_kernel, n_mix=len(mix), final_norm=final_g is not None),
        name="ffn",
        grid=(b, s // tm),
        in_specs=in_specs,
        out_specs=row(D_MODEL),
        out_shape=jax.ShapeDtypeStruct(h.shape, F32),
        scratch_shapes=[
            pltpu.VMEM((tm, D_MODEL), BF16),
            pltpu.VMEM((tm, D_MODEL), F32),
            pltpu.VMEM((2, tm, FFN_FC), BF16),
        ],
        compiler_params=_params("parallel", "parallel"),
    )(*args)


def _ab_in_kernel(h_ref, g_ref, wu_ref, wq_ref, wk_ref, wv_ref, wf_ref, bf_ref, tri_ref,
                  pqt_ref, pk_ref, wpool_ref, pscale_ref,
                  ypool_ref, q_ref, k_ref, v_ref, halo_ref, fcarry_ref):
    i = pl.program_id(1)
    tm = h_ref.shape[1]

    @pl.when(i == 0)
    def _():
        halo_ref[...] = jnp.zeros_like(halo_ref)
        fcarry_ref[...] = jnp.zeros_like(fcarry_ref)

    hn = _rms(h_ref[0], g_ref[...]).astype(BF16)

    logit = jnp.dot(hn, wf_ref[...], preferred_element_type=F32) + bf_ref[...]
    log_f = jnp.minimum(logit, 0.0) - jnp.log1p(jnp.exp(-jnp.abs(logit)))
    tri = tri_ref[...]

    def split3(x):
        hi = x.astype(BF16)
        r1 = x - hi.astype(F32)
        mid = r1.astype(BF16)
        lo = (r1 - mid.astype(F32)).astype(BF16)
        return hi, mid, lo

    hi, mid, lo = split3(log_f)
    csum = (jnp.dot(tri, hi, preferred_element_type=F32)
            + jnp.dot(tri, mid, preferred_element_type=F32)
            + jnp.dot(tri, lo, preferred_element_type=F32))
    cum_f = csum + fcarry_ref[0:1, :]
    fcarry_ref[...] = jnp.broadcast_to(cum_f[tm - 1:tm, :], fcarry_ref.shape)

    fh, fm, fl = (x.astype(F32) for x in split3(cum_f * LOG2E))
    lane = lax.broadcasted_iota(jnp.int32, fh.shape, 1)
    xterms = jnp.where(lane < 8, fh, jnp.where(lane < 16, fm, jnp.where(
        lane < 24, fl, jnp.where(lane == 24, 1.0, 0.0)))).astype(BF16)

    k = jnp.dot(hn, wk_ref[...], preferred_element_type=F32)
    k = k + jnp.dot(xterms, pk_ref[...], preferred_element_type=F32)
    k_ref[0] = k.astype(BF16)

    def head_rows(xt, filler):
        parts = []
        for hd in range(FOX_HEADS):
            parts += [xt[hd * FOX_HEAD_DIM:(hd + 1) * FOX_HEAD_DIM], filler]
        return jnp.concatenate(parts, axis=0)

    pad_rows = LANES - FOX_HEAD_DIM
    nt = (((1,), (1,)), ((), ()))
    qt = lax.dot_general(wq_ref[...], hn, nt, preferred_element_type=F32) * (FOX_HEAD_DIM ** -0.5 * LOG2E)
    q_extra = lax.dot_general(pqt_ref[...], xterms, (((1,), (1,)), ((), ())),
                              preferred_element_type=F32)
    q_ref[0] = (head_rows(qt, jnp.zeros((pad_rows, tm), F32)) + q_extra).astype(BF16)
    vt = lax.dot_general(wv_ref[...], hn, nt, preferred_element_type=F32)
    ones_row = jnp.where(lax.broadcasted_iota(jnp.int32, (pad_rows, tm), 0) == 0, 1.0, 0.0)
    v_ref[0] = head_rows(vt, ones_row).astype(BF16)

    u = jnp.dot(hn, wu_ref[...], preferred_element_type=F32)
    ext = jnp.concatenate([halo_ref[...], u], axis=0)
    halo_ref[...] = u[tm - POOL_HALO:, :]
    t_pos = i * tm + lax.broadcasted_iota(jnp.int32, (tm, POOL_GROUP), 0)
    sums = ext
    outs = []
    for g, w in enumerate(POOL_WINDOWS):
        sums = sums + pltpu.roll(sums, w // 2, axis=0)
        win = sums[POOL_HALO:, :POOL_GROUP]
        count = jnp.minimum(t_pos + 1, w).astype(F32)
        diff = win / count - u[:, g * POOL_GROUP:(g + 1) * POOL_GROUP]
        outs.append(jnp.dot(diff.astype(BF16), wpool_ref[g], preferred_element_type=F32))
        if g + 1 < len(POOL_WINDOWS):
            sums = sums[:, POOL_GROUP:]
    y = jnp.concatenate(outs, axis=-1) * pscale_ref[...]
    ypool_ref[0] = y.astype(BF16)


def _ab_in(h, g, w):
    b, s, _ = h.shape
    tm = PROJ_TM
    hw = FOX_HEADS * LANES
    const2 = lambda shape: pl.BlockSpec(shape, lambda bi, i: (0,) * len(shape))
    tok = lambda width: pl.BlockSpec((1, tm, width), lambda bi, i: (bi, i, 0))
    tok_t = lambda width: pl.BlockSpec((1, width, tm), lambda bi, i: (bi, 0, i))
    return pl.pallas_call(
        _ab_in_kernel,
        name="ab_in",
        grid=(b, s // tm),
        in_specs=[
            tok(D_MODEL), const2((1, D_MODEL)),
            const2((D_MODEL, POOL_WIDTH)), const2((FOX_WIDTH, D_MODEL)), const2((D_MODEL, hw)),
            const2((FOX_WIDTH, D_MODEL)), const2((D_MODEL, LANES)), const2((1, LANES)),
            const2((tm, tm)), const2((hw, LANES)), const2((LANES, hw)),
            const2((len(POOL_WINDOWS), POOL_GROUP, POOL_GROUP)), const2((1, POOL_WIDTH)),
        ],
        out_specs=[tok(POOL_WIDTH), tok_t(hw), tok(hw), tok_t(hw)],
        out_shape=[
            jax.ShapeDtypeStruct((b, s, POOL_WIDTH), BF16),
            jax.ShapeDtypeStruct((b, hw, s), BF16),
            jax.ShapeDtypeStruct((b, s, hw), BF16),
            jax.ShapeDtypeStruct((b, hw, s), BF16),
        ],
        scratch_shapes=[
            pltpu.VMEM((POOL_HALO, POOL_WIDTH), F32),
            pltpu.VMEM((SUBLANES, LANES), F32),
        ],
        compiler_params=_params("arbitrary", "arbitrary"),
    )(h, g, w["wu"], w["wq"], w["wk"], w["wv"], w["wf"], w["bf"], w["tri"],
      w["pqt"], w["pk"], w["wpool"], w["pscale"])


def _head_groups(w, heads, width):
    rows = w.shape[0]
    w = w.reshape(rows, heads, width)
    w = jnp.pad(w, ((0, 0), (0, 0), (0, LANES - width)))
    return w.reshape(rows, heads * LANES)


def _prep_ab(w_in, b_forget, w_pool, pool_scale, w_out):
    o1, o2, o3, o4 = POOL_WIDTH, POOL_WIDTH + FOX_WIDTH, POOL_WIDTH + 2 * FOX_WIDTH, POOL_WIDTH + 3 * FOX_WIDTH
    wb = w_in.astype(BF16)
    hw = FOX_HEADS * LANES
    wf = jnp.pad(jnp.tile(wb[:, o4:], (1, 3)), ((0, 0), (0, LANES - 3 * FOX_HEADS)))
    bf = jnp.pad(jnp.tile(b_forget.astype(F32), 3), (0, LANES - 3 * FOX_HEADS))[None, :]
    r = jnp.arange(LANES)[:, None]
    c = jnp.arange(hw)[None, :]
    head, lane = c // LANES, c % LANES
    is_term = r < 3 * FOX_HEADS
    pq = jnp.where(is_term & (head == r % FOX_HEADS) & (lane == FOX_HEAD_DIM + r // FOX_HEADS), 1.0, 0.0)
    pq = pq + jnp.where((r == 3 * FOX_HEADS) & (lane >= FOX_HEAD_DIM + 3) & (lane < FOX_HEAD_DIM + 6), 1.0, 0.0)
    pk = jnp.where(is_term & (head == r % FOX_HEADS) & (lane == FOX_HEAD_DIM + 3 + r // FOX_HEADS), -1.0, 0.0)
    pk = pk + jnp.where((r == 3 * FOX_HEADS) & (lane >= FOX_HEAD_DIM) & (lane < FOX_HEAD_DIM + 3), 1.0, 0.0)
    tri = jnp.tril(jnp.ones((PROJ_TM, PROJ_TM), BF16))
    wo = w_out.astype(BF16)
    return {
        "wu": wb[:, :o1],
        "wq": wb[:, o1:o2].T,
        "wk": _head_groups(wb[:, o2:o3], FOX_HEADS, FOX_HEAD_DIM),
        "wv": wb[:, o3:o4].T,
        "wf": wf, "bf": bf, "tri": tri,
        "pqt": pq.astype(BF16).T, "pk": pk.astype(BF16),
        "wpool": w_pool.astype(BF16), "pscale": pool_scale.astype(F32)[None, :],
        "wo_pool": wo[:POOL_WIDTH], "wo_fox": wo[POOL_WIDTH:],
    }


def _mla_in_kernel(h_ref, g_ref, win_ref, qn_ref, kvn_ref, wqa_ref, wqb_ref, wk_ref, wv_ref,
                   cos_ref, sin_ref, q_ref, k_ref, v_ref):
    hn = _rms(h_ref[0], g_ref[...]).astype(BF16)
    proj = jnp.dot(hn, win_ref[...], preferred_element_type=F32)
    c_q = proj[:, :MLA_Q_LORA]
    c_kv = proj[:, MLA_Q_LORA:MLA_Q_LORA + MLA_KV_LORA]
    kr_a = proj[:, MLA_Q_LORA + MLA_KV_LORA:MLA_Q_LORA + MLA_KV_LORA + LANES]
    kr_b = proj[:, MLA_Q_LORA + MLA_KV_LORA + LANES:]
    cos = cos_ref[0]
    sin = sin_ref[0]
    k_rope = kr_a * cos + kr_b * sin

    qn = _rms(c_q, qn_ref[...]).astype(BF16)
    kvn = _rms(c_kv, kvn_ref[...]).astype(BF16)

    k_all = jnp.dot(kvn, wk_ref[...], preferred_element_type=F32)
    for hd in range(MLA_HEADS):
        grp = slice(hd * LANES, (hd + 1) * LANES)
        k_ref[0, :, grp] = (k_all[:, grp] + k_rope).astype(BF16)

    nt = (((1,), (1,)), ((), ()))
    scale = (MLA_NOPE + MLA_ROPE) ** -0.5 * LOG2E
    qk_dim = MLA_NOPE + MLA_ROPE
    q_a = lax.dot_general(wqa_ref[...], qn, nt, preferred_element_type=F32)
    q_b = lax.dot_general(wqb_ref[...], qn, nt, preferred_element_type=F32)
    v_t = lax.dot_general(wv_ref[...], kvn, nt, preferred_element_type=F32)
    cos_t = cos.T[MLA_NOPE:qk_dim] * scale
    sin_t = sin.T[MLA_NOPE:qk_dim] * scale
    tm = cos.shape[0]
    q_pad = jnp.zeros((LANES - qk_dim, tm), BF16)
    v_pad = jnp.where(lax.broadcasted_iota(jnp.int32, (LANES - MLA_V, tm), 0) == 0, 1.0, 0.0).astype(BF16)
    for hd in range(MLA_HEADS):
        row = hd * LANES
        qa = q_a[hd * qk_dim:(hd + 1) * qk_dim]
        q_ref[0, row:row + MLA_NOPE, :] = (qa[:MLA_NOPE] * scale).astype(BF16)
        q_ref[0, row + MLA_NOPE:row + qk_dim, :] = (
            qa[MLA_NOPE:] * cos_t + q_b[hd * MLA_ROPE:(hd + 1) * MLA_ROPE] * sin_t).astype(BF16)
        q_ref[0, row + qk_dim:row + LANES, :] = q_pad
        v_ref[0, row:row + MLA_V, :] = v_t[hd * MLA_V:(hd + 1) * MLA_V].astype(BF16)
        v_ref[0, row + MLA_V:row + LANES, :] = v_pad


def _mla_in(h, g, w, cos_l, sin_l):
    b, s, _ = h.shape
    tm = PROJ_TM
    hw = MLA_HEADS * LANES
    nin = MLA_Q_LORA + MLA_KV_LORA + 2 * LANES
    const2 = lambda shape: pl.BlockSpec(shape, lambda bi, i: (0,) * len(shape))
    tok = lambda width: pl.BlockSpec((1, tm, width), lambda bi, i: (bi, i, 0))
    tok_t = lambda width: pl.BlockSpec((1, width, tm), lambda bi, i: (bi, 0, i))
    return pl.pallas_call(
        _mla_in_kernel,
        name="mla_in",
        grid=(b, s // tm),
        in_specs=[
            tok(D_MODEL), const2((1, D_MODEL)), const2((D_MODEL, nin)),
            const2((1, MLA_Q_LORA)), const2((1, MLA_KV_LORA)),
            const2((MLA_HEADS * (MLA_NOPE + MLA_ROPE), MLA_Q_LORA)), const2((MLA_HEADS * MLA_ROPE, MLA_Q_LORA)),
            const2((MLA_KV_LORA, hw)), const2((MLA_HEADS * MLA_V, MLA_KV_LORA)),
            tok(LANES), tok(LANES),
        ],
        out_specs=[tok_t(hw), tok(hw), tok_t(hw)],
        out_shape=[jax.ShapeDtypeStruct((b, hw, s), BF16), jax.ShapeDtypeStruct((b, s, hw), BF16),
                   jax.ShapeDtypeStruct((b, hw, s), BF16)],
        compiler_params=_params("parallel", "parallel"),
    )(h, g, w["win"], w["qn"], w["kvn"], w["wqa"], w["wqb"], w["wk"], w["wv"], cos_l, sin_l)


def _rope_group(x1, x2, lead):
    z0 = jnp.zeros(lead + (MLA_NOPE,), x1.dtype)
    z1 = jnp.zeros(lead + (LANES - MLA_NOPE - MLA_ROPE,), x1.dtype)
    return jnp.concatenate([z0, x1, x2, z1], axis=-1)


def _prep_mla(w_in, q_norm, kv_norm, w_q_b, w_kv_b, w_out):
    half = MLA_ROPE // 2
    wb = w_in.astype(BF16)
    kr = wb[:, MLA_Q_LORA + MLA_KV_LORA:]
    a1, a2 = kr[:, :half], kr[:, half:]
    win = jnp.concatenate([
        wb[:, :MLA_Q_LORA + MLA_KV_LORA],
        _rope_group(a1, a2, (D_MODEL,)),
        _rope_group(-a2, a1, (D_MODEL,)),
    ], axis=-1)
    wq = w_q_b.astype(BF16).reshape(MLA_Q_LORA, MLA_HEADS, MLA_NOPE + MLA_ROPE)
    x1, x2 = wq[..., MLA_NOPE:MLA_NOPE + half], wq[..., MLA_NOPE + half:]
    wqa = w_q_b.astype(BF16).T
    wqb = jnp.concatenate([-x2, x1], axis=-1).reshape(MLA_Q_LORA, MLA_HEADS * MLA_ROPE).T
    wkv = w_kv_b.astype(BF16).reshape(MLA_KV_LORA, MLA_HEADS, MLA_NOPE + MLA_V)
    pad = jnp.zeros((MLA_KV_LORA, MLA_HEADS, LANES - MLA_NOPE), BF16)
    wk = jnp.concatenate([wkv[..., :MLA_NOPE], pad], axis=-1).reshape(MLA_KV_LORA, MLA_HEADS * LANES)
    wv = wkv[..., MLA_NOPE:].reshape(MLA_KV_LORA, MLA_HEADS * MLA_V).T
    return {
        "win": win, "qn": q_norm.astype(F32)[None, :], "kvn": kv_norm.astype(F32)[None, :],
        "wqa": wqa, "wqb": wqb, "wk": wk, "wv": wv, "wo": w_out.astype(BF16),
    }


def _attn_kernel(qt_ref, k_ref, vt_ref, o_ref, m_ref, acc_ref, s_ref, smax_ref, *, tq, tk, chunk, head_dim):
    i = pl.program_id(2)
    shift = chunk.bit_length() - 1
    tiles_per_block = tq // tk

    m_ref[...] = jnp.full(m_ref.shape, MASKED, F32)
    acc_ref[...] = jnp.zeros_like(acc_ref)
    groups = [slice(hh * LANES, (hh + 1) * LANES) for hh in range(HEADS_PER_STEP)]

    units = [(hh, slice(c, c + ATT_COLS)) for hh in range(HEADS_PER_STEP) for c in range(0, tq, ATT_COLS)]

    def scores_into(j, slot, hh, cols):
        off = pl.multiple_of(j * tk, tk)
        grp = groups[hh]
        s = jnp.dot(k_ref[0, pl.ds(off, tk), grp], qt_ref[0, grp, cols],
                    preferred_element_type=F32)
        s_ref[slot, hh, :, cols] = s
        smax_ref[slot, hh, :, cols] = jnp.broadcast_to(jnp.max(s, axis=0, keepdims=True),
                                                       (SUBLANES, s.shape[1]))

    def softmax_pv(j, slot, hh, cols, diag=None):
        off = pl.multiple_of(j * tk, tk)
        s = s_ref[slot, hh, :, cols]
        vt = vt_ref[0, hh * LANES:hh * LANES + V_ROWS, pl.ds(off, tk)]
        if diag is not None:
            key = lax.broadcasted_iota(jnp.int32, s.shape, 0) + diag * tk
            qry = lax.broadcasted_iota(jnp.int32, s.shape, 1) + cols.start
            s = jnp.where((key >> shift) <= (qry >> shift), s, MASKED)
        m_prev = m_ref[hh, 0:1, cols]
        s_max = smax_ref[slot, hh, 0:1, cols] if diag is None else jnp.max(s, axis=0, keepdims=True)
        m_new = jnp.maximum(m_prev, s_max)
        alpha = jnp.exp2(m_prev - m_new)
        p = jnp.exp2(s - m_new)
        acc_ref[hh, :, cols] = acc_ref[hh, :, cols] * alpha + jnp.dot(
            vt, p.astype(BF16), preferred_element_type=F32)
        m_ref[hh, :, cols] = jnp.broadcast_to(m_new, (SUBLANES, m_new.shape[1]))

    first_diag = i * tiles_per_block
    for hh, cols in units:
        scores_into(0, 0, hh, cols)

    def tile_pair(j):
        for hh, cols in units:
            scores_into(j + 1, 1, hh, cols)
            softmax_pv(j, 0, hh, cols)
        for hh, cols in units:
            scores_into(j + 2, 0, hh, cols)
            softmax_pv(j + 1, 1, hh, cols)

    def tile_quad(quad, carry):
        tile_pair(4 * quad)
        tile_pair(4 * quad + 2)
        return carry

    n_pairs = first_diag // 2
    lax.fori_loop(0, n_pairs // 2, tile_quad, 0)

    @pl.when(n_pairs % 2 == 1)
    def _():
        tile_pair(first_diag - 2)

    for d in range(tiles_per_block):
        for hh, cols in units:
            if d + 1 < tiles_per_block and cols.start >= (d + 1) * tk:
                scores_into(first_diag + d + 1, (d + 1) % 2, hh, cols)
            if cols.start >= (d + 1) * tk:
                softmax_pv(first_diag + d, d % 2, hh, cols)
            elif cols.start >= d * tk:
                softmax_pv(first_diag + d, d % 2, hh, cols, diag=d)

    outs = []
    for hh in range(HEADS_PER_STEP):
        acc = acc_ref[hh]
        outs.append(acc[:head_dim, :] / acc[head_dim:head_dim + 1, :])
    o_ref[0] = jnp.concatenate(outs, axis=0).T.astype(BF16)


def _attention(qt, k, vt, heads, chunk, head_dim):
    b, s, _ = k.shape
    tq, tk = ATT_TQ, ATT_TK
    assert tq % (2 * tk) == 0 and tk % chunk == 0 and s % tq == 0 and heads % HEADS_PER_STEP == 0
    assert tk % ATT_COLS == 0 and tq % ATT_COLS == 0
    assert (HEADS_PER_STEP * head_dim) % LANES == 0
    gw = HEADS_PER_STEP * LANES
    return pl.pallas_call(
        functools.partial(_attn_kernel, tq=tq, tk=tk, chunk=chunk, head_dim=head_dim),
        name="attention",
        grid=(b, heads // HEADS_PER_STEP, s // tq),
        in_specs=[
            pl.BlockSpec((1, gw, tq), lambda bi, hp, i: (bi, hp, i)),
            pl.BlockSpec((1, s, gw), lambda bi, hp, i: (bi, 0, hp)),
            pl.BlockSpec((1, gw, s), lambda bi, hp, i: (bi, hp, 0)),
        ],
        out_specs=pl.BlockSpec((1, tq, HEADS_PER_STEP * head_dim), lambda bi, hp, i: (bi, i, hp)),
        out_shape=jax.ShapeDtypeStruct((b, s, heads * head_dim), BF16),
        scratch_shapes=[
            pltpu.VMEM((HEADS_PER_STEP, SUBLANES, tq), F32),
            pltpu.VMEM((HEADS_PER_STEP, V_ROWS, tq), F32),
            pltpu.VMEM((2, HEADS_PER_STEP, tk, tq), F32),
            pltpu.VMEM((2, HEADS_PER_STEP, SUBLANES, tq), F32),
        ],
        compiler_params=_params("parallel", "parallel", "arbitrary"),
    )(qt, k, vt)


def _rope_lane_tables(positions):
    inv_freq = ROPE_THETA ** (-jnp.arange(0, MLA_ROPE, 2, dtype=F32) / MLA_ROPE)
    ang = positions.astype(F32)[..., None] * inv_freq
    cos, sin = jnp.cos(ang), jnp.sin(ang)
    lead = positions.shape
    return _rope_group(cos, cos, lead), _rope_group(sin, sin, lead)


def kernel(x, positions, norm_ffn, norm_mix, norm_final, ffn_w_gate, ffn_w_up, ffn_w_down,
           ab_w_in, ab_b_forget, pool_w, pool_scale, ab_w_out,
           mla_w_in, mla_q_norm, mla_kv_norm, mla_w_q_b, mla_w_kv_b, mla_w_out):
    cos_l, sin_l = _rope_lane_tables(positions)
    w_gate, w_up, w_down = (w.astype(BF16) for w in (ffn_w_gate, ffn_w_up, ffn_w_down))
    h = x.astype(F32)
    for layer in range(DEPTH):
        idx = layer // 2
        h = _ffn(h, norm_ffn[layer, 0][None, :], w_gate, w_up, w_down, (layer, 0))
        g_mix = norm_mix[layer][None, :]
        if layer % 2 == 0:
            w = _prep_ab(ab_w_in[idx], ab_b_forget[idx], pool_w[idx], pool_scale[idx], ab_w_out[idx])
            y_pool, q, k, v = _ab_in(h, g_mix, w)
            y_fox = _attention(q, k, v, FOX_HEADS, 1, FOX_HEAD_DIM)
            mix = ((y_pool, w["wo_pool"]), (y_fox, w["wo_fox"]))
        else:
            w = _prep_mla(mla_w_in[idx], mla_q_norm[idx], mla_kv_norm[idx], mla_w_q_b[idx],
                          mla_w_kv_b[idx], mla_w_out[idx])
            q, k, v = _mla_in(h, g_mix, w, cos_l, sin_l)
            y = _attention(q, k, v, MLA_HEADS, CHUNK, MLA_V)
            mix = ((y, w["wo"]),)
        h = _ffn(h, norm_ffn[layer, 1][None, :], w_gate, w_up, w_down, (layer, 1), mix=mix,
                 final_g=norm_final[None, :] if layer == DEPTH - 1 else None)
    return h
```

```python
import functools
import math

import jax
import jax.numpy as jnp
from jax import lax
from jax.experimental import pallas as pl
from jax.experimental.pallas import tpu as pltpu

F32 = jnp.float32
BF16 = jnp.bfloat16

D_MODEL = 1024
DEPTH = 4
CHUNK = 64
RMS_EPS = 1e-6
D_FF = 2816
POOL_WINDOWS = (2, 4, 8, 16)
POOL_GROUP = 128
POOL_WIDTH = 512
FOX_HEADS = 8
FOX_HEAD_DIM = 64
FOX_WIDTH = 512
MLA_HEADS = 16
MLA_NOPE = 64
MLA_ROPE = 32
MLA_V = 64
MLA_Q_LORA = 256
MLA_KV_LORA = 128
ROPE_THETA = 10000.0

LANES = 128
SUBLANES = 8
VMEM_LIMIT_BYTES = 56 * 1024 * 1024

FFN_TM = 1024
FFN_FC = 256
FFN_NC = D_FF // FFN_FC
PROJ_TM = 512
ATT_TQ = 2048
ATT_TK = 512
HEADS_PER_STEP = 2
ATT_COLS = 256
V_ROWS = 80
POOL_HALO = 16

LOG2E = math.log2(math.e)
MASKED = -1e30


def _rms(x, g):
    return x * lax.rsqrt(jnp.mean(x * x, axis=-1, keepdims=True) + RMS_EPS) * g


def _params(*sem):
    return pltpu.CompilerParams(dimension_semantics=sem, vmem_limit_bytes=VMEM_LIMIT_BYTES)


def _ffn_kernel(*refs, n_mix, final_norm):
    h_ref = refs[0]
    y_refs = refs[1:1 + n_mix]
    wo_refs = refs[1 + n_mix:1 + 2 * n_mix]
    rest = refs[1 + 2 * n_mix:]
    g_ref, wg_ref, wu_ref, wd_ref = rest[:4]
    gf_ref = rest[4] if final_norm else None
    o_ref, xn_ref, acc_ref, act_ref = rest[-4:]

    h = h_ref[0]
    for y_ref, wo_ref in zip(y_refs, wo_refs):
        h = h + jnp.dot(y_ref[0], wo_ref[...], preferred_element_type=F32)
    xn_ref[...] = _rms(h, g_ref[...]).astype(BF16)
    if n_mix:
        o_ref[0] = h

    def hidden(c):
        cols = pl.ds(pl.multiple_of(c * FFN_FC, FFN_FC), FFN_FC)
        gate = jnp.dot(xn_ref[...], wg_ref[:, cols], preferred_element_type=F32)
        up = jnp.dot(xn_ref[...], wu_ref[:, cols], preferred_element_type=F32)
        return (gate * jax.nn.sigmoid(gate) * up).astype(BF16)

    def down(c):
        return wd_ref[pl.ds(pl.multiple_of(c * FFN_FC, FFN_FC), FFN_FC), :]

    act_ref[0] = hidden(0)
    acc_ref[...] = jnp.zeros_like(acc_ref)

    def chunk_pair(pair, carry):
        c = 2 * pair
        act_ref[1] = hidden(c + 1)
        acc_ref[...] += jnp.dot(act_ref[0], down(c), preferred_element_type=F32)
        act_ref[0] = hidden(c + 2)
        acc_ref[...] += jnp.dot(act_ref[1], down(c + 1), preferred_element_type=F32)
        return carry

    assert FFN_NC % 2 == 1
    lax.fori_loop(0, FFN_NC // 2, chunk_pair, 0)
    resid = o_ref[0] if n_mix else h_ref[0]
    out = resid + 0.5 * (acc_ref[...] + jnp.dot(act_ref[0], down(FFN_NC - 1), preferred_element_type=F32))
    o_ref[0] = _rms(out, gf_ref[...]) if final_norm else out


def _ffn(h, g, wg, wu, wd, which, mix=(), final_g=None):
    b, s, _ = h.shape
    tm = FFN_TM
    row = lambda width: pl.BlockSpec((1, tm, width), lambda bi, i: (bi, i, 0))
    const = lambda shape: pl.BlockSpec(shape, lambda bi, i: (0,) * len(shape), pipeline_mode=pl.Buffered(1))
    ys = [y for y, _ in mix]
    wos = [w for _, w in mix]
    in_specs = [row(D_MODEL)] + [row(y.shape[2]) for y in ys] + [const(w.shape) for w in wos]
    pick = lambda rows, cols: pl.BlockSpec((None, None, rows, cols), lambda bi, i: (*which, 0, 0),
                                            pipeline_mode=pl.Buffered(1))
    in_specs += [const((1, D_MODEL)), pick(D_MODEL, D_FF), pick(D_MODEL, D_FF), pick(D_FF, D_MODEL)]
    args = [h, *ys, *wos, g, wg, wu, wd]
    if final_g is not None:
        in_specs.append(const((1, D_MODEL)))
        args.append(final_g)
    return pl.pallas_call(
        functools.partial(_ffn_kernel, n_mix=len(mix), final_norm=final_g is not None),
        name="ffn",
        grid=(b, s // tm),
        in_specs=in_specs,
        out_specs=row(D_MODEL),
        out_shape=jax.ShapeDtypeStruct(h.shape, F32),
        scratch_shapes=[
            pltpu.VMEM((tm, D_MODEL), BF16),
            pltpu.VMEM((tm, D_MODEL), F32),
            pltpu.VMEM((2, tm, FFN_FC), BF16),
        ],
        compiler_params=_params("parallel", "parallel"),
    )(*args)


def _ab_in_kernel(h_ref, g_ref, wu_ref, wq_ref, wk_ref, wv_ref, wf_ref, bf_ref, tri_ref,
                  pqt_ref, pk_ref, wpool_ref, pscale_ref,
                  ypool_ref, q_ref, k_ref, v_ref, halo_ref, fcarry_ref):
    i = pl.program_id(1)
    tm = h_ref.shape[1]

    @pl.when(i == 0)
    def _():
        halo_ref[...] = jnp.zeros_like(halo_ref)
        fcarry_ref[...] = jnp.zeros_like(fcarry_ref)

    hn = _rms(h_ref[0], g_ref[...]).astype(BF16)

    nt = (((1,), (1,)), ((), ()))
    k_main = jnp.dot(hn, wk_ref[...], preferred_element_type=F32)
    qt = lax.dot_general(wq_ref[...], hn, nt, preferred_element_type=F32) * (FOX_HEAD_DIM ** -0.5 * LOG2E)
    vt = lax.dot_general(wv_ref[...], hn, nt, preferred_element_type=F32)
    u = jnp.dot(hn, wu_ref[...], preferred_element_type=F32)

    logit = jnp.dot(hn, wf_ref[...], preferred_element_type=F32) + bf_ref[...]
    log_f = jnp.minimum(logit, 0.0) - jnp.log1p(jnp.exp(-jnp.abs(logit)))
    tri = tri_ref[...]

    def split3(x):
        hi = x.astype(BF16)
        r1 = x - hi.astype(F32)
        mid = r1.astype(BF16)
        lo = (r1 - mid.astype(F32)).astype(BF16)
        return hi, mid, lo

    hi, mid, lo = split3(log_f)
    csum = (jnp.dot(tri, hi, preferred_element_type=F32)
            + jnp.dot(tri, mid, preferred_element_type=F32)
            + jnp.dot(tri, lo, preferred_element_type=F32))
    cum_f = csum + fcarry_ref[0:1, :]
    fcarry_ref[...] = jnp.broadcast_to(cum_f[tm - 1:tm, :], fcarry_ref.shape)

    fh, fm, fl = (x.astype(F32) for x in split3(cum_f * LOG2E))
    lane = lax.broadcasted_iota(jnp.int32, fh.shape, 1)
    xterms = jnp.where(lane < 8, fh, jnp.where(lane < 16, fm, jnp.where(
        lane < 24, fl, jnp.where(lane == 24, 1.0, 0.0)))).astype(BF16)

    k_ref[0] = (k_main + jnp.dot(xterms, pk_ref[...], preferred_element_type=F32)).astype(BF16)

    def head_rows(xt, filler):
        parts = []
        for hd in range(FOX_HEADS):
            parts += [xt[hd * FOX_HEAD_DIM:(hd + 1) * FOX_HEAD_DIM], filler]
        return jnp.concatenate(parts, axis=0)

    pad_rows = LANES - FOX_HEAD_DIM
    q_extra = lax.dot_general(pqt_ref[...], xterms, nt, preferred_element_type=F32)
    q_ref[0] = (head_rows(qt, jnp.zeros((pad_rows, tm), F32)) + q_extra).astype(BF16)
    ones_row = jnp.where(lax.broadcasted_iota(jnp.int32, (pad_rows, tm), 0) == 0, 1.0, 0.0)
    v_ref[0] = head_rows(vt, ones_row).astype(BF16)

    ext = jnp.concatenate([halo_ref[...], u], axis=0)
    halo_ref[...] = u[tm - POOL_HALO:, :]
    t_pos = i * tm + lax.broadcasted_iota(jnp.int32, (tm, POOL_GROUP), 0)
    sums = ext
    outs = []
    for g, w in enumerate(POOL_WINDOWS):
        sums = sums + pltpu.roll(sums, w // 2, axis=0)
        win = sums[POOL_HALO:, :POOL_GROUP]
        count = jnp.minimum(t_pos + 1, w).astype(F32)
        diff = win / count - u[:, g * POOL_GROUP:(g + 1) * POOL_GROUP]
        outs.append(jnp.dot(diff.astype(BF16), wpool_ref[g], preferred_element_type=F32))
        if g + 1 < len(POOL_WINDOWS):
            sums = sums[:, POOL_GROUP:]
    y = jnp.concatenate(outs, axis=-1) * pscale_ref[...]
    ypool_ref[0] = y.astype(BF16)


def _ab_in(h, g, w):
    b, s, _ = h.shape
    tm = PROJ_TM
    hw = FOX_HEADS * LANES
    const2 = lambda shape: pl.BlockSpec(shape, lambda bi, i: (0,) * len(shape))
    tok = lambda width: pl.BlockSpec((1, tm, width), lambda bi, i: (bi, i, 0))
    tok_t = lambda width: pl.BlockSpec((1, width, tm), lambda bi, i: (bi, 0, i))
    return pl.pallas_call(
        _ab_in_kernel,
        name="ab_in",
        grid=(b, s // tm),
        in_specs=[
            tok(D_MODEL), const2((1, D_MODEL)),
            const2((D_MODEL, POOL_WIDTH)), const2((FOX_WIDTH, D_MODEL)), const2((D_MODEL, hw)),
            const2((FOX_WIDTH, D_MODEL)), const2((D_MODEL, LANES)), const2((1, LANES)),
            const2((tm, tm)), const2((hw, LANES)), const2((LANES, hw)),
            const2((len(POOL_WINDOWS), POOL_GROUP, POOL_GROUP)), const2((1, POOL_WIDTH)),
        ],
        out_specs=[tok(POOL_WIDTH), tok_t(hw), tok(hw), tok_t(hw)],
        out_shape=[
            jax.ShapeDtypeStruct((b, s, POOL_WIDTH), BF16),
            jax.ShapeDtypeStruct((b, hw, s), BF16),
            jax.ShapeDtypeStruct((b, s, hw), BF16),
            jax.ShapeDtypeStruct((b, hw, s), BF16),
        ],
        scratch_shapes=[
            pltpu.VMEM((POOL_HALO, POOL_WIDTH), F32),
            pltpu.VMEM((SUBLANES, LANES), F32),
        ],
        compiler_params=_params("arbitrary", "arbitrary"),
    )(h, g, w["wu"], w["wq"], w["wk"], w["wv"], w["wf"], w["bf"], w["tri"],
      w["pqt"], w["pk"], w["wpool"], w["pscale"])


def _head_groups(w, heads, width):
    rows = w.shape[0]
    w = w.reshape(rows, heads, width)
    w = jnp.pad(w, ((0, 0), (0, 0), (0, LANES - width)))
    return w.reshape(rows, heads * LANES)


def _prep_ab(w_in, b_forget, w_pool, pool_scale, w_out):
    o1, o2, o3, o4 = POOL_WIDTH, POOL_WIDTH + FOX_WIDTH, POOL_WIDTH + 2 * FOX_WIDTH, POOL_WIDTH + 3 * FOX_WIDTH
    wb = w_in.astype(BF16)
    hw = FOX_HEADS * LANES
    wf = jnp.pad(jnp.tile(wb[:, o4:], (1, 3)), ((0, 0), (0, LANES - 3 * FOX_HEADS)))
    bf = jnp.pad(jnp.tile(b_forget.astype(F32), 3), (0, LANES - 3 * FOX_HEADS))[None, :]
    r = jnp.arange(LANES)[:, None]
    c = jnp.arange(hw)[None, :]
    head, lane = c // LANES, c % LANES
    is_term = r < 3 * FOX_HEADS
    pq = jnp.where(is_term & (head == r % FOX_HEADS) & (lane == FOX_HEAD_DIM + r // FOX_HEADS), 1.0, 0.0)
    pq = pq + jnp.where((r == 3 * FOX_HEADS) & (lane >= FOX_HEAD_DIM + 3) & (lane < FOX_HEAD_DIM + 6), 1.0, 0.0)
    pk = jnp.where(is_term & (head == r % FOX_HEADS) & (lane == FOX_HEAD_DIM + 3 + r // FOX_HEADS), -1.0, 0.0)
    pk = pk + jnp.where((r == 3 * FOX_HEADS) & (lane >= FOX_HEAD_DIM) & (lane < FOX_HEAD_DIM + 3), 1.0, 0.0)
    tri = jnp.tril(jnp.ones((PROJ_TM, PROJ_TM), BF16))
    wo = w_out.astype(BF16)
    return {
        "wu": wb[:, :o1],
        "wq": wb[:, o1:o2].T,
        "wk": _head_groups(wb[:, o2:o3], FOX_HEADS, FOX_HEAD_DIM),
        "wv": wb[:, o3:o4].T,
        "wf": wf, "bf": bf, "tri": tri,
        "pqt": pq.astype(BF16).T, "pk": pk.astype(BF16),
        "wpool": w_pool.astype(BF16), "pscale": pool_scale.astype(F32)[None, :],
        "wo_pool": wo[:POOL_WIDTH], "wo_fox": wo[POOL_WIDTH:],
    }


def _mla_in_kernel(h_ref, g_ref, win_ref, qn_ref, kvn_ref, wqa_ref, wqb_ref, wk_ref, wv_ref,
                   cos_ref, sin_ref, q_ref, k_ref, v_ref):
    hn = _rms(h_ref[0], g_ref[...]).astype(BF16)
    proj = jnp.dot(hn, win_ref[...], preferred_element_type=F32)
    c_q = proj[:, :MLA_Q_LORA]
    c_kv = proj[:, MLA_Q_LORA:MLA_Q_LORA + MLA_KV_LORA]
    kr_a = proj[:, MLA_Q_LORA + MLA_KV_LORA:MLA_Q_LORA + MLA_KV_LORA + LANES]
    kr_b = proj[:, MLA_Q_LORA + MLA_KV_LORA + LANES:]
    cos = cos_ref[0]
    sin = sin_ref[0]
    k_rope = kr_a * cos + kr_b * sin

    qn = _rms(c_q, qn_ref[...]).astype(BF16)
    kvn = _rms(c_kv, kvn_ref[...]).astype(BF16)

    k_all = jnp.dot(kvn, wk_ref[...], preferred_element_type=F32)
    for hd in range(MLA_HEADS):
        grp = slice(hd * LANES, (hd + 1) * LANES)
        k_ref[0, :, grp] = (k_all[:, grp] + k_rope).astype(BF16)

    nt = (((1,), (1,)), ((), ()))
    scale = (MLA_NOPE + MLA_ROPE) ** -0.5 * LOG2E
    qk_dim = MLA_NOPE + MLA_ROPE
    q_a = lax.dot_general(wqa_ref[...], qn, nt, preferred_element_type=F32)
    q_b = lax.dot_general(wqb_ref[...], qn, nt, preferred_element_type=F32)
    v_t = lax.dot_general(wv_ref[...], kvn, nt, preferred_element_type=F32)
    cos_t = cos.T[MLA_NOPE:qk_dim] * scale
    sin_t = sin.T[MLA_NOPE:qk_dim] * scale
    tm = cos.shape[0]
    q_pad = jnp.zeros((LANES - qk_dim, tm), BF16)
    v_pad = jnp.where(lax.broadcasted_iota(jnp.int32, (LANES - MLA_V, tm), 0) == 0, 1.0, 0.0).astype(BF16)
    for hd in range(MLA_HEADS):
        row = hd * LANES
        qa = q_a[hd * qk_dim:(hd + 1) * qk_dim]
        q_ref[0, row:row + MLA_NOPE, :] = (qa[:MLA_NOPE] * scale).astype(BF16)
        q_ref[0, row + MLA_NOPE:row + qk_dim, :] = (
            qa[MLA_NOPE:] * cos_t + q_b[hd * MLA_ROPE:(hd + 1) * MLA_ROPE] * sin_t).astype(BF16)
        q_ref[0, row + qk_dim:row + LANES, :] = q_pad
        v_ref[0, row:row + MLA_V, :] = v_t[hd * MLA_V:(hd + 1) * MLA_V].astype(BF16)
        v_ref[0, row + MLA_V:row + LANES, :] = v_pad


def _mla_in(h, g, w, cos_l, sin_l):
    b, s, _ = h.shape
    tm = PROJ_TM
    hw = MLA_HEADS * LANES
    nin = MLA_Q_LORA + MLA_KV_LORA + 2 * LANES
    const2 = lambda shape: pl.BlockSpec(shape, lambda bi, i: (0,) * len(shape))
    tok = lambda width: pl.BlockSpec((1, tm, width), lambda bi, i: (bi, i, 0))
    tok_t = lambda width: pl.BlockSpec((1, width, tm), lambda bi, i: (bi, 0, i))
    return pl.pallas_call(
        _mla_in_kernel,
        name="mla_in",
        grid=(b, s // tm),
        in_specs=[
            tok(D_MODEL), const2((1, D_MODEL)), const2((D_MODEL, nin)),
            const2((1, MLA_Q_LORA)), const2((1, MLA_KV_LORA)),
            const2((MLA_HEADS * (MLA_NOPE + MLA_ROPE), MLA_Q_LORA)), const2((MLA_HEADS * MLA_ROPE, MLA_Q_LORA)),
            const2((MLA_KV_LORA, hw)), const2((MLA_HEADS * MLA_V, MLA_KV_LORA)),
            tok(LANES), tok(LANES),
        ],
        out_specs=[tok_t(hw), tok(hw), tok_t(hw)],
        out_shape=[jax.ShapeDtypeStruct((b, hw, s), BF16), jax.ShapeDtypeStruct((b, s, hw), BF16),
                   jax.ShapeDtypeStruct((b, hw, s), BF16)],
        compiler_params=_params("parallel", "parallel"),
    )(h, g, w["win"], w["qn"], w["kvn"], w["wqa"], w["wqb"], w["wk"], w["wv"], cos_l, sin_l)


def _rope_group(x1, x2, lead):
    z0 = jnp.zeros(lead + (MLA_NOPE,), x1.dtype)
    z1 = jnp.zeros(lead + (LANES - MLA_NOPE - MLA_ROPE,), x1.dtype)
    return jnp.concatenate([z0, x1, x2, z1], axis=-1)


def _prep_mla(w_in, q_norm, kv_norm, w_q_b, w_kv_b, w_out):
    half = MLA_ROPE // 2
    wb = w_in.astype(BF16)
    kr = wb[:, MLA_Q_LORA + MLA_KV_LORA:]
    a1, a2 = kr[:, :half], kr[:, half:]
    win = jnp.concatenate([
        wb[:, :MLA_Q_LORA + MLA_KV_LORA],
        _rope_group(a1, a2, (D_MODEL,)),
        _rope_group(-a2, a1, (D_MODEL,)),
    ], axis=-1)
    wq = w_q_b.astype(BF16).reshape(MLA_Q_LORA, MLA_HEADS, MLA_NOPE + MLA_ROPE)
    x1, x2 = wq[..., MLA_NOPE:MLA_NOPE + half], wq[..., MLA_NOPE + half:]
    wqa = w_q_b.astype(BF16).T
    wqb = jnp.concatenate([-x2, x1], axis=-1).reshape(MLA_Q_LORA, MLA_HEADS * MLA_ROPE).T
    wkv = w_kv_b.astype(BF16).reshape(MLA_KV_LORA, MLA_HEADS, MLA_NOPE + MLA_V)
    pad = jnp.zeros((MLA_KV_LORA, MLA_HEADS, LANES - MLA_NOPE), BF16)
    wk = jnp.concatenate([wkv[..., :MLA_NOPE], pad], axis=-1).reshape(MLA_KV_LORA, MLA_HEADS * LANES)
    wv = wkv[..., MLA_NOPE:].reshape(MLA_KV_LORA, MLA_HEADS * MLA_V).T
    return {
        "win": win, "qn": q_norm.astype(F32)[None, :], "kvn": kv_norm.astype(F32)[None, :],
        "wqa": wqa, "wqb": wqb, "wk": wk, "wv": wv, "wo": w_out.astype(BF16),
    }


def _attn_kernel(qt_ref, k_ref, vt_ref, o_ref, m_ref, acc_ref, s_ref, smax_ref, *, tq, tk, chunk, head_dim):
    i = pl.program_id(2)
    shift = chunk.bit_length() - 1
    tiles_per_block = tq // tk

    m_ref[...] = jnp.full(m_ref.shape, MASKED, F32)
    acc_ref[...] = jnp.zeros_like(acc_ref)
    groups = [slice(hh * LANES, (hh + 1) * LANES) for hh in range(HEADS_PER_STEP)]

    units = [(hh, slice(c, c + ATT_COLS)) for hh in range(HEADS_PER_STEP) for c in range(0, tq, ATT_COLS)]

    def scores_into(j, slot, hh, cols):
        off = pl.multiple_of(j * tk, tk)
        grp = groups[hh]
        s = jnp.dot(k_ref[0, pl.ds(off, tk), grp], qt_ref[0, grp, cols],
                    preferred_element_type=F32)
        s_ref[slot, hh, :, cols] = s
        smax_ref[slot, hh, :, cols] = jnp.broadcast_to(jnp.max(s, axis=0, keepdims=True),
                                                       (SUBLANES, s.shape[1]))

    def softmax_pv(j, slot, hh, cols, diag=None):
        off = pl.multiple_of(j * tk, tk)
        s = s_ref[slot, hh, :, cols]
        vt = vt_ref[0, hh * LANES:hh * LANES + V_ROWS, pl.ds(off, tk)]
        if diag is not None:
            key = lax.broadcasted_iota(jnp.int32, s.shape, 0) + diag * tk
            qry = lax.broadcasted_iota(jnp.int32, s.shape, 1) + cols.start
            s = jnp.where((key >> shift) <= (qry >> shift), s, MASKED)
        m_prev = m_ref[hh, 0:1, cols]
        s_max = smax_ref[slot, hh, 0:1, cols] if diag is None else jnp.max(s, axis=0, keepdims=True)
        m_new = jnp.maximum(m_prev, s_max)
        alpha = jnp.exp2(m_prev - m_new)
        p = jnp.exp2(s - m_new)
        acc_ref[hh, :, cols] = acc_ref[hh, :, cols] * alpha + jnp.dot(
            vt, p.astype(BF16), preferred_element_type=F32)
        m_ref[hh, :, cols] = jnp.broadcast_to(m_new, (SUBLANES, m_new.shape[1]))

    first_diag = i * tiles_per_block
    for hh, cols in units:
        scores_into(0, 0, hh, cols)

    def tile_pair(j):
        for hh, cols in units:
            scores_into(j + 1, 1, hh, cols)
            softmax_pv(j, 0, hh, cols)
        for hh, cols in units:
            scores_into(j + 2, 0, hh, cols)
            softmax_pv(j + 1, 1, hh, cols)

    def tile_quad(quad, carry):
        tile_pair(4 * quad)
        tile_pair(4 * quad + 2)
        return carry

    n_pairs = first_diag // 2
    lax.fori_loop(0, n_pairs // 2, tile_quad, 0)

    if (tiles_per_block // 2) % 2:
        @pl.when(n_pairs % 2 == 1)
        def _():
            tile_pair(first_diag - 2)

    for d in range(tiles_per_block):
        for hh, cols in units:
            if d + 1 < tiles_per_block and cols.start >= (d + 1) * tk:
                scores_into(first_diag + d + 1, (d + 1) % 2, hh, cols)
            if cols.start >= (d + 1) * tk:
                softmax_pv(first_diag + d, d % 2, hh, cols)
            elif cols.start >= d * tk:
                softmax_pv(first_diag + d, d % 2, hh, cols, diag=d)

    outs = []
    for hh in range(HEADS_PER_STEP):
        acc = acc_ref[hh]
        outs.append(acc[:head_dim, :] / acc[head_dim:head_dim + 1, :])
    o_ref[0] = jnp.concatenate(outs, axis=0).T.astype(BF16)


def _attention(qt, k, vt, heads, chunk, head_dim):
    b, s, _ = k.shape
    tq, tk = ATT_TQ, ATT_TK
    assert tq % (2 * tk) == 0 and tk % chunk == 0 and s % tq == 0 and heads % HEADS_PER_STEP == 0
    assert tk % ATT_COLS == 0 and tq % ATT_COLS == 0
    assert (HEADS_PER_STEP * head_dim) % LANES == 0
    gw = HEADS_PER_STEP * LANES
    return pl.pallas_call(
        functools.partial(_attn_kernel, tq=tq, tk=tk, chunk=chunk, head_dim=head_dim),
        name="attention",
        grid=(b, heads // HEADS_PER_STEP, s // tq),
        in_specs=[
            pl.BlockSpec((1, gw, tq), lambda bi, hp, i: (bi, hp, i)),
            pl.BlockSpec((1, s, gw), lambda bi, hp, i: (bi, 0, hp)),
            pl.BlockSpec((1, gw, s), lambda bi, hp, i: (bi, hp, 0), pipeline_mode=pl.Buffered(1)),
        ],
        out_specs=pl.BlockSpec((1, tq, HEADS_PER_STEP * head_dim), lambda bi, hp, i: (bi, i, hp)),
        out_shape=jax.ShapeDtypeStruct((b, s, heads * head_dim), BF16),
        scratch_shapes=[
            pltpu.VMEM((HEADS_PER_STEP, SUBLANES, tq), F32),
            pltpu.VMEM((HEADS_PER_STEP, V_ROWS, tq), F32),
            pltpu.VMEM((2, HEADS_PER_STEP, tk, tq), F32),
            pltpu.VMEM((2, HEADS_PER_STEP, SUBLANES, tq), F32),
        ],
        compiler_params=_params("parallel", "parallel", "arbitrary"),
    )(qt, k, vt)


def _rope_lane_tables(positions):
    inv_freq = ROPE_THETA ** (-jnp.arange(0, MLA_ROPE, 2, dtype=F32) / MLA_ROPE)
    ang = positions.astype(F32)[..., None] * inv_freq
    cos, sin = jnp.cos(ang), jnp.sin(ang)
    lead = positions.shape
    return _rope_group(cos, cos, lead), _rope_group(sin, sin, lead)


def kernel(x, positions, norm_ffn, norm_mix, norm_final, ffn_w_gate, ffn_w_up, ffn_w_down,
           ab_w_in, ab_b_forget, pool_w, pool_scale, ab_w_out,
           mla_w_in, mla_q_norm, mla_kv_norm, mla_w_q_b, mla_w_kv_b, mla_w_out):
    cos_l, sin_l = _rope_lane_tables(positions)
    w_gate, w_up, w_down = (w.astype(BF16) for w in (ffn_w_gate, ffn_w_up, ffn_w_down))
    h = x.astype(F32)
    for layer in range(DEPTH):
        idx = layer // 2
        h = _ffn(h, norm_ffn[layer, 0][None, :], w_gate, w_up, w_down, (layer, 0))
        g_mix = norm_mix[layer][None, :]
        if layer % 2 == 0:
            w = _prep_ab(ab_w_in[idx], ab_b_forget[idx], pool_w[idx], pool_scale[idx], ab_w_out[idx])
            y_pool, q, k, v = _ab_in(h, g_mix, w)
            y_fox = _attention(q, k, v, FOX_HEADS, 1, FOX_HEAD_DIM)
            mix = ((y_pool, w["wo_pool"]), (y_fox, w["wo_fox"]))
        else:
            w = _prep_mla(mla_w_in[idx], mla_q_norm[idx], mla_kv_norm[idx], mla_w_q_b[idx],
                          mla_w_kv_b[idx], mla_w_out[idx])
            q, k, v = _mla_in(h, g_mix, w, cos_l, sin_l)
            y = _attention(q, k, v, MLA_HEADS, CHUNK, MLA_V)
            mix = ((y, w["wo"]),)
        h = _ffn(h, norm_ffn[layer, 1][None, :], w_gate, w_up, w_down, (layer, 1), mix=mix,
                 final_g=norm_final[None, :] if layer == DEPTH - 1 else None)
    return h
```

```python
import functools
import math

import jax
import jax.numpy as jnp
from jax import lax
from jax.experimental import pallas as pl
from jax.experimental.pallas import tpu as pltpu

F32 = jnp.float32
BF16 = jnp.bfloat16

D_MODEL = 1024
DEPTH = 4
CHUNK = 64
RMS_EPS = 1e-6
D_FF = 2816
POOL_WINDOWS = (2, 4, 8, 16)
POOL_GROUP = 128
POOL_WIDTH = 512
FOX_HEADS = 8
FOX_HEAD_DIM = 64
FOX_WIDTH = 512
MLA_HEADS = 16
MLA_NOPE = 64
MLA_ROPE = 32
MLA_V = 64
MLA_Q_LORA = 256
MLA_KV_LORA = 128
ROPE_THETA = 10000.0

LANES = 128
SUBLANES = 8
VMEM_LIMIT_BYTES = 56 * 1024 * 1024

FFN_TM = 1024
FFN_FC = 256
FFN_NC = D_FF // FFN_FC
PROJ_TM = 512
ATT_TQ = 2048
ATT_TK = 512
HEADS_PER_STEP = 2
ATT_COLS = 256
V_ROWS = 80
POOL_HALO = 16

LOG2E = math.log2(math.e)
MASKED = -1e30


def _rms(x, g):
    return x * lax.rsqrt(jnp.mean(x * x, axis=-1, keepdims=True) + RMS_EPS) * g


def _params(*sem):
    return pltpu.CompilerParams(dimension_semantics=sem, vmem_limit_bytes=VMEM_LIMIT_BYTES)


def _ffn_kernel(*refs, n_mix, final_norm):
    h_ref = refs[0]
    y_refs = refs[1:1 + n_mix]
    wo_refs = refs[1 + n_mix:1 + 2 * n_mix]
    rest = refs[1 + 2 * n_mix:]
    g_ref, wg_ref, wu_ref, wd_ref = rest[:4]
    gf_ref = rest[4] if final_norm else None
    o_ref, xn_ref, acc_ref, act_ref = rest[-4:]

    h = h_ref[0]
    for y_ref, wo_ref in zip(y_refs, wo_refs):
        h = h + jnp.dot(y_ref[0], wo_ref[...], preferred_element_type=F32)
    xn_ref[...] = _rms(h, g_ref[...]).astype(BF16)
    if n_mix:
        o_ref[0] = h

    def hidden(c):
        cols = pl.ds(pl.multiple_of(c * FFN_FC, FFN_FC), FFN_FC)
        gate = jnp.dot(xn_ref[...], wg_ref[:, cols], preferred_element_type=F32)
        up = jnp.dot(xn_ref[...], wu_ref[:, cols], preferred_element_type=F32)
        return (gate * jax.nn.sigmoid(gate) * up).astype(BF16)

    def down(c):
        return wd_ref[pl.ds(pl.multiple_of(c * FFN_FC, FFN_FC), FFN_FC), :]

    act_ref[0] = hidden(0)
    act_ref[1] = hidden(1)
    acc_ref[...] = jnp.dot(act_ref[0], down(0), preferred_element_type=F32)

    def chunk_pair(pair, carry):
        c = 2 * pair + 1
        act_ref[0] = hidden(c + 1)
        acc_ref[...] += jnp.dot(act_ref[1], down(c), preferred_element_type=F32)
        act_ref[1] = hidden(c + 2)
        acc_ref[...] += jnp.dot(act_ref[0], down(c + 1), preferred_element_type=F32)
        return carry

    assert FFN_NC % 2 == 1 and FFN_NC >= 5
    lax.fori_loop(0, (FFN_NC - 3) // 2, chunk_pair, 0)
    act_ref[0] = hidden(FFN_NC - 1)
    tail = (jnp.dot(act_ref[1], down(FFN_NC - 2), preferred_element_type=F32)
            + jnp.dot(act_ref[0], down(FFN_NC - 1), preferred_element_type=F32))
    resid = o_ref[0] if n_mix else h_ref[0]
    out = resid + 0.5 * (acc_ref[...] + tail)
    o_ref[0] = _rms(out, gf_ref[...]) if final_norm else out


def _ffn(h, g, wg, wu, wd, which, mix=(), final_g=None):
    b, s, _ = h.shape
    tm = FFN_TM
    row = lambda width: pl.BlockSpec((1, tm, width), lambda bi, i: (bi, i, 0))
    const = lambda shape: pl.BlockSpec(shape, lambda bi, i: (0,) * len(shape), pipeline_mode=pl.Buffered(1))
    ys = [y for y, _ in mix]
    wos = [w for _, w in mix]
    in_specs = [row(D_MODEL)] + [row(y.shape[2]) for y in ys] + [const(w.shape) for w in wos]
    pick = lambda rows, cols: pl.BlockSpec((None, None, rows, cols), lambda bi, i: (*which, 0, 0),
                                            pipeline_mode=pl.Buffered(1))
    in_specs += [const((1, D_MODEL)), pick(D_MODEL, D_FF), pick(D_MODEL, D_FF), pick(D_FF, D_MODEL)]
    args = [h, *ys, *wos, g, wg, wu, wd]
    if final_g is not None:
        in_specs.append(const((1, D_MODEL)))
        args.append(final_g)
    return pl.pallas_call(
        functools.partial(_ffn_kernel, n_mix=len(mix), final_norm=final_g is not None),
        name="ffn",
        grid=(b, s // tm),
        in_specs=in_specs,
        out_specs=row(D_MODEL),
        out_shape=jax.ShapeDtypeStruct(h.shape, F32),
        scratch_shapes=[
            pltpu.VMEM((tm, D_MODEL), BF16),
            pltpu.VMEM((tm, D_MODEL), F32),
            pltpu.VMEM((2, tm, FFN_FC), BF16),
        ],
        compiler_params=_params("parallel", "parallel"),
    )(*args)


def _ab_in_kernel(h_ref, g_ref, wu_ref, wq_ref, wk_ref, wv_ref, wf_ref, bf_ref, tri_ref,
                  pqt_ref, pk_ref, wpool_ref, pscale_ref,
                  ypool_ref, q_ref, k_ref, v_ref, halo_ref, fcarry_ref):
    i = pl.program_id(1)
    tm = h_ref.shape[1]

    @pl.when(i == 0)
    def _():
        halo_ref[...] = jnp.zeros_like(halo_ref)
        fcarry_ref[...] = jnp.zeros_like(fcarry_ref)

    hn = _rms(h_ref[0], g_ref[...]).astype(BF16)

    nt = (((1,), (1,)), ((), ()))
    k_main = jnp.dot(hn, wk_ref[...], preferred_element_type=F32)
    qt = lax.dot_general(wq_ref[...], hn, nt, preferred_element_type=F32) * (FOX_HEAD_DIM ** -0.5 * LOG2E)
    vt = lax.dot_general(wv_ref[...], hn, nt, preferred_element_type=F32)
    u = jnp.dot(hn, wu_ref[...], preferred_element_type=F32)

    logit = jnp.dot(hn, wf_ref[...], preferred_element_type=F32) + bf_ref[...]
    log_f = jnp.minimum(logit, 0.0) - jnp.log1p(jnp.exp(-jnp.abs(logit)))
    tri = tri_ref[...]

    def split3(x):
        hi = x.astype(BF16)
        r1 = x - hi.astype(F32)
        mid = r1.astype(BF16)
        lo = (r1 - mid.astype(F32)).astype(BF16)
        return hi, mid, lo

    hi, mid, lo = split3(log_f)
    csum = (jnp.dot(tri, hi, preferred_element_type=F32)
            + jnp.dot(tri, mid, preferred_element_type=F32)
            + jnp.dot(tri, lo, preferred_element_type=F32))
    cum_f = csum + fcarry_ref[0:1, :]
    fcarry_ref[...] = jnp.broadcast_to(cum_f[tm - 1:tm, :], fcarry_ref.shape)

    fh, fm, fl = (x.astype(F32) for x in split3(cum_f * LOG2E))
    lane = lax.broadcasted_iota(jnp.int32, fh.shape, 1)
    xterms = jnp.where(lane < 8, fh, jnp.where(lane < 16, fm, jnp.where(
        lane < 24, fl, jnp.where(lane == 24, 1.0, 0.0)))).astype(BF16)

    k_ref[0] = (k_main + jnp.dot(xterms, pk_ref[...], preferred_element_type=F32)).astype(BF16)

    def head_rows(xt, filler):
        parts = []
        for hd in range(FOX_HEADS):
            parts += [xt[hd * FOX_HEAD_DIM:(hd + 1) * FOX_HEAD_DIM], filler]
        return jnp.concatenate(parts, axis=0)

    pad_rows = LANES - FOX_HEAD_DIM
    q_extra = lax.dot_general(pqt_ref[...], xterms, nt, preferred_element_type=F32)
    q_ref[0] = (head_rows(qt, jnp.zeros((pad_rows, tm), F32)) + q_extra).astype(BF16)
    ones_row = jnp.where(lax.broadcasted_iota(jnp.int32, (pad_rows, tm), 0) == 0, 1.0, 0.0)
    v_ref[0] = head_rows(vt, ones_row).astype(BF16)

    ext = jnp.concatenate([halo_ref[...], u], axis=0)
    halo_ref[...] = u[tm - POOL_HALO:, :]
    t_pos = i * tm + lax.broadcasted_iota(jnp.int32, (tm, POOL_GROUP), 0)
    sums = ext
    outs = []
    for g, w in enumerate(POOL_WINDOWS):
        sums = sums + pltpu.roll(sums, w // 2, axis=0)
        win = sums[POOL_HALO:, :POOL_GROUP]
        count = jnp.minimum(t_pos + 1, w).astype(F32)
        diff = win / count - u[:, g * POOL_GROUP:(g + 1) * POOL_GROUP]
        outs.append(jnp.dot(diff.astype(BF16), wpool_ref[g], preferred_element_type=F32))
        if g + 1 < len(POOL_WINDOWS):
            sums = sums[:, POOL_GROUP:]
    y = jnp.concatenate(outs, axis=-1) * pscale_ref[...]
    ypool_ref[0] = y.astype(BF16)


def _ab_in(h, g, w):
    b, s, _ = h.shape
    tm = PROJ_TM
    hw = FOX_HEADS * LANES
    const2 = lambda shape: pl.BlockSpec(shape, lambda bi, i: (0,) * len(shape))
    tok = lambda width: pl.BlockSpec((1, tm, width), lambda bi, i: (bi, i, 0))
    tok_t = lambda width: pl.BlockSpec((1, width, tm), lambda bi, i: (bi, 0, i))
    return pl.pallas_call(
        _ab_in_kernel,
        name="ab_in",
        grid=(b, s // tm),
        in_specs=[
            tok(D_MODEL), const2((1, D_MODEL)),
            const2((D_MODEL, POOL_WIDTH)), const2((FOX_WIDTH, D_MODEL)), const2((D_MODEL, hw)),
            const2((FOX_WIDTH, D_MODEL)), const2((D_MODEL, LANES)), const2((1, LANES)),
            const2((tm, tm)), const2((hw, LANES)), const2((LANES, hw)),
            const2((len(POOL_WINDOWS), POOL_GROUP, POOL_GROUP)), const2((1, POOL_WIDTH)),
        ],
        out_specs=[tok(POOL_WIDTH), tok_t(hw), tok(hw), tok_t(hw)],
        out_shape=[
            jax.ShapeDtypeStruct((b, s, POOL_WIDTH), BF16),
            jax.ShapeDtypeStruct((b, hw, s), BF16),
            jax.ShapeDtypeStruct((b, s, hw), BF16),
            jax.ShapeDtypeStruct((b, hw, s), BF16),
        ],
        scratch_shapes=[
            pltpu.VMEM((POOL_HALO, POOL_WIDTH), F32),
            pltpu.VMEM((SUBLANES, LANES), F32),
        ],
        compiler_params=_params("arbitrary", "arbitrary"),
    )(h, g, w["wu"], w["wq"], w["wk"], w["wv"], w["wf"], w["bf"], w["tri"],
      w["pqt"], w["pk"], w["wpool"], w["pscale"])


def _head_groups(w, heads, width):
    rows = w.shape[0]
    w = w.reshape(rows, heads, width)
    w = jnp.pad(w, ((0, 0), (0, 0), (0, LANES - width)))
    return w.reshape(rows, heads * LANES)


def _prep_ab(w_in, b_forget, w_pool, pool_scale, w_out):
    o1, o2, o3, o4 = POOL_WIDTH, POOL_WIDTH + FOX_WIDTH, POOL_WIDTH + 2 * FOX_WIDTH, POOL_WIDTH + 3 * FOX_WIDTH
    wb = w_in.astype(BF16)
    hw = FOX_HEADS * LANES
    wf = jnp.pad(jnp.tile(wb[:, o4:], (1, 3)), ((0, 0), (0, LANES - 3 * FOX_HEADS)))
    bf = jnp.pad(jnp.tile(b_forget.astype(F32), 3), (0, LANES - 3 * FOX_HEADS))[None, :]
    r = jnp.arange(LANES)[:, None]
    c = jnp.arange(hw)[None, :]
    head, lane = c // LANES, c % LANES
    is_term = r < 3 * FOX_HEADS
    pq = jnp.where(is_term & (head == r % FOX_HEADS) & (lane == FOX_HEAD_DIM + r // FOX_HEADS), 1.0, 0.0)
    pq = pq + jnp.where((r == 3 * FOX_HEADS) & (lane >= FOX_HEAD_DIM + 3) & (lane < FOX_HEAD_DIM + 6), 1.0, 0.0)
    pk = jnp.where(is_term & (head == r % FOX_HEADS) & (lane == FOX_HEAD_DIM + 3 + r // FOX_HEADS), -1.0, 0.0)
    pk = pk + jnp.where((r == 3 * FOX_HEADS) & (lane >= FOX_HEAD_DIM) & (lane < FOX_HEAD_DIM + 3), 1.0, 0.0)
    tri = jnp.tril(jnp.ones((PROJ_TM, PROJ_TM), BF16))
    wo = w_out.astype(BF16)
    return {
        "wu": wb[:, :o1],
        "wq": wb[:, o1:o2].T,
        "wk": _head_groups(wb[:, o2:o3], FOX_HEADS, FOX_HEAD_DIM),
        "wv": wb[:, o3:o4].T,
        "wf": wf, "bf": bf, "tri": tri,
        "pqt": pq.astype(BF16).T, "pk": pk.astype(BF16),
        "wpool": w_pool.astype(BF16), "pscale": pool_scale.astype(F32)[None, :],
        "wo_pool": wo[:POOL_WIDTH], "wo_fox": wo[POOL_WIDTH:],
    }


def _mla_in_kernel(h_ref, g_ref, win_ref, qn_ref, kvn_ref, wqa_ref, wqb_ref, wk_ref, wv_ref,
                   cos_ref, sin_ref, q_ref, k_ref, v_ref):
    hn = _rms(h_ref[0], g_ref[...]).astype(BF16)
    proj = jnp.dot(hn, win_ref[...], preferred_element_type=F32)
    c_q = proj[:, :MLA_Q_LORA]
    c_kv = proj[:, MLA_Q_LORA:MLA_Q_LORA + MLA_KV_LORA]
    kr_a = proj[:, MLA_Q_LORA + MLA_KV_LORA:MLA_Q_LORA + MLA_KV_LORA + LANES]
    kr_b = proj[:, MLA_Q_LORA + MLA_KV_LORA + LANES:]
    cos = cos_ref[0]
    sin = sin_ref[0]
    k_rope = kr_a * cos + kr_b * sin

    qn = _rms(c_q, qn_ref[...]).astype(BF16)
    kvn = _rms(c_kv, kvn_ref[...]).astype(BF16)

    k_all = jnp.dot(kvn, wk_ref[...], preferred_element_type=F32)
    for hd in range(MLA_HEADS):
        grp = slice(hd * LANES, (hd + 1) * LANES)
        k_ref[0, :, grp] = (k_all[:, grp] + k_rope).astype(BF16)

    nt = (((1,), (1,)), ((), ()))
    scale = (MLA_NOPE + MLA_ROPE) ** -0.5 * LOG2E
    qk_dim = MLA_NOPE + MLA_ROPE
    q_a = lax.dot_general(wqa_ref[...], qn, nt, preferred_element_type=F32)
    q_b = lax.dot_general(wqb_ref[...], qn, nt, preferred_element_type=F32)
    v_t = lax.dot_general(wv_ref[...], kvn, nt, preferred_element_type=F32)
    cos_t = cos.T[MLA_NOPE:qk_dim] * scale
    sin_t = sin.T[MLA_NOPE:qk_dim] * scale
    tm = cos.shape[0]
    q_pad = jnp.zeros((LANES - qk_dim, tm), BF16)
    v_pad = jnp.where(lax.broadcasted_iota(jnp.int32, (LANES - MLA_V, tm), 0) == 0, 1.0, 0.0).astype(BF16)
    for hd in range(MLA_HEADS):
        row = hd * LANES
        qa = q_a[hd * qk_dim:(hd + 1) * qk_dim]
        q_ref[0, row:row + MLA_NOPE, :] = (qa[:MLA_NOPE] * scale).astype(BF16)
        q_ref[0, row + MLA_NOPE:row + qk_dim, :] = (
            qa[MLA_NOPE:] * cos_t + q_b[hd * MLA_ROPE:(hd + 1) * MLA_ROPE] * sin_t).astype(BF16)
        q_ref[0, row + qk_dim:row + LANES, :] = q_pad
        v_ref[0, row:row + MLA_V, :] = v_t[hd * MLA_V:(hd + 1) * MLA_V].astype(BF16)
        v_ref[0, row + MLA_V:row + LANES, :] = v_pad


def _mla_in(h, g, w, cos_l, sin_l):
    b, s, _ = h.shape
    tm = PROJ_TM
    hw = MLA_HEADS * LANES
    nin = MLA_Q_LORA + MLA_KV_LORA + 2 * LANES
    const2 = lambda shape: pl.BlockSpec(shape, lambda bi, i: (0,) * len(shape))
    tok = lambda width: pl.BlockSpec((1, tm, width), lambda bi, i: (bi, i, 0))
    tok_t = lambda width: pl.BlockSpec((1, width, tm), lambda bi, i: (bi, 0, i))
    return pl.pallas_call(
        _mla_in_kernel,
        name="mla_in",
        grid=(b, s // tm),
        in_specs=[
            tok(D_MODEL), const2((1, D_MODEL)), const2((D_MODEL, nin)),
            const2((1, MLA_Q_LORA)), const2((1, MLA_KV_LORA)),
            const2((MLA_HEADS * (MLA_NOPE + MLA_ROPE), MLA_Q_LORA)), const2((MLA_HEADS * MLA_ROPE, MLA_Q_LORA)),
            const2((MLA_KV_LORA, hw)), const2((MLA_HEADS * MLA_V, MLA_KV_LORA)),
            tok(LANES), tok(LANES),
        ],
        out_specs=[tok_t(hw), tok(hw), tok_t(hw)],
        out_shape=[jax.ShapeDtypeStruct((b, hw, s), BF16), jax.ShapeDtypeStruct((b, s, hw), BF16),
                   jax.ShapeDtypeStruct((b, hw, s), BF16)],
        compiler_params=_params("parallel", "parallel"),
    )(h, g, w["win"], w["qn"], w["kvn"], w["wqa"], w["wqb"], w["wk"], w["wv"], cos_l, sin_l)


def _rope_group(x1, x2, lead):
    z0 = jnp.zeros(lead + (MLA_NOPE,), x1.dtype)
    z1 = jnp.zeros(lead + (LANES - MLA_NOPE - MLA_ROPE,), x1.dtype)
    return jnp.concatenate([z0, x1, x2, z1], axis=-1)


def _prep_mla(w_in, q_norm, kv_norm, w_q_b, w_kv_b, w_out):
    half = MLA_ROPE // 2
    wb = w_in.astype(BF16)
    kr = wb[:, MLA_Q_LORA + MLA_KV_LORA:]
    a1, a2 = kr[:, :half], kr[:, half:]
    win = jnp.concatenate([
        wb[:, :MLA_Q_LORA + MLA_KV_LORA],
        _rope_group(a1, a2, (D_MODEL,)),
        _rope_group(-a2, a1, (D_MODEL,)),
    ], axis=-1)
    wq = w_q_b.astype(BF16).reshape(MLA_Q_LORA, MLA_HEADS, MLA_NOPE + MLA_ROPE)
    x1, x2 = wq[..., MLA_NOPE:MLA_NOPE + half], wq[..., MLA_NOPE + half:]
    wqa = w_q_b.astype(BF16).T
    wqb = jnp.concatenate([-x2, x1], axis=-1).reshape(MLA_Q_LORA, MLA_HEADS * MLA_ROPE).T
    wkv = w_kv_b.astype(BF16).reshape(MLA_KV_LORA, MLA_HEADS, MLA_NOPE + MLA_V)
    pad = jnp.zeros((MLA_KV_LORA, MLA_HEADS, LANES - MLA_NOPE), BF16)
    wk = jnp.concatenate([wkv[..., :MLA_NOPE], pad], axis=-1).reshape(MLA_KV_LORA, MLA_HEADS * LANES)
    wv = wkv[..., MLA_NOPE:].reshape(MLA_KV_LORA, MLA_HEADS * MLA_V).T
    return {
        "win": win, "qn": q_norm.astype(F32)[None, :], "kvn": kv_norm.astype(F32)[None, :],
        "wqa": wqa, "wqb": wqb, "wk": wk, "wv": wv, "wo": w_out.astype(BF16),
    }


def _attn_kernel(qt_ref, k_ref, vt_ref, o_ref, m_ref, acc_ref, s_ref, smax_ref, *, tq, tk, chunk, head_dim):
    i = pl.program_id(2)
    shift = chunk.bit_length() - 1
    tiles_per_block = tq // tk

    m_ref[...] = jnp.full(m_ref.shape, MASKED, F32)
    acc_ref[...] = jnp.zeros_like(acc_ref)
    groups = [slice(hh * LANES, (hh + 1) * LANES) for hh in range(HEADS_PER_STEP)]

    units = [(hh, slice(c, c + ATT_COLS)) for hh in range(HEADS_PER_STEP) for c in range(0, tq, ATT_COLS)]

    def scores_into(j, slot, hh, cols):
        off = pl.multiple_of(j * tk, tk)
        grp = groups[hh]
        s = jnp.dot(k_ref[0, pl.ds(off, tk), grp], qt_ref[0, grp, cols],
                    preferred_element_type=F32)
        s_ref[slot, hh, :, cols] = s
        smax_ref[slot, hh, :, cols] = jnp.broadcast_to(jnp.max(s, axis=0, keepdims=True),
                                                       (SUBLANES, s.shape[1]))

    def softmax_pv(j, slot, hh, cols, diag=None):
        off = pl.multiple_of(j * tk, tk)
        s = s_ref[slot, hh, :, cols]
        vt = vt_ref[0, hh * LANES:hh * LANES + V_ROWS, pl.ds(off, tk)]
        if diag is not None:
            key = lax.broadcasted_iota(jnp.int32, s.shape, 0) + diag * tk
            qry = lax.broadcasted_iota(jnp.int32, s.shape, 1) + cols.start
            s = jnp.where((key >> shift) <= (qry >> shift), s, MASKED)
        m_prev = m_ref[hh, 0:1, cols]
        s_max = smax_ref[slot, hh, 0:1, cols] if diag is None else jnp.max(s, axis=0, keepdims=True)
        m_new = jnp.maximum(m_prev, s_max)
        alpha = jnp.exp2(m_prev - m_new)
        p = jnp.exp2(s - m_new)
        acc_ref[hh, :, cols] = acc_ref[hh, :, cols] * alpha + jnp.dot(
            vt, p.astype(BF16), preferred_element_type=F32)
        m_ref[hh, :, cols] = jnp.broadcast_to(m_new, (SUBLANES, m_new.shape[1]))

    first_diag = i * tiles_per_block
    for hh, cols in units:
        scores_into(0, 0, hh, cols)

    def tile_pair(j):
        for hh, cols in units:
            scores_into(j + 1, 1, hh, cols)
            softmax_pv(j, 0, hh, cols)
        for hh, cols in units:
            scores_into(j + 2, 0, hh, cols)
            softmax_pv(j + 1, 1, hh, cols)

    def tile_quad(quad, carry):
        tile_pair(4 * quad)
        tile_pair(4 * quad + 2)
        return carry

    n_pairs = first_diag // 2
    lax.fori_loop(0, n_pairs // 2, tile_quad, 0)

    if (tiles_per_block // 2) % 2:
        @pl.when(n_pairs % 2 == 1)
        def _():
            tile_pair(first_diag - 2)

    for d in range(tiles_per_block):
        for hh, cols in units:
            if d + 1 < tiles_per_block and cols.start >= (d + 1) * tk:
                scores_into(first_diag + d + 1, (d + 1) % 2, hh, cols)
            if cols.start >= (d + 1) * tk:
                softmax_pv(first_diag + d, d % 2, hh, cols)
            elif cols.start >= d * tk:
                softmax_pv(first_diag + d, d % 2, hh, cols, diag=d)

    outs = []
    for hh in range(HEADS_PER_STEP):
        acc = acc_ref[hh]
        outs.append(acc[:head_dim, :] / acc[head_dim:head_dim + 1, :])
    o_ref[0] = jnp.concatenate(outs, axis=0).T.astype(BF16)


def _attention(qt, k, vt, heads, chunk, head_dim):
    b, s, _ = k.shape
    tq, tk = ATT_TQ, ATT_TK
    assert tq % (2 * tk) == 0 and tk % chunk == 0 and s % tq == 0 and heads % HEADS_PER_STEP == 0
    assert tk % ATT_COLS == 0 and tq % ATT_COLS == 0
    assert (HEADS_PER_STEP * head_dim) % LANES == 0
    gw = HEADS_PER_STEP * LANES
    return pl.pallas_call(
        functools.partial(_attn_kernel, tq=tq, tk=tk, chunk=chunk, head_dim=head_dim),
        name="attention",
        grid=(b, heads // HEADS_PER_STEP, s // tq),
        in_specs=[
            pl.BlockSpec((1, gw, tq), lambda bi, hp, i: (bi, hp, i)),
            pl.BlockSpec((1, s, gw), lambda bi, hp, i: (bi, 0, hp)),
            pl.BlockSpec((1, gw, s), lambda bi, hp, i: (bi, hp, 0), pipeline_mode=pl.Buffered(1)),
        ],
        out_specs=pl.BlockSpec((1, tq, HEADS_PER_STEP * head_dim), lambda bi, hp, i: (bi, i, hp)),
        out_shape=jax.ShapeDtypeStruct((b, s, heads * head_dim), BF16),
        scratch_shapes=[
            pltpu.VMEM((HEADS_PER_STEP, SUBLANES, tq), F32),
            pltpu.VMEM((HEADS_PER_STEP, V_ROWS, tq), F32),
            pltpu.VMEM((2, HEADS_PER_STEP, tk, tq), F32),
            pltpu.VMEM((2, HEADS_PER_STEP, SUBLANES, tq), F32),
        ],
        compiler_params=_params("parallel", "parallel", "arbitrary"),
    )(qt, k, vt)


def _rope_lane_tables(positions):
    inv_freq = ROPE_THETA ** (-jnp.arange(0, MLA_ROPE, 2, dtype=F32) / MLA_ROPE)
    ang = positions.astype(F32)[..., None] * inv_freq
    cos, sin = jnp.cos(ang), jnp.sin(ang)
    lead = positions.shape
    return _rope_group(cos, cos, lead), _rope_group(sin, sin, lead)


def kernel(x, positions, norm_ffn, norm_mix, norm_final, ffn_w_gate, ffn_w_up, ffn_w_down,
           ab_w_in, ab_b_forget, pool_w, pool_scale, ab_w_out,
           mla_w_in, mla_q_norm, mla_kv_norm, mla_w_q_b, mla_w_kv_b, mla_w_out):
    cos_l, sin_l = _rope_lane_tables(positions)
    w_gate, w_up, w_down = (w.astype(BF16) for w in (ffn_w_gate, ffn_w_up, ffn_w_down))
    h = x.astype(F32)
    for layer in range(DEPTH):
        idx = layer // 2
        h = _ffn(h, norm_ffn[layer, 0][None, :], w_gate, w_up, w_down, (layer, 0))
        g_mix = norm_mix[layer][None, :]
        if layer % 2 == 0:
            w = _prep_ab(ab_w_in[idx], ab_b_forget[idx], pool_w[idx], pool_scale[idx], ab_w_out[idx])
            y_pool, q, k, v = _ab_in(h, g_mix, w)
            y_fox = _attention(q, k, v, FOX_HEADS, 1, FOX_HEAD_DIM)
            mix = ((y_pool, w["wo_pool"]), (y_fox, w["wo_fox"]))
        else:
            w = _prep_mla(mla_w_in[idx], mla_q_norm[idx], mla_kv_norm[idx], mla_w_q_b[idx],
                          mla_w_kv_b[idx], mla_w_out[idx])
            q, k, v = _mla_in(h, g_mix, w, cos_l, sin_l)
            y = _attention(q, k, v, MLA_HEADS, CHUNK, MLA_V)
            mix = ((y, w["wo"]),)
        h = _ffn(h, norm_ffn[layer, 1][None, :], w_gate, w_up, w_down, (layer, 1), mix=mix,
                 final_g=norm_final[None, :] if layer == DEPTH - 1 else None)
    return h
```

```python
import functools
import math

import jax
import jax.numpy as jnp
from jax import lax
from jax.experimental import pallas as pl
from jax.experimental.pallas import tpu as pltpu

F32 = jnp.float32
BF16 = jnp.bfloat16

D_MODEL = 1024
DEPTH = 4
CHUNK = 64
RMS_EPS = 1e-6
D_FF = 2816
POOL_WINDOWS = (2, 4, 8, 16)
POOL_GROUP = 128
POOL_WIDTH = 512
FOX_HEADS = 8
FOX_HEAD_DIM = 64
FOX_WIDTH = 512
MLA_HEADS = 16
MLA_NOPE = 64
MLA_ROPE = 32
MLA_V = 64
MLA_Q_LORA = 256
MLA_KV_LORA = 128
ROPE_THETA = 10000.0

LANES = 128
SUBLANES = 8
VMEM_LIMIT_BYTES = 56 * 1024 * 1024

FFN_TM = 1024
FFN_FC = 256
FFN_NC = D_FF // FFN_FC
PROJ_TM = 512
ATT_TQ = 2048
ATT_TK = 512
HEADS_PER_STEP = 2
ATT_COLS = 256
V_ROWS = 80
POOL_HALO = 16

LOG2E = math.log2(math.e)
MASKED = -1e30


def _rms(x, g):
    return x * lax.rsqrt(jnp.mean(x * x, axis=-1, keepdims=True) + RMS_EPS) * g


def _params(*sem):
    return pltpu.CompilerParams(dimension_semantics=sem, vmem_limit_bytes=VMEM_LIMIT_BYTES)


def _ffn_kernel(*refs, n_mix, final_norm):
    h_ref = refs[0]
    y_refs = refs[1:1 + n_mix]
    wo_refs = refs[1 + n_mix:1 + 2 * n_mix]
    rest = refs[1 + 2 * n_mix:]
    g_ref, wg_ref, wu_ref, wd_ref = rest[:4]
    gf_ref = rest[4] if final_norm else None
    o_ref, xn_ref, acc_ref, act_ref = rest[-4:]

    h = h_ref[0]
    for y_ref, wo_ref in zip(y_refs, wo_refs):
        h = h + jnp.dot(y_ref[0], wo_ref[...], preferred_element_type=F32)
    xn_ref[...] = _rms(h, g_ref[...]).astype(BF16)
    if n_mix:
        o_ref[0] = h

    def hidden(c):
        cols = pl.ds(pl.multiple_of(c * FFN_FC, FFN_FC), FFN_FC)
        gate = jnp.dot(xn_ref[...], wg_ref[:, cols], preferred_element_type=F32)
        up = jnp.dot(xn_ref[...], wu_ref[:, cols], preferred_element_type=F32)
        return (gate * jax.nn.sigmoid(gate) * up).astype(BF16)

    def down(c):
        return wd_ref[pl.ds(pl.multiple_of(c * FFN_FC, FFN_FC), FFN_FC), :]

    act_ref[0] = hidden(0)
    act_ref[1] = hidden(1)
    acc_ref[...] = jnp.dot(act_ref[0], down(0), preferred_element_type=F32)

    def chunk_pair(pair, carry):
        c = 2 * pair + 1
        act_ref[0] = hidden(c + 1)
        acc_ref[...] += jnp.dot(act_ref[1], down(c), preferred_element_type=F32)
        act_ref[1] = hidden(c + 2)
        acc_ref[...] += jnp.dot(act_ref[0], down(c + 1), preferred_element_type=F32)
        return carry

    assert FFN_NC % 2 == 1 and FFN_NC >= 5
    lax.fori_loop(0, (FFN_NC - 3) // 2, chunk_pair, 0)
    act_ref[0] = hidden(FFN_NC - 1)
    tail = (jnp.dot(act_ref[1], down(FFN_NC - 2), preferred_element_type=F32)
            + jnp.dot(act_ref[0], down(FFN_NC - 1), preferred_element_type=F32))
    resid = o_ref[0] if n_mix else h_ref[0]
    out = resid + 0.5 * (acc_ref[...] + tail)
    o_ref[0] = _rms(out, gf_ref[...]) if final_norm else out


def _ffn(h, g, wg, wu, wd, which, mix=(), final_g=None):
    b, s, _ = h.shape
    tm = FFN_TM
    row = lambda width: pl.BlockSpec((1, tm, width), lambda bi, i: (bi, i, 0))
    const = lambda shape: pl.BlockSpec(shape, lambda bi, i: (0,) * len(shape), pipeline_mode=pl.Buffered(1))
    ys = [y for y, _ in mix]
    wos = [w for _, w in mix]
    in_specs = [row(D_MODEL)] + [row(y.shape[2]) for y in ys] + [const(w.shape) for w in wos]
    pick = lambda rows, cols: pl.BlockSpec((None, None, rows, cols), lambda bi, i: (*which, 0, 0),
                                            pipeline_mode=pl.Buffered(1))
    in_specs += [const((1, D_MODEL)), pick(D_MODEL, D_FF), pick(D_MODEL, D_FF), pick(D_FF, D_MODEL)]
    args = [h, *ys, *wos, g, wg, wu, wd]
    if final_g is not None:
        in_specs.append(const((1, D_MODEL)))
        args.append(final_g)
    return pl.pallas_call(
        functools.partial(_ffn_kernel, n_mix=len(mix), final_norm=final_g is not None),
        name="ffn",
        grid=(b, s // tm),
        in_specs=in_specs,
        out_specs=row(D_MODEL),
        out_shape=jax.ShapeDtypeStruct(h.shape, F32),
        scratch_shapes=[
            pltpu.VMEM((tm, D_MODEL), BF16),
            pltpu.VMEM((tm, D_MODEL), F32),
            pltpu.VMEM((2, tm, FFN_FC), BF16),
        ],
        compiler_params=_params("parallel", "parallel"),
    )(*args)


def _ab_in_kernel(h_ref, g_ref, wu_ref, wq_ref, wk_ref, wv_ref, wf_ref, bf_ref, tri_ref,
                  pqt_ref, pk_ref, wpool_ref, pscale_ref,
                  ypool_ref, q_ref, k_ref, v_ref, halo_ref, fcarry_ref):
    i = pl.program_id(1)
    tm = h_ref.shape[1]

    @pl.when(i == 0)
    def _():
        halo_ref[...] = jnp.zeros_like(halo_ref)
        fcarry_ref[...] = jnp.zeros_like(fcarry_ref)

    hn = _rms(h_ref[0], g_ref[...]).astype(BF16)

    nt = (((1,), (1,)), ((), ()))
    k_main = jnp.dot(hn, wk_ref[...], preferred_element_type=F32)
    qt = lax.dot_general(wq_ref[...], hn, nt, preferred_element_type=F32) * (FOX_HEAD_DIM ** -0.5 * LOG2E)
    vt = lax.dot_general(wv_ref[...], hn, nt, preferred_element_type=F32)
    u = jnp.dot(hn, wu_ref[...], preferred_element_type=F32)

    logit = jnp.dot(hn, wf_ref[...], preferred_element_type=F32) + bf_ref[...]
    log_f = jnp.minimum(logit, 0.0) - jnp.log1p(jnp.exp(-jnp.abs(logit)))
    tri = tri_ref[...]

    def split3(x):
        hi = x.astype(BF16)
        r1 = x - hi.astype(F32)
        mid = r1.astype(BF16)
        lo = (r1 - mid.astype(F32)).astype(BF16)
        return hi, mid, lo

    hi, mid, lo = split3(log_f)
    csum = (jnp.dot(tri, hi, preferred_element_type=F32)
            + jnp.dot(tri, mid, preferred_element_type=F32)
            + jnp.dot(tri, lo, preferred_element_type=F32))
    cum_f = csum + fcarry_ref[0:1, :]
    fcarry_ref[...] = jnp.broadcast_to(cum_f[tm - 1:tm, :], fcarry_ref.shape)

    fh, fm, fl = (x.astype(F32) for x in split3(cum_f * LOG2E))
    lane = lax.broadcasted_iota(jnp.int32, fh.shape, 1)
    nh = FOX_HEADS
    xterms = jnp.where(lane < nh, fh, jnp.where(lane < 2 * nh, fm, jnp.where(
        lane < 3 * nh, fl, jnp.where(lane == 3 * nh, 1.0, 0.0)))).astype(BF16)

    k_ref[0] = (k_main + jnp.dot(xterms, pk_ref[...], preferred_element_type=F32)).astype(BF16)

    def head_rows(xt, filler):
        parts = []
        for hd in range(FOX_HEADS):
            parts += [xt[hd * FOX_HEAD_DIM:(hd + 1) * FOX_HEAD_DIM], filler]
        return jnp.concatenate(parts, axis=0)

    pad_rows = LANES - FOX_HEAD_DIM
    q_extra = lax.dot_general(pqt_ref[...], xterms, nt, preferred_element_type=F32)
    q_ref[0] = (head_rows(qt, jnp.zeros((pad_rows, tm), F32)) + q_extra).astype(BF16)
    ones_row = jnp.where(lax.broadcasted_iota(jnp.int32, (pad_rows, tm), 0) == 0, 1.0, 0.0)
    v_ref[0] = head_rows(vt, ones_row).astype(BF16)

    ext = jnp.concatenate([halo_ref[...], u], axis=0)
    halo_ref[...] = u[tm - POOL_HALO:, :]
    t_pos = i * tm + lax.broadcasted_iota(jnp.int32, (tm, POOL_GROUP), 0)
    sums = ext
    outs = []
    for g, w in enumerate(POOL_WINDOWS):
        sums = sums + pltpu.roll(sums, w // 2, axis=0)
        win = sums[POOL_HALO:, :POOL_GROUP]
        count = jnp.minimum(t_pos + 1, w).astype(F32)
        diff = win / count - u[:, g * POOL_GROUP:(g + 1) * POOL_GROUP]
        outs.append(jnp.dot(diff.astype(BF16), wpool_ref[g], preferred_element_type=F32))
        if g + 1 < len(POOL_WINDOWS):
            sums = sums[:, POOL_GROUP:]
    y = jnp.concatenate(outs, axis=-1) * pscale_ref[...]
    ypool_ref[0] = y.astype(BF16)


def _ab_in(h, g, w):
    b, s, _ = h.shape
    tm = PROJ_TM
    hw = FOX_HEADS * LANES
    const2 = lambda shape: pl.BlockSpec(shape, lambda bi, i: (0,) * len(shape))
    tok = lambda width: pl.BlockSpec((1, tm, width), lambda bi, i: (bi, i, 0))
    tok_t = lambda width: pl.BlockSpec((1, width, tm), lambda bi, i: (bi, 0, i))
    return pl.pallas_call(
        _ab_in_kernel,
        name="ab_in",
        grid=(b, s // tm),
        in_specs=[
            tok(D_MODEL), const2((1, D_MODEL)),
            const2((D_MODEL, POOL_WIDTH)), const2((FOX_WIDTH, D_MODEL)), const2((D_MODEL, hw)),
            const2((FOX_WIDTH, D_MODEL)), const2((D_MODEL, LANES)), const2((1, LANES)),
            const2((tm, tm)), const2((hw, LANES)), const2((LANES, hw)),
            const2((len(POOL_WINDOWS), POOL_GROUP, POOL_GROUP)), const2((1, POOL_WIDTH)),
        ],
        out_specs=[tok(POOL_WIDTH), tok_t(hw), tok(hw), tok_t(hw)],
        out_shape=[
            jax.ShapeDtypeStruct((b, s, POOL_WIDTH), BF16),
            jax.ShapeDtypeStruct((b, hw, s), BF16),
            jax.ShapeDtypeStruct((b, s, hw), BF16),
            jax.ShapeDtypeStruct((b, hw, s), BF16),
        ],
        scratch_shapes=[
            pltpu.VMEM((POOL_HALO, POOL_WIDTH), F32),
            pltpu.VMEM((SUBLANES, LANES), F32),
        ],
        compiler_params=_params("arbitrary", "arbitrary"),
    )(h, g, w["wu"], w["wq"], w["wk"], w["wv"], w["wf"], w["bf"], w["tri"],
      w["pqt"], w["pk"], w["wpool"], w["pscale"])


def _head_groups(w, heads, width):
    rows = w.shape[0]
    w = w.reshape(rows, heads, width)
    w = jnp.pad(w, ((0, 0), (0, 0), (0, LANES - width)))
    return w.reshape(rows, heads * LANES)


def _prep_ab(w_in, b_forget, w_pool, pool_scale, w_out):
    o1, o2, o3, o4 = POOL_WIDTH, POOL_WIDTH + FOX_WIDTH, POOL_WIDTH + 2 * FOX_WIDTH, POOL_WIDTH + 3 * FOX_WIDTH
    wb = w_in.astype(BF16)
    hw = FOX_HEADS * LANES
    wf = jnp.pad(jnp.tile(wb[:, o4:], (1, 3)), ((0, 0), (0, LANES - 3 * FOX_HEADS)))
    bf = jnp.pad(jnp.tile(b_forget.astype(F32), 3), (0, LANES - 3 * FOX_HEADS))[None, :]
    r = jnp.arange(LANES)[:, None]
    c = jnp.arange(hw)[None, :]
    head, lane = c // LANES, c % LANES
    is_term = r < 3 * FOX_HEADS
    pq = jnp.where(is_term & (head == r % FOX_HEADS) & (lane == FOX_HEAD_DIM + r // FOX_HEADS), 1.0, 0.0)
    pq = pq + jnp.where((r == 3 * FOX_HEADS) & (lane >= FOX_HEAD_DIM + 3) & (lane < FOX_HEAD_DIM + 6), 1.0, 0.0)
    pk = jnp.where(is_term & (head == r % FOX_HEADS) & (lane == FOX_HEAD_DIM + 3 + r // FOX_HEADS), -1.0, 0.0)
    pk = pk + jnp.where((r == 3 * FOX_HEADS) & (lane >= FOX_HEAD_DIM) & (lane < FOX_HEAD_DIM + 3), 1.0, 0.0)
    tri = jnp.tril(jnp.ones((PROJ_TM, PROJ_TM), BF16))
    wo = w_out.astype(BF16)
    return {
        "wu": wb[:, :o1],
        "wq": wb[:, o1:o2].T,
        "wk": _head_groups(wb[:, o2:o3], FOX_HEADS, FOX_HEAD_DIM),
        "wv": wb[:, o3:o4].T,
        "wf": wf, "bf": bf, "tri": tri,
        "pqt": pq.astype(BF16).T, "pk": pk.astype(BF16),
        "wpool": w_pool.astype(BF16), "pscale": pool_scale.astype(F32)[None, :],
        "wo_pool": wo[:POOL_WIDTH], "wo_fox": wo[POOL_WIDTH:],
    }


def _mla_in_kernel(h_ref, g_ref, win_ref, qn_ref, kvn_ref, wqa_ref, wqb_ref, wk_ref, wv_ref,
                   cos_ref, sin_ref, q_ref, k_ref, v_ref):
    hn = _rms(h_ref[0], g_ref[...]).astype(BF16)
    proj = jnp.dot(hn, win_ref[...], preferred_element_type=F32)
    c_q = proj[:, :MLA_Q_LORA]
    c_kv = proj[:, MLA_Q_LORA:MLA_Q_LORA + MLA_KV_LORA]
    kr_a = proj[:, MLA_Q_LORA + MLA_KV_LORA:MLA_Q_LORA + MLA_KV_LORA + LANES]
    kr_b = proj[:, MLA_Q_LORA + MLA_KV_LORA + LANES:]
    cos = cos_ref[0]
    sin = sin_ref[0]
    k_rope = kr_a * cos + kr_b * sin

    qn = _rms(c_q, qn_ref[...]).astype(BF16)
    kvn = _rms(c_kv, kvn_ref[...]).astype(BF16)

    k_all = jnp.dot(kvn, wk_ref[...], preferred_element_type=F32)
    for hd in range(MLA_HEADS):
        grp = slice(hd * LANES, (hd + 1) * LANES)
        k_ref[0, :, grp] = (k_all[:, grp] + k_rope).astype(BF16)

    nt = (((1,), (1,)), ((), ()))
    scale = (MLA_NOPE + MLA_ROPE) ** -0.5 * LOG2E
    qk_dim = MLA_NOPE + MLA_ROPE
    q_a = lax.dot_general(wqa_ref[...], qn, nt, preferred_element_type=F32)
    q_b = lax.dot_general(wqb_ref[...], qn, nt, preferred_element_type=F32)
    v_t = lax.dot_general(wv_ref[...], kvn, nt, preferred_element_type=F32)
    cos_t = cos.T[MLA_NOPE:qk_dim] * scale
    sin_t = sin.T[MLA_NOPE:qk_dim] * scale
    tm = cos.shape[0]
    q_pad = jnp.zeros((LANES - qk_dim, tm), BF16)
    v_pad = jnp.where(lax.broadcasted_iota(jnp.int32, (LANES - MLA_V, tm), 0) == 0, 1.0, 0.0).astype(BF16)
    for hd in range(MLA_HEADS):
        row = hd * LANES
        qa = q_a[hd * qk_dim:(hd + 1) * qk_dim]
        q_ref[0, row:row + MLA_NOPE, :] = (qa[:MLA_NOPE] * scale).astype(BF16)
        q_ref[0, row + MLA_NOPE:row + qk_dim, :] = (
            qa[MLA_NOPE:] * cos_t + q_b[hd * MLA_ROPE:(hd + 1) * MLA_ROPE] * sin_t).astype(BF16)
        q_ref[0, row + qk_dim:row + LANES, :] = q_pad
        v_ref[0, row:row + MLA_V, :] = v_t[hd * MLA_V:(hd + 1) * MLA_V].astype(BF16)
        v_ref[0, row + MLA_V:row + LANES, :] = v_pad


def _mla_in(h, g, w, cos_l, sin_l):
    b, s, _ = h.shape
    tm = PROJ_TM
    hw = MLA_HEADS * LANES
    nin = MLA_Q_LORA + MLA_KV_LORA + 2 * LANES
    const2 = lambda shape: pl.BlockSpec(shape, lambda bi, i: (0,) * len(shape))
    tok = lambda width: pl.BlockSpec((1, tm, width), lambda bi, i: (bi, i, 0))
    tok_t = lambda width: pl.BlockSpec((1, width, tm), lambda bi, i: (bi, 0, i))
    return pl.pallas_call(
        _mla_in_kernel,
        name="mla_in",
        grid=(b, s // tm),
        in_specs=[
            tok(D_MODEL), const2((1, D_MODEL)), const2((D_MODEL, nin)),
            const2((1, MLA_Q_LORA)), const2((1, MLA_KV_LORA)),
            const2((MLA_HEADS * (MLA_NOPE + MLA_ROPE), MLA_Q_LORA)), const2((MLA_HEADS * MLA_ROPE, MLA_Q_LORA)),
            const2((MLA_KV_LORA, hw)), const2((MLA_HEADS * MLA_V, MLA_KV_LORA)),
            tok(LANES), tok(LANES),
        ],
        out_specs=[tok_t(hw), tok(hw), tok_t(hw)],
        out_shape=[jax.ShapeDtypeStruct((b, hw, s), BF16), jax.ShapeDtypeStruct((b, s, hw), BF16),
                   jax.ShapeDtypeStruct((b, hw, s), BF16)],
        compiler_params=_params("parallel", "parallel"),
    )(h, g, w["win"], w["qn"], w["kvn"], w["wqa"], w["wqb"], w["wk"], w["wv"], cos_l, sin_l)


def _rope_group(x1, x2, lead):
    z0 = jnp.zeros(lead + (MLA_NOPE,), x1.dtype)
    z1 = jnp.zeros(lead + (LANES - MLA_NOPE - MLA_ROPE,), x1.dtype)
    return jnp.concatenate([z0, x1, x2, z1], axis=-1)


def _prep_mla(w_in, q_norm, kv_norm, w_q_b, w_kv_b, w_out):
    half = MLA_ROPE // 2
    wb = w_in.astype(BF16)
    kr = wb[:, MLA_Q_LORA + MLA_KV_LORA:]
    a1, a2 = kr[:, :half], kr[:, half:]
    win = jnp.concatenate([
        wb[:, :MLA_Q_LORA + MLA_KV_LORA],
        _rope_group(a1, a2, (D_MODEL,)),
        _rope_group(-a2, a1, (D_MODEL,)),
    ], axis=-1)
    wq = w_q_b.astype(BF16).reshape(MLA_Q_LORA, MLA_HEADS, MLA_NOPE + MLA_ROPE)
    x1, x2 = wq[..., MLA_NOPE:MLA_NOPE + half], wq[..., MLA_NOPE + half:]
    wqa = w_q_b.astype(BF16).T
    wqb = jnp.concatenate([-x2, x1], axis=-1).reshape(MLA_Q_LORA, MLA_HEADS * MLA_ROPE).T
    wkv = w_kv_b.astype(BF16).reshape(MLA_KV_LORA, MLA_HEADS, MLA_NOPE + MLA_V)
    pad = jnp.zeros((MLA_KV_LORA, MLA_HEADS, LANES - MLA_NOPE), BF16)
    wk = jnp.concatenate([wkv[..., :MLA_NOPE], pad], axis=-1).reshape(MLA_KV_LORA, MLA_HEADS * LANES)
    wv = wkv[..., MLA_NOPE:].reshape(MLA_KV_LORA, MLA_HEADS * MLA_V).T
    return {
        "win": win, "qn": q_norm.astype(F32)[None, :], "kvn": kv_norm.astype(F32)[None, :],
        "wqa": wqa, "wqb": wqb, "wk": wk, "wv": wv, "wo": w_out.astype(BF16),
    }


def _attn_kernel(qt_ref, k_ref, vt_ref, o_ref, m_ref, acc_ref, s_ref, smax_ref, *, tq, tk, chunk, head_dim):
    i = pl.program_id(2)
    shift = chunk.bit_length() - 1
    tiles_per_block = tq // tk

    m_ref[...] = jnp.full(m_ref.shape, MASKED, F32)
    acc_ref[...] = jnp.zeros_like(acc_ref)
    groups = [slice(hh * LANES, (hh + 1) * LANES) for hh in range(HEADS_PER_STEP)]

    units = [(hh, slice(c, c + ATT_COLS)) for hh in range(HEADS_PER_STEP) for c in range(0, tq, ATT_COLS)]

    def scores_into(j, slot, hh, cols):
        off = pl.multiple_of(j * tk, tk)
        grp = groups[hh]
        s = jnp.dot(k_ref[0, pl.ds(off, tk), grp], qt_ref[0, grp, cols],
                    preferred_element_type=F32)
        s_ref[slot, hh, :, cols] = s
        smax_ref[slot, hh, :, cols] = jnp.broadcast_to(jnp.max(s, axis=0, keepdims=True),
                                                       (SUBLANES, s.shape[1]))

    def softmax_pv(j, slot, hh, cols, diag=None):
        off = pl.multiple_of(j * tk, tk)
        s = s_ref[slot, hh, :, cols]
        vt = vt_ref[0, hh * LANES:hh * LANES + V_ROWS, pl.ds(off, tk)]
        if diag is not None:
            key = lax.broadcasted_iota(jnp.int32, s.shape, 0) + diag * tk
            qry = lax.broadcasted_iota(jnp.int32, s.shape, 1) + cols.start
            s = jnp.where((key >> shift) <= (qry >> shift), s, MASKED)
        m_prev = m_ref[hh, 0:1, cols]
        s_max = smax_ref[slot, hh, 0:1, cols] if diag is None else jnp.max(s, axis=0, keepdims=True)
        m_new = jnp.maximum(m_prev, s_max)
        alpha = jnp.exp2(m_prev - m_new)
        p = jnp.exp2(s - m_new)
        acc_ref[hh, :, cols] = acc_ref[hh, :, cols] * alpha + jnp.dot(
            vt, p.astype(BF16), preferred_element_type=F32)
        m_ref[hh, :, cols] = jnp.broadcast_to(m_new, (SUBLANES, m_new.shape[1]))

    first_diag = i * tiles_per_block
    for hh, cols in units:
        scores_into(0, 0, hh, cols)

    def tile_pair(j):
        for hh, cols in units:
            scores_into(j + 1, 1, hh, cols)
            softmax_pv(j, 0, hh, cols)
        for hh, cols in units:
            scores_into(j + 2, 0, hh, cols)
            softmax_pv(j + 1, 1, hh, cols)

    def tile_quad(quad, carry):
        tile_pair(4 * quad)
        tile_pair(4 * quad + 2)
        return carry

    n_pairs = first_diag // 2
    lax.fori_loop(0, n_pairs // 2, tile_quad, 0)

    if (tiles_per_block // 2) % 2:
        @pl.when(n_pairs % 2 == 1)
        def _():
            tile_pair(first_diag - 2)

    for d in range(tiles_per_block):
        for hh, cols in units:
            if d + 1 < tiles_per_block and cols.start >= (d + 1) * tk:
                scores_into(first_diag + d + 1, (d + 1) % 2, hh, cols)
            if cols.start >= (d + 1) * tk:
                softmax_pv(first_diag + d, d % 2, hh, cols)
            elif cols.start >= d * tk:
                softmax_pv(first_diag + d, d % 2, hh, cols, diag=d)

    outs = []
    for hh in range(HEADS_PER_STEP):
        acc = acc_ref[hh]
        outs.append(acc[:head_dim, :] / acc[head_dim:head_dim + 1, :])
    o_ref[0] = jnp.concatenate(outs, axis=0).T.astype(BF16)


def _attention(qt, k, vt, heads, chunk, head_dim):
    b, s, _ = k.shape
    tq, tk = ATT_TQ, ATT_TK
    assert tq % (2 * tk) == 0 and tk % chunk == 0 and s % tq == 0 and heads % HEADS_PER_STEP == 0
    assert tk % ATT_COLS == 0 and tq % ATT_COLS == 0
    assert (HEADS_PER_STEP * head_dim) % LANES == 0
    gw = HEADS_PER_STEP * LANES
    return pl.pallas_call(
        functools.partial(_attn_kernel, tq=tq, tk=tk, chunk=chunk, head_dim=head_dim),
        name="attention",
        grid=(b, heads // HEADS_PER_STEP, s // tq),
        in_specs=[
            pl.BlockSpec((1, gw, tq), lambda bi, hp, i: (bi, hp, i)),
            pl.BlockSpec((1, s, gw), lambda bi, hp, i: (bi, 0, hp)),
            pl.BlockSpec((1, gw, s), lambda bi, hp, i: (bi, hp, 0), pipeline_mode=pl.Buffered(1)),
        ],
        out_specs=pl.BlockSpec((1, tq, HEADS_PER_STEP * head_dim), lambda bi, hp, i: (bi, i, hp)),
        out_shape=jax.ShapeDtypeStruct((b, s, heads * head_dim), BF16),
        scratch_shapes=[
            pltpu.VMEM((HEADS_PER_STEP, SUBLANES, tq), F32),
            pltpu.VMEM((HEADS_PER_STEP, V_ROWS, tq), F32),
            pltpu.VMEM((2, HEADS_PER_STEP, tk, tq), F32),
            pltpu.VMEM((2, HEADS_PER_STEP, SUBLANES, tq), F32),
        ],
        compiler_params=_params("parallel", "parallel", "arbitrary"),
    )(qt, k, vt)


def _rope_lane_tables(positions):
    inv_freq = ROPE_THETA ** (-jnp.arange(0, MLA_ROPE, 2, dtype=F32) / MLA_ROPE)
    lead = positions.shape
    per_row = LANES // inv_freq.shape[0]
    ang = positions.astype(F32).reshape(lead[0], -1, per_row)[..., None] * inv_freq
    ang = ang.reshape(lead[0], -1, LANES)
    cos, sin = lax.optimization_barrier((jnp.cos(ang), jnp.sin(ang)))
    cos, sin = cos.reshape(lead + (-1,)), sin.reshape(lead + (-1,))
    return _rope_group(cos, cos, lead), _rope_group(sin, sin, lead)


def kernel(x, positions, norm_ffn, norm_mix, norm_final, ffn_w_gate, ffn_w_up, ffn_w_down,
           ab_w_in, ab_b_forget, pool_w, pool_scale, ab_w_out,
           mla_w_in, mla_q_norm, mla_kv_norm, mla_w_q_b, mla_w_kv_b, mla_w_out):
    cos_l, sin_l = _rope_lane_tables(positions)
    w_gate, w_up, w_down = (w.astype(BF16) for w in (ffn_w_gate, ffn_w_up, ffn_w_down))
    h = x.astype(F32)
    for layer in range(DEPTH):
        idx = layer // 2
        h = _ffn(h, norm_ffn[layer, 0][None, :], w_gate, w_up, w_down, (layer, 0))
        g_mix = norm_mix[layer][None, :]
        if layer % 2 == 0:
            w = _prep_ab(ab_w_in[idx], ab_b_forget[idx], pool_w[idx], pool_scale[idx], ab_w_out[idx])
            y_pool, q, k, v = _ab_in(h, g_mix, w)
            y_fox = _attention(q, k, v, FOX_HEADS, 1, FOX_HEAD_DIM)
            mix = ((y_pool, w["wo_pool"]), (y_fox, w["wo_fox"]))
        else:
            w = _prep_mla(mla_w_in[idx], mla_q_norm[idx], mla_kv_norm[idx], mla_w_q_b[idx],
                          mla_w_kv_b[idx], mla_w_out[idx])
            q, k, v = _mla_in(h, g_mix, w, cos_l, sin_l)
            y = _attention(q, k, v, MLA_HEADS, CHUNK, MLA_V)
            mix = ((y, w["wo"]),)
        h = _ffn(h, norm_ffn[layer, 1][None, :], w_gate, w_up, w_down, (layer, 1), mix=mix,
                 final_g=norm_final[None, :] if layer == DEPTH - 1 else None)
    return h
```
